```python
import math
import jax
import jax.numpy as jnp
from jax import lax
import numpy as np

D_MODEL = 2048
BATCH = 4
SEQ = 4096
DEPTH = 2

GRID_W = 64
CTX_LEN = 256
HEAD_DIM = 64
Q_BLOCK = 128
ROPE_THETA = 10000.0
EPS = 1e-6
A_HEADS = 4
A_QK_DIM = 64
A_V_DIM = 128
B_HEADS = 8
B_KV_HEADS = 2
C_HEADS = 8
C_Q_RANK = 768
C_KV_RANK = 256
C_NOPE_DIM = 64
C_ROPE_DIM = 32
C_V_DIM = 64
D_HEADS = 8
D_KV_HEADS = 2
WINDOW = 128
A_Q_COLS = 2 * A_HEADS * A_QK_DIM
B_Q_COLS = B_HEADS * HEAD_DIM
C_Q_COLS = C_Q_RANK
D_Q_COLS = D_HEADS * HEAD_DIM
A_KV_COLS = 2 * A_HEADS * A_QK_DIM + A_HEADS * A_V_DIM
B_KV_COLS = 2 * B_KV_HEADS * HEAD_DIM
C_KV_COLS = C_KV_RANK + C_ROPE_DIM
D_KV_COLS = 2 * D_KV_HEADS * HEAD_DIM
Q_SIZES = (A_Q_COLS, B_Q_COLS, C_Q_COLS, D_Q_COLS)
KV_SIZES = (A_KV_COLS, B_KV_COLS, C_KV_COLS, D_KV_COLS)
Q_COLS = A_Q_COLS + B_Q_COLS + C_Q_COLS + D_Q_COLS
KV_COLS = A_KV_COLS + B_KV_COLS + C_KV_COLS + D_KV_COLS
IN_COLS = Q_COLS + KV_COLS
MIX_WIDTH = A_HEADS * A_V_DIM + B_HEADS * HEAD_DIM + C_HEADS * C_V_DIM + D_HEADS * HEAD_DIM
D_FF = 5632
N_EXPERTS = 8
TOP_K = 2
EXPERT_FF = 5632
N_DENSE = (DEPTH + 1) // 2
N_MOE = DEPTH // 2

kernel_name = 'hybrid_parallel_heads_flow_block'


def rms_norm(x, g):
    xf = x.astype(jnp.float32)
    y = xf * lax.rsqrt(jnp.mean(xf * xf, axis=-1, keepdims=True) + EPS)
    return (y * g.astype(jnp.float32)).astype(x.dtype)


def split_cols(t, sizes):
    out = []
    o = 0
    for s in sizes:
        out.append(t[..., o:o + s])
        o += s
    return out


def split_heads(t, n, d):
    return t.reshape(t.shape[0], t.shape[1], n, d)


def merge_heads(o):
    return o.reshape(o.shape[0], o.shape[1], -1)


def rope_tables(rows, cols, rot_dim):
    axis_dim = rot_dim // 2
    inv = ROPE_THETA ** (-jnp.arange(0, axis_dim, 2, dtype=jnp.float32) / axis_dim)
    ar = rows.astype(jnp.float32)[:, None] * inv[None, :]
    ac = cols.astype(jnp.float32)[:, None] * inv[None, :]
    return (jnp.cos(ar), jnp.sin(ar), jnp.cos(ac), jnp.sin(ac))


def _rotate(x, cos, sin):
    n = x.shape[-1] // 2
    x1, x2 = x[..., :n], x[..., n:]
    return jnp.concatenate([x1 * cos - x2 * sin, x2 * cos + x1 * sin], axis=-1)


def apply_rope2d(x, tables):
    shape = (x.shape[1],) + (1,) * (x.ndim - 3) + (tables[0].shape[-1],)
    cr, sr, cc, sc = (t.reshape(shape).astype(x.dtype) for t in tables)
    half = x.shape[-1] // 2
    return jnp.concatenate([_rotate(x[..., :half], cr, sr), _rotate(x[..., half:], cc, sc)], axis=-1)


def _attend_block(q, k, v, scale, sink=None, mask=None):
    s = jnp.einsum('bqgrd,bkgd->bgrqk', q, k).astype(jnp.float32) * scale
    if mask is not None:
        s = jnp.where(mask, s, -jnp.inf)
    if sink is not None:
        g, r = q.shape[2], q.shape[3]
        sk = jnp.broadcast_to(sink.astype(jnp.float32).reshape(1, g, r, 1, 1), s.shape[:-1] + (1,))
        p = jax.nn.softmax(jnp.concatenate([s, sk], axis=-1), axis=-1)[..., :-1]
    else:
        p = jax.nn.softmax(s, axis=-1)
    return jnp.einsum('bgrqk,bkgd->bqgrd', p.astype(v.dtype), v)


def dense_attention(q, k, v, scale, sink=None):
    bsz, sq, n_heads, dk = q.shape
    g = k.shape[2]
    r = n_heads // g
    nb = sq // Q_BLOCK
    qb = q.reshape(bsz, nb, Q_BLOCK, g, r, dk).swapaxes(0, 1)
    o = lax.map(lambda qi: _attend_block(qi, k, v, scale, sink), qb)
    return o.swapaxes(0, 1).reshape(bsz, sq, n_heads, v.shape[-1])


def window_attention(q, k, v, k_ctx, v_ctx, scale, sink):
    bsz, s_len, n_heads, dk = q.shape
    g = k.shape[2]
    r = n_heads // g
    nb = s_len // Q_BLOCK
    wb = WINDOW // Q_BLOCK
    band_len = (2 * wb + 1) * Q_BLOCK

    def band(t):
        tp = jnp.pad(t, ((0, 0), (WINDOW, WINDOW), (0, 0), (0, 0)))
        tb = tp.reshape(bsz, nb + 2 * wb, Q_BLOCK, g, t.shape[-1])
        return jnp.concatenate([tb[:, j:j + nb] for j in range(2 * wb + 1)], axis=2).swapaxes(0, 1)

    blk = jnp.arange(nb)[:, None, None]
    qpos = blk * Q_BLOCK + jnp.arange(Q_BLOCK)[None, :, None]
    kpos = blk * Q_BLOCK - WINDOW + jnp.arange(band_len)[None, None, :]
    band_mask = (jnp.abs(qpos - kpos) <= WINDOW) & (kpos >= 0) & (kpos < s_len)
    mask = jnp.concatenate([band_mask, jnp.ones((nb, Q_BLOCK, k_ctx.shape[1]), dtype=bool)], axis=-1)
    qb = q.reshape(bsz, nb, Q_BLOCK, g, r, dk).swapaxes(0, 1)

    def one(args):
        qi, ki, vi, mi = args
        return _attend_block(qi, jnp.concatenate([ki, k_ctx], axis=1), jnp.concatenate([vi, v_ctx], axis=1), scale, sink, mi)

    o = lax.map(one, (qb, band(k), band(v), mask))
    return o.swapaxes(0, 1).reshape(bsz, s_len, n_heads, v.shape[-1])


def diff_attention_mixer(q_lat, kv_lat, q_ctx, kv_ctx, lam_q1, lam_k1, lam_q2, lam_k2, subln, lam_init, rope):
    nk = 2 * A_HEADS * A_QK_DIM

    def qk(t):
        return t.reshape(t.shape[0], t.shape[1], 2, A_HEADS, A_QK_DIM)

    def vals(t):
        return split_heads(t, A_HEADS, A_V_DIM)

    q = apply_rope2d(qk(q_lat), rope)
    k = apply_rope2d(qk(kv_lat[..., :nk]), rope)
    v = vals(kv_lat[..., nk:])
    k_c, v_c = qk(kv_ctx[..., :nk]), vals(kv_ctx[..., nk:])
    k_all = jnp.concatenate([k_c, k], axis=1)
    v_all = jnp.concatenate([v_c, v], axis=1)
    lam = (jnp.exp(jnp.sum(lam_q1.astype(jnp.float32) * lam_k1.astype(jnp.float32)))
           - jnp.exp(jnp.sum(lam_q2.astype(jnp.float32) * lam_k2.astype(jnp.float32))) + lam_init)
    scale = A_QK_DIM ** -0.5

    def diff(qq, kk, vv):
        a1 = dense_attention(qq[:, :, 0], kk[:, :, 0], vv, scale)
        a2 = dense_attention(qq[:, :, 1], kk[:, :, 1], vv, scale)
        o = rms_norm(a1 - lam.astype(a1.dtype) * a2, subln) * (1.0 - lam_init)
        return merge_heads(o)

    o_lat = diff(q, k_all, v_all)
    o_ctx = None if q_ctx is None else diff(qk(q_ctx), k_c, v_c)
    return o_lat, o_ctx


def qknorm_gqa_mixer(q_lat, kv_lat, q_ctx, kv_ctx, q_norm, k_norm, rope):
    nk = B_KV_HEADS * HEAD_DIM

    def qproj(t):
        return rms_norm(split_heads(t, B_HEADS, HEAD_DIM), q_norm)

    def kproj(t):
        return rms_norm(split_heads(t[..., :nk], B_KV_HEADS, HEAD_DIM), k_norm)

    def vproj(t):
        return split_heads(t[..., nk:], B_KV_HEADS, HEAD_DIM)

    q = apply_rope2d(qproj(q_lat), rope)
    k = apply_rope2d(kproj(kv_lat), rope)
    v = vproj(kv_lat)
    k_c, v_c = kproj(kv_ctx), vproj(kv_ctx)
    scale = HEAD_DIM ** -0.5
    o_lat = dense_attention(q, jnp.concatenate([k_c, k], axis=1), jnp.concatenate([v_c, v], axis=1), scale)
    o_ctx = None if q_ctx is None else merge_heads(dense_attention(qproj(q_ctx), k_c, v_c, scale))
    return merge_heads(o_lat), o_ctx


def mla_mixer(q_lat, kv_lat, q_ctx, kv_ctx, q_norm, kv_norm, w_q_up, w_kv_up, rope):
    def qproj(t):
        qf = split_heads(rms_norm(t, q_norm) @ w_q_up, C_HEADS, C_NOPE_DIM + C_ROPE_DIM)
        return qf[..., :C_NOPE_DIM], qf[..., C_NOPE_DIM:]

    def kvproj(t):
        c_kv, k_rope = t[..., :C_KV_RANK], t[..., C_KV_RANK:]
        kv = split_heads(rms_norm(c_kv, kv_norm) @ w_kv_up, C_HEADS, C_NOPE_DIM + C_V_DIM)
        return kv[..., :C_NOPE_DIM], k_rope[:, :, None, :], kv[..., C_NOPE_DIM:]

    def join(nope, rope_part):
        return jnp.concatenate([nope, jnp.broadcast_to(rope_part, nope.shape[:-1] + (C_ROPE_DIM,))], axis=-1)

    qn, qr = qproj(q_lat)
    q = join(qn, apply_rope2d(qr, rope))
    kn, kr, v = kvproj(kv_lat)
    k = join(kn, apply_rope2d(kr, rope))
    kn_c, kr_c, v_c = kvproj(kv_ctx)
    k_c = join(kn_c, kr_c)
    scale = (C_NOPE_DIM + C_ROPE_DIM) ** -0.5
    o_lat = dense_attention(q, jnp.concatenate([k_c, k], axis=1), jnp.concatenate([v_c, v], axis=1), scale)
    if q_ctx is None:
        o_ctx = None
    else:
        qn_c, qr_c = qproj(q_ctx)
        o_ctx = merge_heads(dense_attention(join(qn_c, qr_c), k_c, v_c, scale))
    return merge_heads(o_lat), o_ctx


def window_sink_mixer(q_lat, kv_lat, q_ctx, kv_ctx, sink, rope):
    nk = D_KV_HEADS * HEAD_DIM
    q = apply_rope2d(split_heads(q_lat, D_HEADS, HEAD_DIM), rope)
    k = apply_rope2d(split_heads(kv_lat[..., :nk], D_KV_HEADS, HEAD_DIM), rope)
    v = split_heads(kv_lat[..., nk:], D_KV_HEADS, HEAD_DIM)
    k_c = split_heads(kv_ctx[..., :nk], D_KV_HEADS, HEAD_DIM)
    v_c = split_heads(kv_ctx[..., nk:], D_KV_HEADS, HEAD_DIM)
    scale = HEAD_DIM ** -0.5
    o_lat = window_attention(q, k, v, k_c, v_c, scale, sink)
    o_ctx = None if q_ctx is None else merge_heads(dense_attention(split_heads(q_ctx, D_HEADS, HEAD_DIM), k_c, v_c, scale, sink))
    return merge_heads(o_lat), o_ctx


def swiglu(u, w_gate, w_up, w_down):
    return (jax.nn.silu(u @ w_gate) * (u @ w_up)) @ w_down


def moe_swiglu(u, w_router, b_router, w_gate, w_up, w_down):
    logits = (u @ w_router + b_router).astype(jnp.float32)
    top_val, top_idx = lax.top_k(logits, TOP_K)
    top_w = jax.nn.softmax(top_val, axis=-1)
    gates = jnp.sum(jax.nn.one_hot(top_idx, N_EXPERTS, dtype=jnp.float32) * top_w[..., None], axis=-2)
    out = jnp.zeros_like(u)
    for e in range(N_EXPERTS):
        out = out + gates[..., e:e + 1].astype(u.dtype) * swiglu(u, w_gate[e], w_up[e], w_down[e])
    return out


def setup_inputs(seed: int = 0) -> dict:
    key = jax.random.key(seed)
    ks = iter(jax.random.split(key, 40))

    def nrm(shape, scale):
        return jax.random.normal(next(ks), shape, jnp.float32) * scale

    def gain(shape):
        return 1.0 + nrm(shape, 0.02)

    d = D_MODEL
    return {
        'x': nrm((BATCH, SEQ, d), 1.0),
        'c': nrm((BATCH, d), 1.0),
        'ctx': nrm((BATCH, CTX_LEN, d), 1.0),
        'c_ctx': nrm((d,), 1.0),
        'w_mod': nrm((DEPTH, d, 6 * d), 0.5 * d ** -0.5),
        'b_mod': nrm((DEPTH, 6 * d), 0.02),
        'g_mix': gain((DEPTH, d)),
        'g_ffn': gain((DEPTH, d)),
        'g_final': gain((d,)),
        'w_in': nrm((DEPTH, d, IN_COLS), d ** -0.5),
        'w_out': nrm((DEPTH, MIX_WIDTH, d), MIX_WIDTH ** -0.5),
        'a_lam_q1': nrm((DEPTH, A_QK_DIM), 0.1),
        'a_lam_k1': nrm((DEPTH, A_QK_DIM), 0.1),
        'a_lam_q2': nrm((DEPTH, A_QK_DIM), 0.1),
        'a_lam_k2': nrm((DEPTH, A_QK_DIM), 0.1),
        'a_subln': gain((DEPTH, A_V_DIM)),
        'b_q_norm': gain((DEPTH, HEAD_DIM)),
        'b_k_norm': gain((DEPTH, HEAD_DIM)),
        'c_q_norm': gain((DEPTH, C_Q_RANK)),
        'c_kv_norm': gain((DEPTH, C_KV_RANK)),
        'c_w_q_up': nrm((DEPTH, C_Q_RANK, C_HEADS * (C_NOPE_DIM + C_ROPE_DIM)), C_Q_RANK ** -0.5),
        'c_w_kv_up': nrm((DEPTH, C_KV_RANK, C_HEADS * (C_NOPE_DIM + C_V_DIM)), C_KV_RANK ** -0.5),
        'd_sink': nrm((DEPTH, D_HEADS), 0.5),
        'ffn_w_gate': nrm((N_DENSE, d, D_FF), d ** -0.5),
        'ffn_w_up': nrm((N_DENSE, d, D_FF), d ** -0.5),
        'ffn_w_down': nrm((N_DENSE, D_FF, d), D_FF ** -0.5),
        'moe_w_router': nrm((N_MOE, d, N_EXPERTS), d ** -0.5),
        'moe_b_router': nrm((N_MOE, N_EXPERTS), 0.01),
        'moe_w_gate': nrm((N_MOE, N_EXPERTS, d, EXPERT_FF), d ** -0.5),
        'moe_w_up': nrm((N_MOE, N_EXPERTS, d, EXPERT_FF), d ** -0.5),
        'moe_w_down': nrm((N_MOE, N_EXPERTS, EXPERT_FF, d), EXPERT_FF ** -0.5),
    }


def reference(x, c, ctx, c_ctx, w_mod, b_mod, g_mix, g_ffn, g_final, w_in, w_out,
              a_lam_q1, a_lam_k1, a_lam_q2, a_lam_k2, a_subln, b_q_norm, b_k_norm,
              c_q_norm, c_kv_norm, c_w_q_up, c_w_kv_up, d_sink,
              ffn_w_gate, ffn_w_up, ffn_w_down,
              moe_w_router, moe_b_router, moe_w_gate, moe_w_up, moe_w_down):
    n_tok = x.shape[1]
    n_rows = n_tok // GRID_W
    t = jnp.arange(n_rows * GRID_W)
    rows, cols = t // GRID_W, t % GRID_W
    rope_head = rope_tables(rows, cols, HEAD_DIM)
    rope_mla = rope_tables(rows, cols, C_ROPE_DIM)

    def ffn(l, u):
        i = l // 2
        if l % 2 == 0:
            return swiglu(u, ffn_w_gate[i], ffn_w_up[i], ffn_w_down[i])
        return moe_swiglu(u, moe_w_router[i], moe_b_router[i], moe_w_gate[i], moe_w_up[i], moe_w_down[i])

    h_lat = x
    h_ctx = ctx
    for l in range(DEPTH):
        last = l == DEPTH - 1
        lam_init = 0.8 - 0.6 * math.exp(-0.3 * l)
        n_ctx_mod = 2 * D_MODEL if last else 6 * D_MODEL
        m_lat = (jax.nn.silu(c) @ w_mod[l] + b_mod[l])[:, None, :]
        m_ctx = (jax.nn.silu(c_ctx) @ w_mod[l][:, :n_ctx_mod] + b_mod[l][:n_ctx_mod])[None, None, :]
        sh1, sc1, gt1, sh2, sc2, gt2 = jnp.split(m_lat, 6, axis=-1)
        cm = jnp.split(m_ctx, n_ctx_mod // D_MODEL, axis=-1)

        u_lat = rms_norm(h_lat, g_mix[l]) * (1 + sc1) + sh1
        u_ctx = rms_norm(h_ctx, g_mix[l]) * (1 + cm[1]) + cm[0]
        p_lat = u_lat @ w_in[l]
        qs_lat = split_cols(p_lat[..., :Q_COLS], Q_SIZES)
        kvs_lat = split_cols(p_lat[..., Q_COLS:], KV_SIZES)
        if last:
            qs_ctx = [None, None, None, None]
            kvs_ctx = split_cols(u_ctx @ w_in[l][:, Q_COLS:], KV_SIZES)
        else:
            p_ctx = u_ctx @ w_in[l]
            qs_ctx = split_cols(p_ctx[..., :Q_COLS], Q_SIZES)
            kvs_ctx = split_cols(p_ctx[..., Q_COLS:], KV_SIZES)

        oa, oa_c = diff_attention_mixer(qs_lat[0], kvs_lat[0], qs_ctx[0], kvs_ctx[0],
                                        a_lam_q1[l], a_lam_k1[l], a_lam_q2[l], a_lam_k2[l], a_subln[l], lam_init, rope_head)
        ob, ob_c = qknorm_gqa_mixer(qs_lat[1], kvs_lat[1], qs_ctx[1], kvs_ctx[1], b_q_norm[l], b_k_norm[l], rope_head)
        oc, oc_c = mla_mixer(qs_lat[2], kvs_lat[2], qs_ctx[2], kvs_ctx[2],
                             c_q_norm[l], c_kv_norm[l], c_w_q_up[l], c_w_kv_up[l], rope_mla)
        od, od_c = window_sink_mixer(qs_lat[3], kvs_lat[3], qs_ctx[3], kvs_ctx[3], d_sink[l], rope_head)

        h_lat = h_lat + gt1 * (jnp.concatenate([oa, ob, oc, od], axis=-1) @ w_out[l])
        if not last:
            h_ctx = h_ctx + cm[2] * (jnp.concatenate([oa_c, ob_c, oc_c, od_c], axis=-1) @ w_out[l])

        h_lat = h_lat + gt2 * ffn(l, rms_norm(h_lat, g_ffn[l]) * (1 + sc2) + sh2)
        if not last:
            h_ctx = h_ctx + cm[5] * ffn(l, rms_norm(h_ctx, g_ffn[l]) * (1 + cm[4]) + cm[3])

    return rms_norm(h_lat, g_final)
```

```python
import functools
import math

import numpy as np
import jax
import jax.numpy as jnp
from jax import lax
from jax.experimental import pallas as pl
from jax.experimental.pallas import tpu as pltpu

F32 = jnp.float32
BF16 = jnp.bfloat16

GRID_W = 64
HEAD_DIM = 64
ROPE_THETA = 10000.0
EPS = 1e-6
A_HEADS, A_QK_DIM, A_V_DIM = 4, 64, 128
B_HEADS, B_KV_HEADS = 8, 2
C_HEADS, C_Q_RANK, C_KV_RANK, C_NOPE_DIM, C_ROPE_DIM, C_V_DIM = 8, 768, 256, 64, 32, 64
D_HEADS, D_KV_HEADS = 8, 2
WINDOW = 128
N_EXPERTS = 8

LANES = 128
VMEM_LIMIT = 48 * 2**20

TM = 512
TR = 256
TQ = 256
TN = 512

PERM8 = (0, 4, 1, 5, 2, 6, 3, 7)
OFF_QA, OFF_QB, OFF_QC, OFF_QD = 0, 512, 1024, 1792
OFF_KA, OFF_VA, OFF_KB, OFF_VB = 2304, 2816, 3328, 3456
OFF_CKV, OFF_KD, OFF_VD, OFF_KR = 3584, 3840, 3968, 4096
P_COLS = 4608


def _cparams(sem):
    return pltpu.CompilerParams(dimension_semantics=sem, vmem_limit_bytes=VMEM_LIMIT)


def _pick(n, prefs):
    for p in prefs:
        if n % p == 0:
            return p
    return n


def _mod_kernel(c_ref, w_ref, b_ref, o_ref):
    a = c_ref[...]
    a = (a * jax.nn.sigmoid(a)).astype(BF16)
    o_ref[0] = jnp.dot(a, w_ref[0].astype(BF16), preferred_element_type=F32) + b_ref[0]


def _modulation(cc, w_mod, b_mod):
    depth, d, d6 = w_mod.shape
    rows = cc.shape[0]
    tn = _pick(d6, (1024, 512, 256, 128))
    return pl.pallas_call(
        _mod_kernel,
        out_shape=jax.ShapeDtypeStruct((depth, rows, d6), F32),
        grid=(depth, d6 // tn),
        in_specs=[
            pl.BlockSpec((rows, d), lambda l, n: (0, 0)),
            pl.BlockSpec((1, d, tn), lambda l, n: (l, 0, n)),
            pl.BlockSpec((1, 1, tn), lambda l, n: (l, 0, n)),
        ],
        out_specs=pl.BlockSpec((1, rows, tn), lambda l, n: (l, 0, n)),
        compiler_params=_cparams(("arbitrary", "arbitrary")),
        name="modulation",
    )(cc, w_mod, b_mod.reshape(depth, 1, d6))


def _normmod_kernel(x_ref, g_ref, sh_ref, sc_ref, o_ref):
    x = x_ref[...]
    ms = jnp.mean(x * x, axis=-1, keepdims=True)
    y = x * lax.rsqrt(ms + EPS) * g_ref[...]
    o_ref[...] = (y * (1.0 + sc_ref[0]) + sh_ref[0]).astype(o_ref.dtype)


def _normmod(h, g, mod, sh_blk, sc_blk, n_rows, mod_row, out_dtype):
    d = h.shape[1]
    tm = _pick(n_rows, (TM, 256, 128))
    return pl.pallas_call(
        _normmod_kernel,
        out_shape=jax.ShapeDtypeStruct((n_rows, d), out_dtype),
        grid=(n_rows // tm,),
        in_specs=[
            pl.BlockSpec((tm, d), lambda m: (m, 0)),
            pl.BlockSpec((1, d), lambda m: (0, 0)),
            pl.BlockSpec((1, 1, d), lambda m: (mod_row(m * tm), 0, sh_blk)),
            pl.BlockSpec((1, 1, d), lambda m: (mod_row(m * tm), 0, sc_blk)),
        ],
        out_specs=pl.BlockSpec((tm, d), lambda m: (m, 0)),
        compiler_params=_cparams(("arbitrary",)),
        name="normmod",
    )(h, g.reshape(1, d), mod, mod)


def _final_norm_kernel(x_ref, g_ref, o_ref):
    x = x_ref[...]
    ms = jnp.mean(x * x, axis=-1, keepdims=True)
    o_ref[...] = x * lax.rsqrt(ms + EPS) * g_ref[...]


def _mm_kernel(te_ref, nu_ref, *refs, n_x, n_g, epi):
    x_refs = refs[:n_x]
    w_refs = refs[n_x:n_x + n_x * n_g]
    pos = n_x + n_x * n_g
    if epi == "resgate":
        res_ref, gt_ref = refs[pos], refs[pos + 1]
        pos += 2
    o_ref = refs[pos]
    wc_refs = refs[pos + 1:]
    m = pl.program_id(1)
    panel_changed = (m == 0) | (te_ref[m] != te_ref[jnp.maximum(m - 1, 0)])

    @pl.when(panel_changed)
    def _():
        for w, wc in zip(w_refs, wc_refs):
            wc[...] = w[0].astype(BF16)

    @pl.when(m < nu_ref[0])
    def _():
        accs = []
        for g in range(n_g):
            acc = None
            for i in range(n_x):
                part = jnp.dot(x_refs[i][...], wc_refs[g * n_x + i][...], preferred_element_type=F32)
                acc = part if acc is None else acc + part
            accs.append(acc)
        if epi == "plain":
            out = accs[0]
        elif epi == "swiglu":
            a = accs[0]
            out = (a * jax.nn.sigmoid(a)) * accs[1]
        else:
            out = res_ref[...] + gt_ref[0] * accs[0]
        o_ref[...] = out.astype(o_ref.dtype)

    @pl.when(m >= nu_ref[0])
    def _():
        o_ref[...] = jnp.zeros_like(o_ref)


def _matmul(xs, ws, *, n_rows, out_dtype, epi="plain", tile_expert=None, n_used=None,
            res=None, mod=None, gt_blk=None, mod_row=None, tm=TM, tn=TN):
    n_x, n_g = len(xs), len(ws)
    n_cols = ws[0].shape[2]
    tn = _pick(n_cols, (tn, 256, 128))
    tm = _pick(n_rows, (tm, 256, 128))
    mt, nt = n_rows // tm, n_cols // tn
    if tile_expert is None:
        tile_expert = jnp.zeros((mt,), jnp.int32)
        n_used = jnp.full((1,), mt, jnp.int32)
    ks = [x.shape[1] for x in xs]
    in_specs = [pl.BlockSpec((tm, k), lambda n, m, te, nu: (m, 0)) for k in ks]
    args = list(xs)
    for g in range(n_g):
        for i, k in enumerate(ks):
            in_specs.append(pl.BlockSpec((1, k, tn), lambda n, m, te, nu, i=i: (te[m], i, n)))
            args.append(ws[g])
    if epi == "resgate":
        blk0 = gt_blk * (n_cols // tn)
        in_specs.append(pl.BlockSpec((tm, tn), lambda n, m, te, nu: (m, n)))
        in_specs.append(pl.BlockSpec((1, 1, tn), lambda n, m, te, nu: (mod_row(m * tm), 0, blk0 + n)))
        args += [res, mod]
    scratch = [pltpu.VMEM((k, tn), BF16) for _ in range(n_g) for k in ks]
    return pl.pallas_call(
        functools.partial(_mm_kernel, n_x=n_x, n_g=n_g, epi=epi),
        out_shape=jax.ShapeDtypeStruct((n_rows, n_cols), out_dtype),
        grid_spec=pltpu.PrefetchScalarGridSpec(
            num_scalar_prefetch=2,
            grid=(nt, mt),
            in_specs=in_specs,
            out_specs=pl.BlockSpec((tm, tn), lambda n, m, te, nu: (m, n)),
            scratch_shapes=scratch,
        ),
        compiler_params=_cparams(("arbitrary", "arbitrary")),
        name="matmul_" + epi,
    )(tile_expert, n_used, *args)


def _rope(x, cos, sin_signed, half):
    lane = lax.broadcasted_iota(jnp.int32, (x.shape[0], LANES), 1)
    first = (lane & (2 * half - 1)) < half
    blocks = []
    for j in range(x.shape[1] // LANES):
        xj = x[:, j * LANES:(j + 1) * LANES]
        blocks.append(jnp.where(first, pltpu.roll(xj, LANES - half, 1), pltpu.roll(xj, half, 1)))
    partner = blocks[0] if len(blocks) == 1 else jnp.concatenate(blocks, axis=-1)
    return x * cos + partner * sin_signed


def _group_mean_sq(x, g_ref):
    x2 = x * x
    hi = x2.astype(BF16)
    lo = (x2 - hi.astype(F32)).astype(BF16)
    g = g_ref[...]
    return (jnp.dot(hi, g, preferred_element_type=F32) + jnp.dot(lo, g, preferred_element_type=F32))


def _prep_kernel(p_ref, c64_ref, s64_ref, cq_ref, sq_ref, ck_ref, sk_ref,
                 bqn_ref, bkn_ref, cqn_ref, ckvn_ref, g512_ref, g128_ref,
                 wq_ref, wkk_ref, wkv_ref, place_ref,
                 qa_ref, qb_ref, qc_ref, qd_ref,
                 ka_ref, va_ref, kb_ref, vb_ref, kc_ref, vc_ref, kd_ref, vd_ref):
    c64, s64 = c64_ref[...], s64_ref[...]
    c128, s128 = c64[:, :LANES], s64[:, :LANES]
    qk_scale = HEAD_DIM ** -0.5

    qa_ref[...] = (_rope(p_ref[:, OFF_QA:OFF_QA + 512], c64, s64, 16) * qk_scale).astype(BF16)
    ka_ref[0] = _rope(p_ref[:, OFF_KA:OFF_KA + 512], c64, s64, 16).astype(BF16)
    va_ref[0] = p_ref[:, OFF_VA:OFF_VA + 512].astype(BF16)

    xb = p_ref[:, OFF_QB:OFF_QB + 512]
    yb = xb * lax.rsqrt(_group_mean_sq(xb, g512_ref) + EPS) * bqn_ref[...]
    qb_ref[...] = (_rope(yb, c64, s64, 16) * qk_scale).astype(BF16)
    xk = p_ref[:, OFF_KB:OFF_KB + 128]
    yk = xk * lax.rsqrt(_group_mean_sq(xk, g128_ref) + EPS) * bkn_ref[...]
    kb_ref[0] = _rope(yk, c128, s128, 16).astype(BF16)
    vb_ref[0] = p_ref[:, OFF_VB:OFF_VB + 128].astype(BF16)

    xq = p_ref[:, OFF_QC:OFF_QC + C_Q_RANK]
    yq = xq * lax.rsqrt(jnp.mean(xq * xq, axis=-1, keepdims=True) + EPS) * cqn_ref[...]
    qf = jnp.dot(yq.astype(BF16), wq_ref[...], preferred_element_type=F32)
    qc_ref[...] = _rope(qf, cq_ref[...], sq_ref[...], 8).astype(BF16)
    xc = p_ref[:, OFF_CKV:OFF_CKV + C_KV_RANK]
    yc = (xc * lax.rsqrt(jnp.mean(xc * xc, axis=-1, keepdims=True) + EPS) * ckvn_ref[...]).astype(BF16)
    kr = _rope(p_ref[:, OFF_KR:OFF_KR + LANES], ck_ref[...], sk_ref[...], 8).astype(BF16)
    kc = (jnp.dot(yc, wkk_ref[...], preferred_element_type=F32)
          + jnp.dot(kr, place_ref[...], preferred_element_type=F32))
    kc_ref[0] = kc.astype(BF16)
    vc_ref[0] = jnp.dot(yc, wkv_ref[...], preferred_element_type=F32).astype(BF16)

    qd_ref[...] = (_rope(p_ref[:, OFF_QD:OFF_QD + 512], c64, s64, 16) * qk_scale).astype(BF16)
    kd_ref[0] = _rope(p_ref[:, OFF_KD:OFF_KD + 128], c128, s128, 16).astype(BF16)
    vd_ref[0] = p_ref[:, OFF_VD:OFF_VD + 128].astype(BF16)


def _prep(p, tables, consts, dims):
    bsz, s_len, n_ctx = dims
    n_rows = p.shape[0]
    sk = n_ctx + s_len
    tr = _pick(math.gcd(s_len, n_ctx), (TR, 128))
    n_lat_t, lat_pb, ctx_pb = bsz * s_len // tr, s_len // tr, n_ctx // tr

    def is_lat(t):
        return t < n_lat_t

    def tbl_idx(t):
        return jnp.where(is_lat(t), t % lat_pb, lat_pb)

    def kv_b(t):
        return jnp.where(is_lat(t), t // lat_pb, (t - n_lat_t) // ctx_pb)

    def kv_j(t):
        return jnp.where(is_lat(t), ctx_pb + t % lat_pb, (t - n_lat_t) % ctx_pb)

    def row_spec(w):
        return pl.BlockSpec((tr, w), lambda t: (t, 0))

    def tbl_spec(w):
        return pl.BlockSpec((tr, w), lambda t: (tbl_idx(t), 0))

    def const_spec(a):
        return pl.BlockSpec(a.shape, lambda t: (0,) * a.ndim)

    def kv_spec(w):
        return pl.BlockSpec((1, tr, w), lambda t: (kv_b(t), kv_j(t), 0))

    q_widths = (512, 512, 1024, 512)
    kv_widths = (512, 512, 128, 128, 1024, 512, 128, 128)
    out_shape = ([jax.ShapeDtypeStruct((n_rows, w), BF16) for w in q_widths]
                 + [jax.ShapeDtypeStruct((bsz, sk, w), BF16) for w in kv_widths])
    out_specs = [row_spec(w) for w in q_widths] + [kv_spec(w) for w in kv_widths]
    in_specs = ([row_spec(P_COLS)] + [tbl_spec(t.shape[1]) for t in tables]
                + [const_spec(a) for a in consts])
    return pl.pallas_call(
        _prep_kernel,
        out_shape=out_shape,
        grid=(n_rows // tr,),
        in_specs=in_specs,
        out_specs=out_specs,
        compiler_params=_cparams(("arbitrary",)),
        name="attn_prep",
    )(p, *tables, *consts)


def _scores(q, k):
    return lax.dot_general(q, k, (((1,), (1,)), ((), ())), preferred_element_type=F32)


def _attend(q, k, v, scale=None, sink=None):
    s = _scores(q, k)
    if scale is not None:
        s = s * scale
    m = jnp.max(s, axis=-1, keepdims=True)
    if sink is not None:
        m = jnp.maximum(m, sink)
    e = jnp.exp(s - m)
    l = jnp.sum(e, axis=-1, keepdims=True)
    if sink is not None:
        l = l + jnp.exp(sink - m)
    return jnp.dot(e.astype(BF16), v, preferred_element_type=F32) / l


def _half_masks(shape):
    lane = lax.broadcasted_iota(jnp.int32, shape, 1)
    lo = lane < HEAD_DIM
    return lo, jnp.logical_not(lo)


def _lat_or_ctx(n_q_lat, nk_all, n_ctx, with_ctx, body):
    if not with_ctx:
        body(nk_all, True)
        return
    i = pl.program_id(2)

    @pl.when(i < n_q_lat)
    def _():
        body(nk_all, True)

    @pl.when(i >= n_q_lat)
    def _():
        body(n_ctx, False)


def _pair_out(o_ref, outs):
    lo, _ = _half_masks(outs[0].shape)
    o_ref[...] = jnp.where(lo, outs[0], outs[1]).astype(o_ref.dtype)


def _gqa_kernel(q_ref, k_ref, v_ref, o_ref, *, cfg):
    def body(nk, _):
        q, k, v = q_ref[...], k_ref[0, :nk, :], v_ref[0, :nk, :]
        masks = _half_masks(q.shape)
        _pair_out(o_ref, [_attend(jnp.where(masks[z], q, jnp.zeros_like(q)), k, v) for z in range(2)])

    _lat_or_ctx(*cfg, body)


def _mla_kernel(q_ref, k_ref, v_ref, o_ref, *, scale, cfg):
    def body(nk, _):
        v = v_ref[0, :nk, :]
        _pair_out(o_ref, [_attend(q_ref[:, z * LANES:(z + 1) * LANES],
                                  k_ref[0, :nk, z * LANES:(z + 1) * LANES], v, scale=scale)
                          for z in range(2)])

    _lat_or_ctx(*cfg, body)


def _diff_kernel(lam_ref, subln_ref, q0_ref, q1_ref, k0_ref, k1_ref, v_ref, o_ref, *, lam_init, cfg):
    def body(nk, _):
        t = lam_ref[...]
        lam = (jnp.exp(jnp.sum(t[0:1] * t[1:2], axis=-1, keepdims=True))
               - jnp.exp(jnp.sum(t[2:3] * t[3:4], axis=-1, keepdims=True)) + lam_init)
        q0, q1, k0, k1 = q0_ref[...], q1_ref[...], k0_ref[0, :nk, :], k1_ref[0, :nk, :]
        masks = _half_masks(q0.shape)
        for z in range(2):
            vz = v_ref[0, :nk, z * A_V_DIM:(z + 1) * A_V_DIM]
            a1 = _attend(jnp.where(masks[z], q0, jnp.zeros_like(q0)), k0, vz)
            a2 = _attend(jnp.where(masks[z], q1, jnp.zeros_like(q1)), k1, vz)
            d = a1 - lam * a2
            y = d * lax.rsqrt(jnp.mean(d * d, axis=-1, keepdims=True) + EPS) * subln_ref[...]
            o_ref[:, z * A_V_DIM:(z + 1) * A_V_DIM] = (y * (1.0 - lam_init)).astype(o_ref.dtype)

    _lat_or_ctx(*cfg, body)


def _window_kernel(sink_ref, q_ref, k_ref, v_ref, o_ref, *, s_len, tq, cfg):
    n_ctx = cfg[2]
    band = tq + 2 * WINDOW
    pair = pl.program_id(1)

    def body(_, is_lat):
        q = q_ref[...]
        k_c, v_c = k_ref[0, :n_ctx, :], v_ref[0, :n_ctx, :]
        masks = _half_masks(q.shape)
        if is_lat:
            i = pl.program_id(2)
            start = pl.multiple_of(jnp.clip(i * tq - WINDOW, 0, s_len - band), LANES)
            row0 = pl.multiple_of(n_ctx + start, LANES)
            k_b = k_ref[0, pl.ds(row0, band), :]
            v_b = v_ref[0, pl.ds(row0, band), :]
            qpos = i * tq + lax.broadcasted_iota(jnp.int32, (tq, band), 0)
            kpos = start + lax.broadcasted_iota(jnp.int32, (tq, band), 1)
            in_band = jnp.abs(qpos - kpos) <= WINDOW
        outs = []
        for z in range(2):
            qz = jnp.where(masks[z], q, jnp.zeros_like(q))
            sink = sink_ref[2 * pair + z]
            if not is_lat:
                outs.append(_attend(qz, k_c, v_c, sink=sink))
                continue
            s_c = _scores(qz, k_c)
            s_b = jnp.where(in_band, _scores(qz, k_b), -jnp.inf)
            m = jnp.maximum(jnp.maximum(jnp.max(s_c, axis=-1, keepdims=True),
                                        jnp.max(s_b, axis=-1, keepdims=True)), sink)
            e_c, e_b = jnp.exp(s_c - m), jnp.exp(s_b - m)
            l = (jnp.sum(e_c, axis=-1, keepdims=True) + jnp.sum(e_b, axis=-1, keepdims=True)
                 + jnp.exp(sink - m))
            o = (jnp.dot(e_c.astype(BF16), v_c, preferred_element_type=F32)
                 + jnp.dot(e_b.astype(BF16), v_b, preferred_element_type=F32))
            outs.append(o / l)
        _pair_out(o_ref, outs)

    _lat_or_ctx(*cfg, body)


def _mixers(ops, dims, params, lam_init, with_ctx):
    bsz, s_len, n_ctx = dims
    qa, qb, qc, qd, ka, va, kb, vb, kc, vc, kd, vd = ops
    lam_vecs, subln, sink_perm = params
    tq = _pick(math.gcd(s_len, n_ctx), (TQ, 128))
    n_lat = bsz * s_len
    out_rows = n_lat + bsz * n_ctx if with_ctx else n_lat
    sk_all = n_ctx + s_len
    n_q_lat, n_q_ctx = s_len // tq, n_ctx // tq
    n_q = n_q_lat + n_q_ctx if with_ctx else n_q_lat
    cfg = (n_q_lat, sk_all, n_ctx, with_ctx)

    def q_row(b, i):
        return jnp.where(i < n_q_lat, b * n_q_lat + i, n_lat // tq + b * n_q_ctx + (i - n_q_lat))

    def qspec(w, col):
        return pl.BlockSpec((tq, w), lambda b, p, i: (q_row(b, i), col(p)))

    def kvspec(w, col):
        return pl.BlockSpec((1, sk_all, w), lambda b, p, i: (b, 0, col(p)))

    def const_spec(a):
        return pl.BlockSpec(a.shape, lambda b, p, i: (0,) * a.ndim)

    def call(kernel, pairs, in_specs, args, out_block_w, name):
        return pl.pallas_call(
            kernel,
            out_shape=jax.ShapeDtypeStruct((out_rows, 512), BF16),
            grid=(bsz, pairs, n_q),
            in_specs=in_specs,
            out_specs=pl.BlockSpec((tq, out_block_w), lambda b, p, i: (q_row(b, i), p)),
            compiler_params=_cparams(("arbitrary", "arbitrary", "arbitrary")),
            name=name,
        )(*args)

    oa = call(functools.partial(_diff_kernel, lam_init=lam_init, cfg=cfg), 2,
              [const_spec(lam_vecs), const_spec(subln),
               qspec(128, lambda p: p), qspec(128, lambda p: 2 + p),
               kvspec(128, lambda p: p), kvspec(128, lambda p: 2 + p), kvspec(256, lambda p: p)],
              [lam_vecs, subln, qa, qa, ka, ka, va], 256, "attn_diff")
    ob = call(functools.partial(_gqa_kernel, cfg=cfg), 4,
              [qspec(128, lambda p: p), kvspec(128, lambda p: 0), kvspec(128, lambda p: 0)],
              [qb, kb, vb], 128, "attn_qknorm")
    oc = call(functools.partial(_mla_kernel, scale=(C_NOPE_DIM + C_ROPE_DIM) ** -0.5, cfg=cfg), 4,
              [qspec(256, lambda p: p), kvspec(256, lambda p: p), kvspec(128, lambda p: p)],
              [qc, kc, vc], 128, "attn_mla")
    od = call(functools.partial(_window_kernel, s_len=s_len, tq=tq, cfg=cfg), 4,
              [pl.BlockSpec(memory_space=pltpu.SMEM),
               qspec(128, lambda p: p), kvspec(128, lambda p: 0), kvspec(128, lambda p: 0)],
              [sink_perm, qd, kd, vd], 128, "attn_window")
    return oa, ob, oc, od


def _router_kernel(u_ref, w_ref, b_ref, sel_ref, idx_ref, gw_ref):
    u, w = u_ref[...], w_ref[...]
    u_hi = u.astype(BF16)
    u_lo = (u - u_hi.astype(F32)).astype(BF16)
    w_hi = w.astype(BF16)
    w_lo = (w - w_hi.astype(F32)).astype(BF16)
    logits = (_scores(w_hi, u_hi) + _scores(w_hi, u_lo) + _scores(w_lo, u_hi) + _scores(w_lo, u_lo)
              + b_ref[...])
    ids = lax.broadcasted_iota(jnp.int32, logits.shape, 0).astype(F32)
    m1 = jnp.max(logits, axis=0, keepdims=True)
    i1 = jnp.min(jnp.where(logits == m1, ids, float(N_EXPERTS)), axis=0, keepdims=True)
    first = ids == i1
    rest = jnp.where(first, -jnp.inf, logits)
    m2 = jnp.max(rest, axis=0, keepdims=True)
    i2 = jnp.min(jnp.where(rest == m2, ids, float(N_EXPERTS)), axis=0, keepdims=True)
    second = ids == i2
    e = jnp.exp(m2 - m1)
    w1 = 1.0 / (1.0 + e)
    w2 = e / (1.0 + e)
    sel_ref[...] = jnp.where(first | second, 1, 0).astype(jnp.int32)
    idx_ref[...] = jnp.where(ids == 0.0, i1, jnp.where(ids == 1.0, i2, 0.0)).astype(jnp.int32)
    gw_ref[...] = jnp.where(ids == 0.0, w1, jnp.where(ids == 1.0, w2, 0.0))


def _router(u, w_router_t, b_router):
    n_tok, d = u.shape
    tm = _pick(n_tok, (TM, 256, 128))
    outs = [jax.ShapeDtypeStruct((N_EXPERTS, n_tok), dt) for dt in (jnp.int32, jnp.int32, F32)]
    return pl.pallas_call(
        _router_kernel,
        out_shape=outs,
        grid=(n_tok // tm,),
        in_specs=[pl.BlockSpec((tm, d), lambda m: (m, 0)),
                  pl.BlockSpec((N_EXPERTS, d), lambda m: (0, 0)),
                  pl.BlockSpec((N_EXPERTS, 1), lambda m: (0, 0))],
        out_specs=[pl.BlockSpec((N_EXPERTS, tm), lambda m: (0, m))] * 3,
        compiler_params=_cparams(("arbitrary",)),
        name="router",
    )(u, w_router_t, b_router.reshape(N_EXPERTS, 1))


def _row_copy(src, dst, sem, src_row, dst_row):
    return pltpu.make_async_copy(src.at[pl.ds(src_row, 1)], dst.at[pl.ds(dst_row, 1)], sem)


def _gather_kernel(tok_ref, src_ref, o_ref, buf, sem):
    n = buf.shape[0]

    def start(r, carry):
        _row_copy(src_ref, buf, sem, tok_ref[0, 0, r], r).start()
        return carry

    def wait(r, carry):
        _row_copy(src_ref, buf, sem, 0, r).wait()
        return carry

    lax.fori_loop(0, n, start, 0)
    lax.fori_loop(0, n, wait, 0)
    o_ref[...] = buf[...].astype(o_ref.dtype)


def _gather_rows(src, tok, tg):
    n_rows = tok.shape[0]
    d = src.shape[1]
    return pl.pallas_call(
        _gather_kernel,
        out_shape=jax.ShapeDtypeStruct((n_rows, d), BF16),
        grid=(n_rows // tg,),
        in_specs=[pl.BlockSpec((1, 1, tg), lambda i: (i, 0, 0), memory_space=pltpu.SMEM),
                  pl.BlockSpec(memory_space=pl.ANY)],
        out_specs=pl.BlockSpec((tg, d), lambda i: (i, 0)),
        scratch_shapes=[pltpu.VMEM((tg, d), F32), pltpu.SemaphoreType.DMA],
        compiler_params=_cparams(("arbitrary",)),
        name="moe_gather",
    )(tok.reshape(n_rows // tg, 1, tg), src)


def _combine_kernel(pos_ref, y_ref, h_ref, gw_ref, gt_ref, gf_ref, o_ref, buf0, buf1, sem):
    n = buf0.shape[0]

    def start(r, carry):
        _row_copy(y_ref, buf0, sem, pos_ref[0, 0, r], r).start()
        _row_copy(y_ref, buf1, sem, pos_ref[0, 1, r], r).start()
        return carry

    def wait(r, carry):
        _row_copy(y_ref, buf0, sem, 0, r).wait()
        _row_copy(y_ref, buf1, sem, 0, r).wait()
        return carry

    lax.fori_loop(0, n, start, 0)
    lax.fori_loop(0, n, wait, 0)
    gw = gw_ref[...]
    moe = gw[:, 0:1] * buf0[...] + gw[:, 1:2] * buf1[...]
    x = h_ref[...] + gt_ref[0] * moe
    ms = jnp.mean(x * x, axis=-1, keepdims=True)
    o_ref[...] = x * lax.rsqrt(ms + EPS) * gf_ref[...]


def _combine(y, h, pos, gw_t, mod, gt_blk, mod_row, g_final, n_tok, tc):
    d = h.shape[1]
    return pl.pallas_call(
        _combine_kernel,
        out_shape=jax.ShapeDtypeStruct((n_tok, d), F32),
        grid=(n_tok // tc,),
        in_specs=[pl.BlockSpec((1, 2, tc), lambda i: (i, 0, 0), memory_space=pltpu.SMEM),
                  pl.BlockSpec(memory_space=pl.ANY),
                  pl.BlockSpec((tc, d), lambda i: (i, 0)),
                  pl.BlockSpec((tc, N_EXPERTS), lambda i: (i, 0)),
                  pl.BlockSpec((1, 1, d), lambda i: (mod_row(i * tc), 0, gt_blk)),
                  pl.BlockSpec((1, d), lambda i: (0, 0))],
        out_specs=pl.BlockSpec((tc, d), lambda i: (i, 0)),
        scratch_shapes=[pltpu.VMEM((tc, d), F32), pltpu.VMEM((tc, d), F32), pltpu.SemaphoreType.DMA],
        compiler_params=_cparams(("arbitrary",)),
        name="moe_combine",
    )(pos, y, h, gw_t, mod, g_final.reshape(1, d))


def _dispatch_plan(sel, idx, tm, n_slots):
    n_tok = sel.shape[1]
    counts = jnp.sum(sel, axis=1)
    padded = ((counts + tm - 1) // tm) * tm
    ends = jnp.cumsum(padded)
    offs = ends - padded
    pos = offs[:, None] + jnp.cumsum(sel, axis=1) - sel
    rows = jnp.where(sel > 0, pos, n_slots).reshape(-1)
    toks = jnp.tile(jnp.arange(n_tok, dtype=jnp.int32), N_EXPERTS)
    tok_of_row = jnp.zeros((n_slots,), jnp.int32).at[rows].set(toks, mode="drop")
    pos0 = jnp.take_along_axis(pos, idx[0:1], axis=0)[0]
    pos1 = jnp.take_along_axis(pos, idx[1:2], axis=0)[0]
    tile_start = jnp.arange(n_slots // tm, dtype=jnp.int32) * tm
    tile_expert = jnp.minimum(jnp.sum(ends[None, :] <= tile_start[:, None], axis=1), N_EXPERTS - 1)
    n_used = (ends[-1:] // tm).astype(jnp.int32)
    return tok_of_row, pos0.astype(jnp.int32), pos1.astype(jnp.int32), tile_expert.astype(jnp.int32), n_used


def _permute_in_proj(w):
    kva0 = 2304
    kvb0 = kva0 + 1024
    kvc0 = kvb0 + 256
    kvd0 = kvc0 + C_KV_RANK + C_ROPE_DIM
    n_src = kvd0 + 256

    def heads(base):
        return [w[:, base + h * HEAD_DIM:base + (h + 1) * HEAD_DIM] for h in PERM8]

    pieces = ([w[:, 0:512]] + heads(512) + [w[:, 1024:1792]] + heads(1792)
              + [w[:, kva0:kva0 + 1024], w[:, kvb0:kvb0 + 256], w[:, kvc0:kvc0 + C_KV_RANK],
                 w[:, kvd0:kvd0 + 256], w[:, kvc0 + C_KV_RANK:kvd0],
                 jnp.zeros((w.shape[0], P_COLS - n_src), w.dtype)])
    return jnp.concatenate(pieces, axis=1)


def _permute_out_proj(w):
    def heads(base):
        return [w[base + h * HEAD_DIM:base + (h + 1) * HEAD_DIM] for h in PERM8]

    return jnp.concatenate([w[0:512]] + heads(512) + [w[1024:1536]] + heads(1536), axis=0)


def _rope_tables(s_len, pad_rows):
    t = jnp.arange(s_len)
    rows, cols = (t // GRID_W).astype(F32), (t % GRID_W).astype(F32)

    def axis_tables(rot_dim):
        axis_dim = rot_dim // 2
        inv = ROPE_THETA ** (-jnp.arange(0, axis_dim, 2, dtype=F32) / axis_dim)
        ar, ac = rows[:, None] * inv[None, :], cols[:, None] * inv[None, :]
        cos = jnp.concatenate([jnp.cos(ar), jnp.cos(ar), jnp.cos(ac), jnp.cos(ac)], axis=1)
        sin = jnp.concatenate([-jnp.sin(ar), jnp.sin(ar), -jnp.sin(ac), jnp.sin(ac)], axis=1)
        return cos, sin

    def with_identity(cos, sin):
        w = cos.shape[1]
        return (jnp.concatenate([cos, jnp.ones((pad_rows, w), F32)], axis=0),
                jnp.concatenate([sin, jnp.zeros((pad_rows, w), F32)], axis=0))

    c64, s64 = axis_tables(HEAD_DIM)
    c64, s64 = jnp.tile(c64, (1, 8)), jnp.tile(s64, (1, 8))
    c32, s32 = axis_tables(C_ROPE_DIM)
    ones, zeros = jnp.ones((s_len, 1), F32), jnp.zeros((s_len, 1), F32)
    cq = jnp.tile(jnp.concatenate([jnp.tile(ones, (1, 64)), c32, jnp.tile(ones, (1, 32))], axis=1), (1, 8))
    sq = jnp.tile(jnp.concatenate([jnp.tile(zeros, (1, 64)), s32, jnp.tile(zeros, (1, 32))], axis=1), (1, 8))
    ck = jnp.concatenate([c32, jnp.tile(ones, (1, 96))], axis=1)
    sk = jnp.concatenate([s32, jnp.tile(zeros, (1, 96))], axis=1)
    out = []
    for c, s in ((c64, s64), (cq, sq), (ck, sk)):
        out += list(with_identity(c, s))
    return out


def _mla_weights(w_q_up, w_kv_up):
    qd = C_NOPE_DIM + C_ROPE_DIM
    wq = jnp.pad(w_q_up.reshape(C_Q_RANK, C_HEADS, qd), ((0, 0), (0, 0), (0, LANES - qd)))
    wkv = w_kv_up.reshape(C_KV_RANK, C_HEADS, C_NOPE_DIM + C_V_DIM)
    wkk = jnp.pad(wkv[:, :, :C_NOPE_DIM], ((0, 0), (0, 0), (0, LANES - C_NOPE_DIM)))
    wkv_v = wkv[:, :, C_NOPE_DIM:]
    return (wq.reshape(C_Q_RANK, C_HEADS * LANES).astype(BF16),
            wkk.reshape(C_KV_RANK, C_HEADS * LANES).astype(BF16),
            wkv_v.reshape(C_KV_RANK, C_HEADS * C_V_DIM).astype(BF16))


def _static_mats():
    g = (np.arange(512)[:, None] // HEAD_DIM == np.arange(512)[None, :] // HEAD_DIM) / HEAD_DIM
    place = np.zeros((LANES, C_HEADS * LANES), np.float32)
    for h in range(C_HEADS):
        place[np.arange(C_ROPE_DIM), h * LANES + C_NOPE_DIM + np.arange(C_ROPE_DIM)] = 1.0
    return (jnp.asarray(g, BF16), jnp.asarray(g[:128, :128], BF16), jnp.asarray(place, BF16))


def kernel(x, c, ctx, c_ctx, w_mod, b_mod, g_mix, g_ffn, g_final, w_in, w_out, a_lam_q1, a_lam_k1, a_lam_q2, a_lam_k2, a_subln, b_q_norm, b_k_norm, c_q_norm, c_kv_norm, c_w_q_up, c_w_kv_up, d_sink, ffn_w_gate, ffn_w_up, ffn_w_down, moe_w_router, moe_b_router, moe_w_gate, moe_w_up, moe_w_down):
    bsz, s_len, d = x.shape
    n_ctx = ctx.shape[1]
    depth = w_mod.shape[0]
    n_lat = bsz * s_len
    n_all = n_lat + bsz * n_ctx
    dims = (bsz, s_len, n_ctx)

    def mod_row(row0):
        return jnp.where(row0 < n_lat, row0 // s_len, bsz)

    mod_rows = -(-(bsz + 1) // 8) * 8
    cc = jnp.zeros((mod_rows, d), F32).at[:bsz].set(c).at[bsz].set(c_ctx)
    mods = _modulation(cc, w_mod, b_mod)

    tr = _pick(math.gcd(s_len, n_ctx), (TR, 128))
    tables = _rope_tables(s_len, tr)
    g512, g128, place = _static_mats()

    h = jnp.concatenate([x.reshape(n_lat, d), ctx.reshape(bsz * n_ctx, d)], axis=0)
    for l in range(depth):
        last = l == depth - 1
        lam_init = 0.8 - 0.6 * math.exp(-0.3 * l)
        n_rows = n_lat if last else n_all
        mod = mods[l].reshape(mod_rows, 1, 6 * d)

        u = _normmod(h, g_mix[l], mod, 0, 1, n_all, mod_row, BF16)
        p = _matmul([u], [_permute_in_proj(w_in[l])[None]], n_rows=n_all, out_dtype=F32)
        wq, wkk, wkv_v = _mla_weights(c_w_q_up[l], c_w_kv_up[l])
        consts = [jnp.tile(b_q_norm[l], 8)[None], jnp.tile(b_k_norm[l], 2)[None],
                  c_q_norm[l][None], c_kv_norm[l][None], g512, g128, wq, wkk, wkv_v, place]
        ops = _prep(p, tables, consts, dims)
        lam_vecs = jnp.stack([a_lam_q1[l], a_lam_k1[l], a_lam_q2[l], a_lam_k2[l]])
        sink_perm = d_sink[l][np.array(PERM8)]
        mix = _mixers(ops, dims, (lam_vecs, a_subln[l][None], sink_perm), lam_init, not last)
        h = _matmul(list(mix), [_permute_out_proj(w_out[l])[None]], n_rows=n_rows, out_dtype=F32, epi="resgate",
                    res=h, mod=mod, gt_blk=2, mod_row=mod_row)

        i = l // 2
        if l % 2 == 0:
            u2 = _normmod(h, g_ffn[l], mod, 3, 4, n_rows, mod_row, BF16)
            mid = _matmul([u2], [ffn_w_gate[i][None], ffn_w_up[i][None]], n_rows=n_rows,
                          out_dtype=BF16, epi="swiglu")
            h = _matmul([mid], [ffn_w_down[i][None]], n_rows=n_rows, out_dtype=F32, epi="resgate",
                        res=h, mod=mod, gt_blk=5, mod_row=mod_row)
        else:
            if not last:
                raise NotImplementedError("expert layers are only supported as the last layer")
            u2 = _normmod(h, g_ffn[l], mod, 3, 4, n_rows, mod_row, F32)
            sel, idx, gw = _router(u2, moe_w_router[i].T, moe_b_router[i])
            tm = _pick(n_rows, (TM, 256, 128))
            n_slots = 2 * n_rows + N_EXPERTS * tm
            tok_of_row, pos0, pos1, tile_expert, n_used = _dispatch_plan(sel, idx, tm, n_slots)
            xs = _gather_rows(u2, tok_of_row, tm)
            mid = _matmul([xs], [moe_w_gate[i], moe_w_up[i]], n_rows=n_slots, out_dtype=BF16,
                          epi="swiglu", tile_expert=tile_expert, n_used=n_used)
            y = _matmul([mid], [moe_w_down[i]], n_rows=n_slots, out_dtype=F32,
                        tile_expert=tile_expert, n_used=n_used)
            tc = _pick(s_len, (256, 128))
            pos = jnp.stack([pos0.reshape(n_rows // tc, tc), pos1.reshape(n_rows // tc, tc)], axis=1)
            out = _combine(y, h, pos, gw.T, mod, 5, mod_row, g_final, n_rows, tc)
            return out.reshape(bsz, s_len, d)

    tm = _pick(n_lat, (TM, 256, 128))
    out = pl.pallas_call(
        _final_norm_kernel,
        out_shape=jax.ShapeDtypeStruct((n_lat, d), F32),
        grid=(n_lat // tm,),
        in_specs=[pl.BlockSpec((tm, d), lambda m: (m, 0)), pl.BlockSpec((1, d), lambda m: (0, 0))],
        out_specs=pl.BlockSpec((tm, d), lambda m: (m, 0)),
        compiler_params=_cparams(("arbitrary",)),
        name="final_norm",
    )(h, g_final.reshape(1, d))
    return out.reshape(bsz, s_len, d)
```

```python
import functools
import math

import numpy as np
import jax
import jax.numpy as jnp
from jax import lax
from jax.experimental import pallas as pl
from jax.experimental.pallas import tpu as pltpu

F32 = jnp.float32
BF16 = jnp.bfloat16

GRID_W = 64
HEAD_DIM = 64
ROPE_THETA = 10000.0
EPS = 1e-6
A_HEADS, A_QK_DIM, A_V_DIM = 4, 64, 128
B_HEADS, B_KV_HEADS = 8, 2
C_HEADS, C_Q_RANK, C_KV_RANK, C_NOPE_DIM, C_ROPE_DIM, C_V_DIM = 8, 768, 256, 64, 32, 64
D_HEADS, D_KV_HEADS = 8, 2
WINDOW = 128
N_EXPERTS = 8
LOG2E = 1.4426950408889634

LANES = 128
BF16_SUBLANES = 16
VMEM_LIMIT = 48 * 2**20
VMEM_LIMIT_BIG = 56 * 2**20

TM = 512
TM_X = 1024
TN = 512
TR = 256
TQ = 512
TQ_WIN = 256
TC = 256
DMA_UNROLL = 8

OFF_QA, OFF_QB, OFF_QC, OFF_QD = 0, 512, 1024, 1792
OFF_KA, OFF_VA, OFF_KB, OFF_VB = 2304, 2816, 3328, 3456
OFF_CKV, OFF_KR, OFF_KD, OFF_VD = 3584, 3840, 3872, 4000
IN_COLS = 4128


def _cparams(sem, vmem=VMEM_LIMIT):
    return pltpu.CompilerParams(dimension_semantics=sem, vmem_limit_bytes=vmem)


def _pick(n, prefs):
    for p in prefs:
        if n % p == 0:
            return p
    return n


def _row_tile(n_rows, s_len, prefs):
    for p in prefs:
        if n_rows % p == 0 and s_len % p == 0:
            return p
    raise ValueError("no row tile fits")


def _rms_modulate(x, g, sc, sh):
    ms = jnp.mean(x * x, axis=-1, keepdims=True)
    return (x * lax.rsqrt(ms + EPS) * g) * (1.0 + sc) + sh


def _mod_kernel(c_ref, w_ref, b_ref, o_ref):
    a = c_ref[...]
    a = (a * jax.nn.sigmoid(a)).astype(BF16)
    o_ref[0] = jnp.dot(a, w_ref[0].astype(BF16), preferred_element_type=F32) + b_ref[0]


def _modulation(cc, w_mod, b_mod):
    depth, d, d6 = w_mod.shape
    rows = cc.shape[0]
    tn = _pick(d6, (1024, 512, 256, 128))
    return pl.pallas_call(
        _mod_kernel,
        out_shape=jax.ShapeDtypeStruct((depth, rows, d6), F32),
        grid=(depth, d6 // tn),
        in_specs=[
            pl.BlockSpec((rows, d), lambda l, n: (0, 0)),
            pl.BlockSpec((1, d, tn), lambda l, n: (l, 0, n)),
            pl.BlockSpec((1, 1, tn), lambda l, n: (l, 0, n)),
        ],
        out_specs=pl.BlockSpec((1, rows, tn), lambda l, n: (l, 0, n)),
        compiler_params=_cparams(("arbitrary", "arbitrary")),
        name="modulation",
    )(cc, w_mod, b_mod.reshape(depth, 1, d6))


def _nm_mm_kernel(h_ref, g_ref, sh_ref, sc_ref, *refs, n_g):
    w_refs, o_ref, u_ref = refs[:n_g], refs[n_g], refs[n_g + 1]

    @pl.when(pl.program_id(1) == 0)
    def _():
        u_ref[...] = _rms_modulate(h_ref[...], g_ref[...], sc_ref[0], sh_ref[0]).astype(BF16)

    u = u_ref[...]
    accs = [jnp.dot(u, w[...].astype(BF16), preferred_element_type=F32) for w in w_refs]
    out = accs[0] if n_g == 1 else (accs[0] * jax.nn.sigmoid(accs[0])) * accs[1]
    o_ref[...] = out.astype(o_ref.dtype)


def _nm_matmul(h, g, mod, sh_blk, sc_blk, ws, *, n_rows, s_len, mod_row, out_dtype):
    d = h.shape[1]
    n_cols = ws[0].shape[1]
    tm = _row_tile(n_rows, s_len, (TM_X, 512, 256, 128))
    tn = TN
    in_specs = [
        pl.BlockSpec((tm, d), lambda m, n: (m, 0)),
        pl.BlockSpec((1, d), lambda m, n: (0, 0)),
        pl.BlockSpec((1, 1, d), lambda m, n: (mod_row(m * tm), 0, sh_blk)),
        pl.BlockSpec((1, 1, d), lambda m, n: (mod_row(m * tm), 0, sc_blk)),
    ] + [pl.BlockSpec((d, tn), lambda m, n: (0, n)) for _ in ws]
    return pl.pallas_call(
        functools.partial(_nm_mm_kernel, n_g=len(ws)),
        out_shape=jax.ShapeDtypeStruct((n_rows, n_cols), out_dtype),
        grid=(n_rows // tm, pl.cdiv(n_cols, tn)),
        in_specs=in_specs,
        out_specs=pl.BlockSpec((tm, tn), lambda m, n: (m, n)),
        scratch_shapes=[pltpu.VMEM((tm, d), BF16)],
        compiler_params=_cparams(("arbitrary", "arbitrary"), VMEM_LIMIT_BIG),
        name="normmod_matmul",
    )(h, g.reshape(1, d), mod, mod, *ws)


def _final_norm_kernel(x_ref, g_ref, o_ref):
    x = x_ref[...]
    ms = jnp.mean(x * x, axis=-1, keepdims=True)
    o_ref[...] = x * lax.rsqrt(ms + EPS) * g_ref[...]


def _mm_kernel(te_ref, nu_ref, *refs, n_x, n_g, epi):
    x_refs = refs[:n_x]
    w_refs = refs[n_x:n_x + n_x * n_g]
    pos = n_x + n_x * n_g
    if epi == "resgate":
        res_ref, gt_ref = refs[pos], refs[pos + 1]
        pos += 2
    o_ref = refs[pos]
    wc_refs = refs[pos + 1:]
    m = pl.program_id(1)
    panel_changed = (m == 0) | (te_ref[m] != te_ref[jnp.maximum(m - 1, 0)])

    @pl.when(panel_changed)
    def _():
        for w, wc in zip(w_refs, wc_refs):
            wc[...] = w[0].astype(BF16)

    @pl.when(m < nu_ref[0])
    def _():
        accs = []
        for g in range(n_g):
            acc = None
            for i in range(n_x):
                part = jnp.dot(x_refs[i][...].astype(BF16), wc_refs[g * n_x + i][...],
                               preferred_element_type=F32)
                acc = part if acc is None else acc + part
            accs.append(acc)
        if epi == "plain":
            out = accs[0]
        elif epi == "swiglu":
            a = accs[0]
            out = (a * jax.nn.sigmoid(a)) * accs[1]
        else:
            out = res_ref[...] + gt_ref[0] * accs[0]
        o_ref[...] = out.astype(o_ref.dtype)

    @pl.when(m >= nu_ref[0])
    def _():
        o_ref[...] = jnp.zeros_like(o_ref)


def _matmul(xs, ws, *, n_rows, out_dtype, epi="plain", tile_expert=None, n_used=None,
            res=None, mod=None, gt_blk=None, mod_row=None, tm=TM, tn=TN):
    n_x, n_g = len(xs), len(ws)
    n_cols = ws[0].shape[2]
    tn = _pick(n_cols, (tn, 512, 256, 128))
    tm = _pick(n_rows, (tm, 256, 128))
    mt, nt = n_rows // tm, n_cols // tn
    if tile_expert is None:
        tile_expert = jnp.zeros((mt,), jnp.int32)
        n_used = jnp.full((1,), mt, jnp.int32)
    ks = [x.shape[1] for x in xs]
    in_specs = [pl.BlockSpec((tm, k), lambda n, m, te, nu: (jnp.minimum(m, nu[0] - 1), 0)) for k in ks]
    args = list(xs)
    for g in range(n_g):
        for i, k in enumerate(ks):
            in_specs.append(pl.BlockSpec((1, k, tn), lambda n, m, te, nu, i=i: (te[m], i, n)))
            args.append(ws[g])
    if epi == "resgate":
        blk0 = gt_blk * (n_cols // tn)
        in_specs.append(pl.BlockSpec((tm, tn), lambda n, m, te, nu: (m, n)))
        in_specs.append(pl.BlockSpec((1, 1, tn), lambda n, m, te, nu: (mod_row(m * tm), 0, blk0 + n)))
        args += [res, mod]
    scratch = [pltpu.VMEM((k, tn), BF16) for _ in range(n_g) for k in ks]
    return pl.pallas_call(
        functools.partial(_mm_kernel, n_x=n_x, n_g=n_g, epi=epi),
        out_shape=jax.ShapeDtypeStruct((n_rows, n_cols), out_dtype),
        grid_spec=pltpu.PrefetchScalarGridSpec(
            num_scalar_prefetch=2,
            grid=(nt, mt),
            in_specs=in_specs,
            out_specs=pl.BlockSpec((tm, tn), lambda n, m, te, nu: (m, n)),
            scratch_shapes=scratch,
        ),
        compiler_params=_cparams(("arbitrary", "arbitrary")),
        name="matmul_" + epi,
    )(tile_expert, n_used, *args)


def _rope(x, cos, sin_signed, half):
    lane = lax.broadcasted_iota(jnp.int32, (x.shape[0], LANES), 1)
    first = (lane & (2 * half - 1)) < half
    blocks = []
    for j in range(x.shape[1] // LANES):
        xj = x[:, j * LANES:(j + 1) * LANES]
        blocks.append(jnp.where(first, pltpu.roll(xj, LANES - half, 1), pltpu.roll(xj, half, 1)))
    partner = blocks[0] if len(blocks) == 1 else jnp.concatenate(blocks, axis=-1)
    return x * cos + partner * sin_signed


def _group_mean_sq(x, g_ref):
    x2 = x * x
    hi = x2.astype(BF16)
    lo = (x2 - hi.astype(F32)).astype(BF16)
    g = g_ref[...]
    return (jnp.dot(hi, g, preferred_element_type=F32) + jnp.dot(lo, g, preferred_element_type=F32))


def _dup_halves(x):
    lane = lax.broadcasted_iota(jnp.int32, x.shape, 1)
    lo = lane < HEAD_DIM
    swapped = pltpu.roll(x, HEAD_DIM, 1)
    return jnp.concatenate([jnp.where(lo, x, swapped), jnp.where(lo, swapped, x)], axis=-1)


def _store_transposed(dst_ref, x):
    for j in range(x.shape[1] // LANES):
        dst_ref[0, j * LANES:(j + 1) * LANES, :] = x[:, j * LANES:(j + 1) * LANES].T.astype(BF16)


def _prep_kernel(p_ref, c64_ref, s64_ref, cq_ref, sq_ref, ck_ref, sk_ref,
                 bqn_ref, bkn_ref, cqn_ref, ckvn_ref, g512_ref, g128_ref,
                 wq_ref, wkk_ref, wkv_ref, place_ref,
                 qa_ref, qb_ref, qc_ref, qd_ref,
                 ka_ref, kb_ref, kc_ref, kd_ref, vd_ref, vta_ref, vtb_ref, vtc_ref):
    c64, s64 = c64_ref[...], s64_ref[...]
    c128, s128 = c64[:, :LANES], s64[:, :LANES]
    scale64 = HEAD_DIM ** -0.5
    scale_mla = (C_NOPE_DIM + C_ROPE_DIM) ** -0.5

    qa_ref[...] = (_rope(p_ref[:, OFF_QA:OFF_QA + 512], c64, s64, 16) * (scale64 * LOG2E)).astype(BF16)
    ka_ref[0] = _rope(p_ref[:, OFF_KA:OFF_KA + 512], c64, s64, 16).astype(BF16)
    _store_transposed(vta_ref, p_ref[:, OFF_VA:OFF_VA + 512])

    xb = p_ref[:, OFF_QB:OFF_QB + 512]
    yb = xb * lax.rsqrt(_group_mean_sq(xb, g512_ref) + EPS) * bqn_ref[...]
    qb_ref[...] = (_rope(yb, c64, s64, 16) * (scale64 * LOG2E)).astype(BF16)
    xk = p_ref[:, OFF_KB:OFF_KB + 128]
    yk = xk * lax.rsqrt(_group_mean_sq(xk, g128_ref) + EPS) * bkn_ref[...]
    kb_ref[0] = _dup_halves(_rope(yk, c128, s128, 16)).astype(BF16)
    _store_transposed(vtb_ref, p_ref[:, OFF_VB:OFF_VB + 128])

    xq = p_ref[:, OFF_QC:OFF_QC + C_Q_RANK]
    yq = xq * lax.rsqrt(jnp.mean(xq * xq, axis=-1, keepdims=True) + EPS) * cqn_ref[...]
    qf = jnp.dot(yq.astype(BF16), wq_ref[...], preferred_element_type=F32)
    qc_ref[...] = (_rope(qf, cq_ref[...], sq_ref[...], 8) * (scale_mla * LOG2E)).astype(BF16)
    xc = p_ref[:, OFF_CKV:OFF_CKV + C_KV_RANK]
    yc = (xc * lax.rsqrt(jnp.mean(xc * xc, axis=-1, keepdims=True) + EPS) * ckvn_ref[...]).astype(BF16)
    kr = _rope(p_ref[:, OFF_KR:OFF_KR + LANES], ck_ref[...], sk_ref[...], 8).astype(BF16)
    kc = (jnp.dot(yc, wkk_ref[...], preferred_element_type=F32)
          + jnp.dot(kr, place_ref[...], preferred_element_type=F32))
    kc_ref[0] = kc.astype(BF16)
    _store_transposed(vtc_ref, jnp.dot(yc, wkv_ref[...], preferred_element_type=F32))

    qd_ref[...] = (_rope(p_ref[:, OFF_QD:OFF_QD + 512], c64, s64, 16) * scale64).astype(BF16)
    kd_ref[0] = _dup_halves(_rope(p_ref[:, OFF_KD:OFF_KD + 128], c128, s128, 16)).astype(BF16)
    vd_ref[0] = _dup_halves(p_ref[:, OFF_VD:OFF_VD + 128]).astype(BF16)


def _prep(p, tables, consts, dims):
    bsz, s_len, n_ctx = dims
    n_rows = p.shape[0]
    sk = n_ctx + s_len
    tr = _pick(math.gcd(s_len, n_ctx), (TR, 128))
    n_lat_t, lat_pb, ctx_pb = bsz * s_len // tr, s_len // tr, n_ctx // tr

    def is_lat(t):
        return t < n_lat_t

    def tbl_idx(t):
        return jnp.where(is_lat(t), t % lat_pb, lat_pb)

    def kv_b(t):
        return jnp.where(is_lat(t), t // lat_pb, (t - n_lat_t) // ctx_pb)

    def kv_j(t):
        return jnp.where(is_lat(t), ctx_pb + t % lat_pb, (t - n_lat_t) % ctx_pb)

    def row_spec(w):
        return pl.BlockSpec((tr, w), lambda t: (t, 0))

    def tbl_spec(w):
        return pl.BlockSpec((tr, w), lambda t: (tbl_idx(t), 0))

    def const_spec(a):
        return pl.BlockSpec(a.shape, lambda t: (0,) * a.ndim)

    def kv_spec(w):
        return pl.BlockSpec((1, tr, w), lambda t: (kv_b(t), kv_j(t), 0))

    def vt_spec(w):
        return pl.BlockSpec((1, w, tr), lambda t: (kv_b(t), 0, kv_j(t)))

    q_widths = (512, 512, 1024, 512)
    kv_widths = (512, 256, 1024, 256, 256)
    vt_widths = (512, 128, 512)
    out_shape = ([jax.ShapeDtypeStruct((n_rows, w), BF16) for w in q_widths]
                 + [jax.ShapeDtypeStruct((bsz, sk, w), BF16) for w in kv_widths]
                 + [jax.ShapeDtypeStruct((bsz, w, sk), BF16) for w in vt_widths])
    out_specs = ([row_spec(w) for w in q_widths] + [kv_spec(w) for w in kv_widths]
                 + [vt_spec(w) for w in vt_widths])
    in_specs = ([row_spec(IN_COLS)] + [tbl_spec(t.shape[1]) for t in tables]
                + [const_spec(a) for a in consts])
    return pl.pallas_call(
        _prep_kernel,
        out_shape=out_shape,
        grid=(n_rows // tr,),
        in_specs=in_specs,
        out_specs=out_specs,
        compiler_params=_cparams(("arbitrary",)),
        name="attn_prep",
    )(p, *tables, *consts)


def _scores(a, b):
    return lax.dot_general(a, b, (((1,), (1,)), ((), ())), preferred_element_type=F32)


def _half_masks(shape):
    lane = lax.broadcasted_iota(jnp.int32, shape, 1)
    lo = lane < HEAD_DIM
    return lo, jnp.logical_not(lo)


def _softmax_pv_t(s, v_t):
    m = jnp.max(s, axis=0, keepdims=True)
    e = jnp.exp2(s - m).astype(BF16)
    dv = v_t.shape[0]
    v_ext = jnp.concatenate([v_t, jnp.ones((BF16_SUBLANES, v_t.shape[1]), BF16)], axis=0)
    oe = jnp.dot(v_ext, e, preferred_element_type=F32)
    return oe[:dv] / oe[dv:dv + 1]


def _run_units(score_fns, finish_fns):
    s = score_fns[0]()
    for i, finish in enumerate(finish_fns):
        s_next = score_fns[i + 1]() if i + 1 < len(score_fns) else None
        finish(s)
        s = s_next


def _gqa_t_kernel(q_ref, k_ref, vt_ref, o_ref):
    k, v_t = k_ref[0], vt_ref[0]
    masks = _half_masks((q_ref.shape[0], LANES))
    n_pairs = q_ref.shape[1] // LANES
    outs = {}

    def score(j, z):
        q = q_ref[:, j * LANES:(j + 1) * LANES]
        return lambda: _scores(k, jnp.where(masks[z], q, jnp.zeros_like(q)))

    def finish(j, z):
        def fin(s):
            outs[z] = _softmax_pv_t(s, v_t)
            if z == 1:
                pair = jnp.concatenate([outs[0], outs[1]], axis=0)
                o_ref[:, j * LANES:(j + 1) * LANES] = pair.T.astype(o_ref.dtype)
        return fin

    units = [(j, z) for j in range(n_pairs) for z in range(2)]
    _run_units([score(j, z) for j, z in units], [finish(j, z) for j, z in units])


def _mla_t_kernel(q_ref, k_ref, vt_ref, o_ref):
    n_heads = q_ref.shape[1] // LANES
    outs = {}

    def score(u):
        return lambda: _scores(k_ref[0, :, u * LANES:(u + 1) * LANES], q_ref[:, u * LANES:(u + 1) * LANES])

    def finish(u):
        def fin(s):
            outs[u % 2] = _softmax_pv_t(s, vt_ref[0, u * C_V_DIM:(u + 1) * C_V_DIM, :])
            if u % 2 == 1:
                pair = jnp.concatenate([outs[0], outs[1]], axis=0)
                j = u // 2
                o_ref[:, j * LANES:(j + 1) * LANES] = pair.T.astype(o_ref.dtype)
        return fin

    _run_units([score(u) for u in range(n_heads)], [finish(u) for u in range(n_heads)])


def _diff_t_kernel(lam_ref, subln_ref, q0_ref, q1_ref, k0_ref, k1_ref, vt_ref, o_ref, *, lam_init):
    t = lam_ref[...]
    lam = (jnp.exp(jnp.sum(t[0:1] * t[1:2], axis=-1, keepdims=True))
           - jnp.exp(jnp.sum(t[2:3] * t[3:4], axis=-1, keepdims=True)) + lam_init)
    q_refs, k_refs = (q0_ref, q1_ref), (k0_ref, k1_ref)
    masks = _half_masks(q0_ref.shape)
    first_map = {}

    def score(z, mp):
        q = q_refs[mp][...]
        return lambda: _scores(k_refs[mp][0], jnp.where(masks[z], q, jnp.zeros_like(q)))

    def finish(z, mp):
        def fin(s):
            a = _softmax_pv_t(s, vt_ref[0, z * A_V_DIM:(z + 1) * A_V_DIM, :])
            if mp == 0:
                first_map[z] = a
                return
            d = (first_map[z] - lam * a).T
            y = d * lax.rsqrt(jnp.mean(d * d, axis=-1, keepdims=True) + EPS) * subln_ref[...]
            o_ref[:, z * A_V_DIM:(z + 1) * A_V_DIM] = (y * (1.0 - lam_init)).astype(o_ref.dtype)
        return fin

    units = [(z, mp) for z in range(2) for mp in range(2)]
    _run_units([score(z, mp) for z, mp in units], [finish(z, mp) for z, mp in units])


def _attend_sink(q, k, v, sink):
    s = _scores(q, k)
    m = jnp.maximum(jnp.max(s, axis=-1, keepdims=True), sink)
    e = jnp.exp(s - m)
    l = jnp.sum(e, axis=-1, keepdims=True) + jnp.exp(sink - m)
    return jnp.dot(e.astype(BF16), v, preferred_element_type=F32) / l


def _window_kernel(sink_ref, q_ref, k_ref, v_ref, o_ref, *, s_len, n_ctx, n_q_lat, with_ctx):
    tq = q_ref.shape[0]
    band = tq + 2 * WINDOW
    pair = pl.program_id(1)
    i = pl.program_id(2)

    def body(is_lat):
        q = q_ref[...]
        k_c, v_c = k_ref[0, :n_ctx, :], v_ref[0, :n_ctx, :]
        masks = _half_masks(q.shape)
        if is_lat:
            start = pl.multiple_of(jnp.clip(i * tq - WINDOW, 0, s_len - band), LANES)
            row0 = pl.multiple_of(n_ctx + start, LANES)
            k_b = k_ref[0, pl.ds(row0, band), :]
            v_b = v_ref[0, pl.ds(row0, band), :]
            qpos = i * tq + lax.broadcasted_iota(jnp.int32, (tq, band), 0)
            kpos = start + lax.broadcasted_iota(jnp.int32, (tq, band), 1)
            in_band = jnp.abs(qpos - kpos) <= WINDOW
        outs = []
        for z in range(2):
            qz = jnp.where(masks[z], q, jnp.zeros_like(q))
            sink = sink_ref[2 * pair + z]
            if not is_lat:
                outs.append(_attend_sink(qz, k_c, v_c, sink))
                continue
            s_c = _scores(qz, k_c)
            s_b = jnp.where(in_band, _scores(qz, k_b), -jnp.inf)
            m = jnp.maximum(jnp.maximum(jnp.max(s_c, axis=-1, keepdims=True),
                                        jnp.max(s_b, axis=-1, keepdims=True)), sink)
            e_c, e_b = jnp.exp(s_c - m), jnp.exp(s_b - m)
            l = (jnp.sum(e_c, axis=-1, keepdims=True) + jnp.sum(e_b, axis=-1, keepdims=True)
                 + jnp.exp(sink - m))
            o = (jnp.dot(e_c.astype(BF16), v_c, preferred_element_type=F32)
                 + jnp.dot(e_b.astype(BF16), v_b, preferred_element_type=F32))
            outs.append(o / l)
        lo, _ = _half_masks(outs[0].shape)
        o_ref[...] = jnp.where(lo, outs[0], outs[1]).astype(o_ref.dtype)

    if not with_ctx:
        body(True)
        return

    @pl.when(i < n_q_lat)
    def _():
        body(True)

    @pl.when(i >= n_q_lat)
    def _():
        body(False)


def _mixers(ops, dims, params, lam_init, with_ctx):
    bsz, s_len, n_ctx = dims
    qa, qb, qc, qd, ka, kb, kc, kd, vd, vta, vtb, vtc = ops
    lam_vecs, subln, sink = params
    n_lat = bsz * s_len
    sk_all = n_ctx + s_len

    def const_spec(a):
        return pl.BlockSpec(a.shape, lambda b, p, i: (0,) * a.ndim)

    def dense_set(tq, nk, row0_tiles, n_q, out_rows):
        def q_row(b, i):
            return row0_tiles + b * n_q + i

        def qspec(w, col):
            return pl.BlockSpec((tq, w), lambda b, p, i: (q_row(b, i), col(p)))

        def kspec(w, col):
            return pl.BlockSpec((1, nk, w), lambda b, p, i: (b, 0, col(p)))

        def vtspec(w, row):
            return pl.BlockSpec((1, w, nk), lambda b, p, i: (b, row(p), 0))

        def call(kernel, in_specs, args, name):
            return pl.pallas_call(
                kernel,
                out_shape=jax.ShapeDtypeStruct((out_rows, 512), BF16),
                grid=(bsz, 2, n_q),
                in_specs=in_specs,
                out_specs=pl.BlockSpec((tq, 256), lambda b, p, i: (b * n_q + i, p)),
                compiler_params=_cparams(("arbitrary", "arbitrary", "arbitrary")),
                name=name,
            )(*args)

        oa = call(functools.partial(_diff_t_kernel, lam_init=lam_init),
                  [const_spec(lam_vecs), const_spec(subln),
                   qspec(128, lambda p: p), qspec(128, lambda p: 2 + p),
                   kspec(128, lambda p: p), kspec(128, lambda p: 2 + p), vtspec(256, lambda p: p)],
                  [lam_vecs, subln, qa, qa, ka, ka, vta], "attn_diff")
        ob = call(_gqa_t_kernel,
                  [qspec(256, lambda p: p), kspec(128, lambda p: p), vtspec(64, lambda p: p)],
                  [qb, kb, vtb], "attn_qknorm")
        oc = call(_mla_t_kernel,
                  [qspec(512, lambda p: p), kspec(512, lambda p: p), vtspec(256, lambda p: p)],
                  [qc, kc, vtc], "attn_mla")
        return [oa, ob, oc]

    tq = _pick(s_len, (TQ, 256, 128))
    dense = dense_set(tq, sk_all, 0, s_len // tq, n_lat)
    if with_ctx:
        tq_c = _pick(n_ctx, (TQ, 256, 128))
        ctx = dense_set(tq_c, n_ctx, n_lat // tq_c, n_ctx // tq_c, bsz * n_ctx)
        dense = [jnp.concatenate([a, b], axis=0) for a, b in zip(dense, ctx)]

    tw = _pick(math.gcd(s_len, n_ctx), (TQ_WIN, 128))
    n_q_lat, n_q_ctx = s_len // tw, n_ctx // tw
    n_q = n_q_lat + n_q_ctx if with_ctx else n_q_lat
    out_rows = n_lat + bsz * n_ctx if with_ctx else n_lat

    def w_row(b, i):
        return jnp.where(i < n_q_lat, b * n_q_lat + i, n_lat // tw + b * n_q_ctx + (i - n_q_lat))

    od = pl.pallas_call(
        functools.partial(_window_kernel, s_len=s_len, n_ctx=n_ctx, n_q_lat=n_q_lat, with_ctx=with_ctx),
        out_shape=jax.ShapeDtypeStruct((out_rows, 512), BF16),
        grid=(bsz, 4, n_q),
        in_specs=[pl.BlockSpec(memory_space=pltpu.SMEM),
                  pl.BlockSpec((tw, 128), lambda b, p, i: (w_row(b, i), p)),
                  pl.BlockSpec((1, sk_all, 128), lambda b, p, i: (b, 0, p // 2)),
                  pl.BlockSpec((1, sk_all, 128), lambda b, p, i: (b, 0, p // 2))],
        out_specs=pl.BlockSpec((tw, 128), lambda b, p, i: (w_row(b, i), p)),
        compiler_params=_cparams(("arbitrary", "arbitrary", "arbitrary")),
        name="attn_window",
    )(sink, qd, kd, vd)
    return dense + [od]


def _route_kernel(h_ref, g_ref, sh_ref, sc_ref, w_ref, b_ref, u_ref, sel_ref, idx_ref, gw_ref):
    u = _rms_modulate(h_ref[...], g_ref[...], sc_ref[0], sh_ref[0])
    u_ref[...] = u
    w = w_ref[...]
    u_hi = u.astype(BF16)
    u_lo = (u - u_hi.astype(F32)).astype(BF16)
    w_hi = w.astype(BF16)
    w_lo = (w - w_hi.astype(F32)).astype(BF16)
    logits = (_scores(w_hi, u_hi) + _scores(w_hi, u_lo) + _scores(w_lo, u_hi) + _scores(w_lo, u_lo)
              + b_ref[...])
    ids = lax.broadcasted_iota(jnp.int32, logits.shape, 0).astype(F32)
    m1 = jnp.max(logits, axis=0, keepdims=True)
    i1 = jnp.min(jnp.where(logits == m1, ids, float(N_EXPERTS)), axis=0, keepdims=True)
    first = ids == i1
    rest = jnp.where(first, -jnp.inf, logits)
    m2 = jnp.max(rest, axis=0, keepdims=True)
    i2 = jnp.min(jnp.where(rest == m2, ids, float(N_EXPERTS)), axis=0, keepdims=True)
    second = ids == i2
    e = jnp.exp(m2 - m1)
    w1 = 1.0 / (1.0 + e)
    w2 = e / (1.0 + e)
    sel_ref[...] = jnp.where(first | second, 1, 0).astype(jnp.int32)
    idx_ref[...] = jnp.where(ids == 0.0, i1, jnp.where(ids == 1.0, i2, 0.0)).astype(jnp.int32)
    gw_ref[...] = jnp.where(ids == 0.0, w1, jnp.where(ids == 1.0, w2, 0.0))


def _route(h, g, mod, sh_blk, sc_blk, w_router_t, b_router, n_tok, s_len, mod_row):
    d = h.shape[1]
    tm = _row_tile(n_tok, s_len, (TM, 256, 128))
    outs = ([jax.ShapeDtypeStruct((n_tok, d), F32)]
            + [jax.ShapeDtypeStruct((N_EXPERTS, n_tok), dt) for dt in (jnp.int32, jnp.int32, F32)])
    return pl.pallas_call(
        _route_kernel,
        out_shape=outs,
        grid=(n_tok // tm,),
        in_specs=[pl.BlockSpec((tm, d), lambda m: (m, 0)),
                  pl.BlockSpec((1, d), lambda m: (0, 0)),
                  pl.BlockSpec((1, 1, d), lambda m: (mod_row(m * tm), 0, sh_blk)),
                  pl.BlockSpec((1, 1, d), lambda m: (mod_row(m * tm), 0, sc_blk)),
                  pl.BlockSpec((N_EXPERTS, d), lambda m: (0, 0)),
                  pl.BlockSpec((N_EXPERTS, 1), lambda m: (0, 0))],
        out_specs=[pl.BlockSpec((tm, d), lambda m: (m, 0))]
        + [pl.BlockSpec((N_EXPERTS, tm), lambda m: (0, m))] * 3,
        compiler_params=_cparams(("arbitrary",)),
        name="moe_route",
    )(h, g.reshape(1, d), mod, mod, w_router_t, b_router.reshape(N_EXPERTS, 1))


def _row_copy(src, dst, sem, src_row, dst_row):
    return pltpu.make_async_copy(src.at[pl.ds(src_row, 1)], dst.at[pl.ds(dst_row, 1)], sem)


def _gather_kernel(tok_ref, nxt_ref, src_ref, o_ref, buf, sem):
    i = pl.program_id(0)
    n = pl.num_programs(0)
    tg = buf.shape[1]
    slot = i % 2

    def issue(t_ref, s):
        def body(r, carry):
            _row_copy(src_ref, buf.at[s], sem.at[s], t_ref[0, 0, r], r).start()
            return carry
        lax.fori_loop(0, tg, body, 0, unroll=DMA_UNROLL)

    @pl.when(i == 0)
    def _():
        issue(tok_ref, 0)

    @pl.when(i + 1 < n)
    def _():
        issue(nxt_ref, 1 - slot)

    def wait(r, carry):
        _row_copy(src_ref, buf.at[slot], sem.at[slot], 0, r).wait()
        return carry

    lax.fori_loop(0, tg, wait, 0, unroll=DMA_UNROLL)
    o_ref[...] = buf[slot].astype(o_ref.dtype)


def _gather_rows(src, tok, tg):
    n_rows = tok.shape[0]
    d = src.shape[1]
    n_t = n_rows // tg
    tok = tok.reshape(n_t, 1, tg)
    return pl.pallas_call(
        _gather_kernel,
        out_shape=jax.ShapeDtypeStruct((n_rows, d), BF16),
        grid=(n_t,),
        in_specs=[pl.BlockSpec((1, 1, tg), lambda i: (i, 0, 0), memory_space=pltpu.SMEM),
                  pl.BlockSpec((1, 1, tg), lambda i: (jnp.minimum(i + 1, n_t - 1), 0, 0),
                               memory_space=pltpu.SMEM),
                  pl.BlockSpec(memory_space=pl.ANY)],
        out_specs=pl.BlockSpec((tg, d), lambda i: (i, 0)),
        scratch_shapes=[pltpu.VMEM((2, tg, d), F32), pltpu.SemaphoreType.DMA((2,))],
        compiler_params=_cparams(("arbitrary",)),
        name="moe_gather",
    )(tok, tok, src)


def _combine_kernel(pos_ref, nxt_ref, y_ref, h_ref, gw_ref, gt_ref, gf_ref, o_ref, buf, sem):
    i = pl.program_id(0)
    n = pl.num_programs(0)
    tc = buf.shape[2]
    slot = i % 2

    def issue(p_ref, s):
        def body(r, carry):
            _row_copy(y_ref, buf.at[s, 0], sem.at[s], p_ref[0, 0, r], r).start()
            _row_copy(y_ref, buf.at[s, 1], sem.at[s], p_ref[0, 1, r], r).start()
            return carry
        lax.fori_loop(0, tc, body, 0, unroll=DMA_UNROLL)

    @pl.when(i == 0)
    def _():
        issue(pos_ref, 0)

    @pl.when(i + 1 < n)
    def _():
        issue(nxt_ref, 1 - slot)

    def wait(r, carry):
        _row_copy(y_ref, buf.at[slot, 0], sem.at[slot], 0, r).wait()
        _row_copy(y_ref, buf.at[slot, 1], sem.at[slot], 0, r).wait()
        return carry

    lax.fori_loop(0, tc, wait, 0, unroll=DMA_UNROLL)
    gw = gw_ref[...]
    moe = gw[:, 0:1] * buf[slot, 0] + gw[:, 1:2] * buf[slot, 1]
    x = h_ref[...] + gt_ref[0] * moe
    ms = jnp.mean(x * x, axis=-1, keepdims=True)
    o_ref[...] = x * lax.rsqrt(ms + EPS) * gf_ref[...]


def _combine(y, h, pos, gw_t, mod, gt_blk, mod_row, g_final, n_tok):
    d = h.shape[1]
    n_t, _, tc = pos.shape
    return pl.pallas_call(
        _combine_kernel,
        out_shape=jax.ShapeDtypeStruct((n_tok, d), F32),
        grid=(n_t,),
        in_specs=[pl.BlockSpec((1, 2, tc), lambda i: (i, 0, 0), memory_space=pltpu.SMEM),
                  pl.BlockSpec((1, 2, tc), lambda i: (jnp.minimum(i + 1, n_t - 1), 0, 0),
                               memory_space=pltpu.SMEM),
                  pl.BlockSpec(memory_space=pl.ANY),
                  pl.BlockSpec((tc, d), lambda i: (i, 0)),
                  pl.BlockSpec((tc, N_EXPERTS), lambda i: (i, 0)),
                  pl.BlockSpec((1, 1, d), lambda i: (mod_row(i * tc), 0, gt_blk)),
                  pl.BlockSpec((1, d), lambda i: (0, 0))],
        out_specs=pl.BlockSpec((tc, d), lambda i: (i, 0)),
        scratch_shapes=[pltpu.VMEM((2, 2, tc, d), F32), pltpu.SemaphoreType.DMA((2,))],
        compiler_params=_cparams(("arbitrary",)),
        name="moe_combine",
    )(pos, pos, y, h, gw_t, mod, g_final.reshape(1, d))


def _dispatch_plan(sel, idx, tm, n_slots, tc):
    n_tok = sel.shape[1]
    n_tiles = n_slots // tm
    counts = jnp.sum(sel, axis=1)
    padded = ((counts + tm - 1) // tm) * tm
    ends = jnp.cumsum(padded)
    offs = ends - padded
    pos = offs[:, None] + jnp.cumsum(sel, axis=1) - sel
    pos0 = jnp.take_along_axis(pos, idx[0:1], axis=0)[0]
    pos1 = jnp.take_along_axis(pos, idx[1:2], axis=0)[0]
    pos_tiles = jnp.stack([pos0.reshape(n_tok // tc, tc), pos1.reshape(n_tok // tc, tc)], axis=1)
    tile_start = jnp.arange(n_tiles, dtype=jnp.int32) * tm
    tile_expert = jnp.minimum(jnp.sum(ends[None, :] <= tile_start[:, None], axis=1), N_EXPERTS - 1)
    n_used = ends[-1] // tm
    tok = jnp.arange(n_tok, dtype=jnp.int32)
    tok_of_row = jnp.zeros((n_slots,), jnp.int32).at[jnp.concatenate([pos0, pos1])].set(
        jnp.concatenate([tok, tok]), unique_indices=True)
    return (pos_tiles.astype(jnp.int32), tok_of_row, tile_expert.astype(jnp.int32),
            n_used.reshape(1).astype(jnp.int32))


def _rope_tables(s_len, pad_rows):
    t = jnp.arange(s_len)
    rows, cols = (t // GRID_W).astype(F32), (t % GRID_W).astype(F32)

    def axis_tables(rot_dim):
        axis_dim = rot_dim // 2
        inv = ROPE_THETA ** (-jnp.arange(0, axis_dim, 2, dtype=F32) / axis_dim)
        ar, ac = rows[:, None] * inv[None, :], cols[:, None] * inv[None, :]
        cos = jnp.concatenate([jnp.cos(ar), jnp.cos(ar), jnp.cos(ac), jnp.cos(ac)], axis=1)
        sin = jnp.concatenate([-jnp.sin(ar), jnp.sin(ar), -jnp.sin(ac), jnp.sin(ac)], axis=1)
        return cos, sin

    def with_identity(cos, sin):
        w = cos.shape[1]
        return (jnp.concatenate([cos, jnp.ones((pad_rows, w), F32)], axis=0),
                jnp.concatenate([sin, jnp.zeros((pad_rows, w), F32)], axis=0))

    c64, s64 = axis_tables(HEAD_DIM)
    c64, s64 = jnp.tile(c64, (1, 8)), jnp.tile(s64, (1, 8))
    c32, s32 = axis_tables(C_ROPE_DIM)
    ones, zeros = jnp.ones((s_len, 1), F32), jnp.zeros((s_len, 1), F32)
    cq = jnp.tile(jnp.concatenate([jnp.tile(ones, (1, 64)), c32, jnp.tile(ones, (1, 32))], axis=1), (1, 8))
    sq = jnp.tile(jnp.concatenate([jnp.tile(zeros, (1, 64)), s32, jnp.tile(zeros, (1, 32))], axis=1), (1, 8))
    ck = jnp.concatenate([c32, jnp.tile(ones, (1, 96))], axis=1)
    sk = jnp.concatenate([s32, jnp.tile(zeros, (1, 96))], axis=1)
    out = []
    for c, s in ((c64, s64), (cq, sq), (ck, sk)):
        out += list(with_identity(c, s))
    return out


def _mla_weights(w_q_up, w_kv_up):
    qd = C_NOPE_DIM + C_ROPE_DIM
    wq = jnp.pad(w_q_up.reshape(C_Q_RANK, C_HEADS, qd), ((0, 0), (0, 0), (0, LANES - qd)))
    wkv = w_kv_up.reshape(C_KV_RANK, C_HEADS, C_NOPE_DIM + C_V_DIM)
    wkk = jnp.pad(wkv[:, :, :C_NOPE_DIM], ((0, 0), (0, 0), (0, LANES - C_NOPE_DIM)))
    wkv_v = wkv[:, :, C_NOPE_DIM:]
    return (wq.reshape(C_Q_RANK, C_HEADS * LANES).astype(BF16),
            wkk.reshape(C_KV_RANK, C_HEADS * LANES).astype(BF16),
            wkv_v.reshape(C_KV_RANK, C_HEADS * C_V_DIM).astype(BF16))


def _static_mats():
    g = (np.arange(512)[:, None] // HEAD_DIM == np.arange(512)[None, :] // HEAD_DIM) / HEAD_DIM
    place = np.zeros((LANES, C_HEADS * LANES), np.float32)
    for h in range(C_HEADS):
        place[np.arange(C_ROPE_DIM), h * LANES + C_NOPE_DIM + np.arange(C_ROPE_DIM)] = 1.0
    return (jnp.asarray(g, BF16), jnp.asarray(g[:128, :128], BF16), jnp.asarray(place, BF16))


def kernel(x, c, ctx, c_ctx, w_mod, b_mod, g_mix, g_ffn, g_final, w_in, w_out, a_lam_q1, a_lam_k1, a_lam_q2, a_lam_k2, a_subln, b_q_norm, b_k_norm, c_q_norm, c_kv_norm, c_w_q_up, c_w_kv_up, d_sink, ffn_w_gate, ffn_w_up, ffn_w_down, moe_w_router, moe_b_router, moe_w_gate, moe_w_up, moe_w_down):
    bsz, s_len, d = x.shape
    n_ctx = ctx.shape[1]
    depth = w_mod.shape[0]
    n_lat = bsz * s_len
    n_all = n_lat + bsz * n_ctx
    dims = (bsz, s_len, n_ctx)

    def mod_row(row0):
        return jnp.where(row0 < n_lat, row0 // s_len, bsz)

    mod_rows = -(-(bsz + 1) // 8) * 8
    cc = jnp.zeros((mod_rows, d), F32).at[:bsz].set(c).at[bsz].set(c_ctx)
    mods = _modulation(cc, w_mod, b_mod)

    tr = _pick(math.gcd(s_len, n_ctx), (TR, 128))
    tables = _rope_tables(s_len, tr)
    g512, g128, place = _static_mats()

    h = jnp.concatenate([x.reshape(n_lat, d), ctx.reshape(bsz * n_ctx, d)], axis=0)
    for l in range(depth):
        last = l == depth - 1
        lam_init = 0.8 - 0.6 * math.exp(-0.3 * l)
        n_rows = n_lat if last else n_all
        mod = mods[l].reshape(mod_rows, 1, 6 * d)
        nm = dict(s_len=s_len, mod_row=mod_row)

        p = _nm_matmul(h, g_mix[l], mod, 0, 1, [w_in[l]], n_rows=n_all, out_dtype=F32, **nm)
        wq, wkk, wkv_v = _mla_weights(c_w_q_up[l], c_w_kv_up[l])
        consts = [jnp.tile(b_q_norm[l], 8)[None], jnp.tile(b_k_norm[l], 2)[None],
                  c_q_norm[l][None], c_kv_norm[l][None], g512, g128, wq, wkk, wkv_v, place]
        ops = _prep(p, tables, consts, dims)
        lam_vecs = jnp.stack([a_lam_q1[l], a_lam_k1[l], a_lam_q2[l], a_lam_k2[l]])
        mix = _mixers(ops, dims, (lam_vecs, a_subln[l][None], d_sink[l]), lam_init, not last)
        h = _matmul(mix, [w_out[l][None]], n_rows=n_rows, out_dtype=F32, epi="resgate",
                    res=h, mod=mod, gt_blk=2, mod_row=mod_row, tn=1024)

        i = l // 2
        if l % 2 == 0:
            mid = _nm_matmul(h, g_ffn[l], mod, 3, 4, [ffn_w_gate[i], ffn_w_up[i]], n_rows=n_rows,
                             out_dtype=BF16, **nm)
            h = _matmul([mid], [ffn_w_down[i][None]], n_rows=n_rows, out_dtype=F32, epi="resgate",
                        res=h, mod=mod, gt_blk=5, mod_row=mod_row)
        else:
            if not last:
                raise NotImplementedError("expert layers are only supported as the last layer")
            u2, sel, idx, gw = _route(h, g_ffn[l], mod, 3, 4, moe_w_router[i].T, moe_b_router[i],
                                      n_rows, s_len, mod_row)
            tm = _pick(n_rows, (TM, 256, 128))
            tc = _pick(s_len, (TC, 128))
            n_slots = 2 * n_rows + N_EXPERTS * tm
            pos, tok_of_row, tile_expert, n_used = _dispatch_plan(sel, idx, tm, n_slots, tc)
            xs = _gather_rows(u2, tok_of_row, tm)
            mid = _matmul([xs], [moe_w_gate[i], moe_w_up[i]], n_rows=n_slots, out_dtype=BF16,
                          epi="swiglu", tile_expert=tile_expert, n_used=n_used)
            y = _matmul([mid], [moe_w_down[i]], n_rows=n_slots, out_dtype=F32,
                        tile_expert=tile_expert, n_used=n_used)
            out = _combine(y, h, pos, gw.T, mod, 5, mod_row, g_final, n_rows)
            return out.reshape(bsz, s_len, d)

    tm = _pick(n_lat, (TM, 256, 128))
    out = pl.pallas_call(
        _final_norm_kernel,
        out_shape=jax.ShapeDtypeStruct((n_lat, d), F32),
        grid=(n_lat // tm,),
        in_specs=[pl.BlockSpec((tm, d), lambda m: (m, 0)), pl.BlockSpec((1, d), lambda m: (0, 0))],
        out_specs=pl.BlockSpec((tm, d), lambda m: (m, 0)),
        compiler_params=_cparams(("arbitrary",)),
        name="final_norm",
    )(h, g_final.reshape(1, d))
    return out.reshape(bsz, s_len, d)
```

```python
import functools
import math

import numpy as np
import jax
import jax.numpy as jnp
from jax import lax
from jax.experimental import pallas as pl
from jax.experimental.pallas import tpu as pltpu

F32 = jnp.float32
BF16 = jnp.bfloat16

GRID_W = 64
HEAD_DIM = 64
ROPE_THETA = 10000.0
EPS = 1e-6
A_HEADS, A_QK_DIM, A_V_DIM = 4, 64, 128
B_HEADS, B_KV_HEADS = 8, 2
C_HEADS, C_Q_RANK, C_KV_RANK, C_NOPE_DIM, C_ROPE_DIM, C_V_DIM = 8, 768, 256, 64, 32, 64
D_HEADS, D_KV_HEADS = 8, 2
WINDOW = 128
N_EXPERTS = 8
LOG2E = 1.4426950408889634

LANES = 128
BF16_SUBLANES = 16
VMEM_LIMIT = 48 * 2**20
VMEM_LIMIT_BIG = 56 * 2**20

TM = 512
TM_X = 1024
TN = 512
TR = 256
TQ = 512
TQ_WIN = 256
TC = 256
DMA_UNROLL = 8

OFF_QA, OFF_QB, OFF_QC, OFF_QD = 0, 512, 1024, 1792
OFF_KA, OFF_VA, OFF_KB, OFF_VB = 2304, 2816, 3328, 3456
OFF_CKV, OFF_KR, OFF_KD, OFF_VD = 3584, 3840, 3872, 4000
IN_COLS = 4128


def _cparams(sem, vmem=VMEM_LIMIT):
    return pltpu.CompilerParams(dimension_semantics=sem, vmem_limit_bytes=vmem)


def _pick(n, prefs):
    for p in prefs:
        if n % p == 0:
            return p
    return n


def _row_tile(n_rows, s_len, prefs):
    for p in prefs:
        if n_rows % p == 0 and s_len % p == 0:
            return p
    raise ValueError("no row tile fits")


def _rms_modulate(x, g, sc, sh):
    ms = jnp.mean(x * x, axis=-1, keepdims=True)
    return (x * lax.rsqrt(ms + EPS) * g) * (1.0 + sc) + sh


def _mod_kernel(c_ref, w_ref, b_ref, o_ref):
    a = c_ref[...]
    a = (a * jax.nn.sigmoid(a)).astype(BF16)
    o_ref[0] = jnp.dot(a, w_ref[0].astype(BF16), preferred_element_type=F32) + b_ref[0]


def _modulation(cc, w_mod, b_mod):
    depth, d, d6 = w_mod.shape
    rows = cc.shape[0]
    tn = _pick(d6, (1024, 512, 256, 128))
    return pl.pallas_call(
        _mod_kernel,
        out_shape=jax.ShapeDtypeStruct((depth, rows, d6), F32),
        grid=(depth, d6 // tn),
        in_specs=[
            pl.BlockSpec((rows, d), lambda l, n: (0, 0)),
            pl.BlockSpec((1, d, tn), lambda l, n: (l, 0, n)),
            pl.BlockSpec((1, 1, tn), lambda l, n: (l, 0, n)),
        ],
        out_specs=pl.BlockSpec((1, rows, tn), lambda l, n: (l, 0, n)),
        compiler_params=_cparams(("arbitrary", "arbitrary")),
        name="modulation",
    )(cc, w_mod, b_mod.reshape(depth, 1, d6))


def _stacked_specs(parts, block, tile_rows, col_of, n_lead):
    def grid_m(idx):
        return idx[n_lead]

    if len(parts) == 1:
        return [pl.BlockSpec(block, lambda *idx: (grid_m(idx), col_of(idx)))], None
    mt_a = parts[0].shape[0] // tile_rows
    return ([pl.BlockSpec(block, lambda *idx: (jnp.minimum(grid_m(idx), mt_a - 1), col_of(idx))),
             pl.BlockSpec(block, lambda *idx: (jnp.maximum(grid_m(idx) - mt_a, 0), col_of(idx)))], mt_a)


def _nm_mm_kernel(*refs, n_h, mt_a, n_g):
    h_refs = refs[:n_h]
    g_ref, sh_ref, sc_ref = refs[n_h:n_h + 3]
    w_refs, o_ref, u_ref = refs[n_h + 3:n_h + 3 + n_g], refs[n_h + 3 + n_g], refs[n_h + 4 + n_g]
    m = pl.program_id(0)

    def fill(h_ref):
        u_ref[...] = _rms_modulate(h_ref[...], g_ref[...], sc_ref[0], sh_ref[0]).astype(BF16)

    first = pl.program_id(1) == 0
    if n_h == 1:
        pl.when(first)(lambda: fill(h_refs[0]))
    else:
        pl.when(first & (m < mt_a))(lambda: fill(h_refs[0]))
        pl.when(first & (m >= mt_a))(lambda: fill(h_refs[1]))

    u = u_ref[...]
    accs = [jnp.dot(u, w[...].astype(BF16), preferred_element_type=F32) for w in w_refs]
    out = accs[0] if n_g == 1 else (accs[0] * jax.nn.sigmoid(accs[0])) * accs[1]
    o_ref[...] = out.astype(o_ref.dtype)


def _nm_matmul(hs, g, mod, sh_blk, sc_blk, ws, *, n_rows, s_len, mod_row, out_dtype):
    d = hs[0].shape[1]
    n_cols = ws[0].shape[1]
    tm = _row_tile(n_rows, s_len, (TM_X, 512, 256, 128))
    tn = TN
    h_specs, mt_a = _stacked_specs(hs, (tm, d), tm, lambda idx: 0, 0)
    in_specs = h_specs + [
        pl.BlockSpec((1, d), lambda m, n: (0, 0)),
        pl.BlockSpec((1, 1, d), lambda m, n: (mod_row(m * tm), 0, sh_blk)),
        pl.BlockSpec((1, 1, d), lambda m, n: (mod_row(m * tm), 0, sc_blk)),
    ] + [pl.BlockSpec((d, tn), lambda m, n: (0, n)) for _ in ws]
    return pl.pallas_call(
        functools.partial(_nm_mm_kernel, n_h=len(hs), mt_a=mt_a, n_g=len(ws)),
        out_shape=jax.ShapeDtypeStruct((n_rows, n_cols), out_dtype),
        grid=(n_rows // tm, pl.cdiv(n_cols, tn)),
        in_specs=in_specs,
        out_specs=pl.BlockSpec((tm, tn), lambda m, n: (m, n)),
        scratch_shapes=[pltpu.VMEM((tm, d), BF16)],
        compiler_params=_cparams(("arbitrary", "arbitrary"), VMEM_LIMIT_BIG),
        name="normmod_matmul",
    )(*hs, g.reshape(1, d), mod, mod, *ws)


def _final_norm_kernel(x_ref, g_ref, o_ref):
    x = x_ref[...]
    ms = jnp.mean(x * x, axis=-1, keepdims=True)
    o_ref[...] = x * lax.rsqrt(ms + EPS) * g_ref[...]


def _mm_kernel(te_ref, nu_ref, *refs, x_parts, res_parts, mt_a, n_g, epi):
    n_x = len(x_parts)
    refs = list(refs)

    def take(n):
        out = refs[:n]
        del refs[:n]
        return out

    x_refs = [take(p) for p in x_parts]
    w_refs = take(n_x * n_g)
    if epi == "resgate":
        res_refs = take(res_parts)
        gt_ref, = take(1)
    o_ref, = take(1)
    wc_refs = refs
    m = pl.program_id(1)

    def stacked(parts):
        if len(parts) == 1:
            return parts[0][...]
        return jnp.where(m < mt_a, parts[0][...], parts[1][...])

    panel_changed = (m == 0) | (te_ref[m] != te_ref[jnp.maximum(m - 1, 0)])

    @pl.when(panel_changed)
    def _():
        for w, wc in zip(w_refs, wc_refs):
            wc[...] = w[0].astype(BF16)

    @pl.when(m < nu_ref[0])
    def _():
        accs = []
        for g in range(n_g):
            acc = None
            for i in range(n_x):
                part = jnp.dot(stacked(x_refs[i]).astype(BF16), wc_refs[g * n_x + i][...],
                               preferred_element_type=F32)
                acc = part if acc is None else acc + part
            accs.append(acc)
        if epi == "plain":
            out = accs[0]
        elif epi == "swiglu":
            a = accs[0]
            out = (a * jax.nn.sigmoid(a)) * accs[1]
        else:
            out = stacked(res_refs) + gt_ref[0] * accs[0]
        o_ref[...] = out.astype(o_ref.dtype)

    @pl.when(m >= nu_ref[0])
    def _():
        o_ref[...] = jnp.zeros_like(o_ref)


def _matmul(xs, ws, *, n_rows, out_dtype, epi="plain", tile_expert=None, n_used=None,
            res=None, mod=None, gt_blk=None, mod_row=None, tm=TM, tn=TN):
    n_x, n_g = len(xs), len(ws)
    n_cols = ws[0].shape[2]
    tn = _pick(n_cols, (tn, 512, 256, 128))
    tm = _pick(n_rows, (tm, 256, 128))
    mt, nt = n_rows // tm, n_cols // tn
    if tile_expert is None:
        tile_expert = jnp.zeros((mt,), jnp.int32)
        n_used = jnp.full((1,), mt, jnp.int32)
    ks = [x[0].shape[1] for x in xs]
    in_specs, args, mt_a = [], [], None
    for x, k in zip(xs, ks):
        if len(x) == 1:
            in_specs.append(pl.BlockSpec((tm, k), lambda n, m, te, nu: (jnp.minimum(m, nu[0] - 1), 0)))
        else:
            specs, mt_a = _stacked_specs(x, (tm, k), tm, lambda idx: 0, 1)
            in_specs += specs
        args += list(x)
    for g in range(n_g):
        for i, k in enumerate(ks):
            in_specs.append(pl.BlockSpec((1, k, tn), lambda n, m, te, nu, i=i: (te[m], i, n)))
            args.append(ws[g])
    if epi == "resgate":
        blk0 = gt_blk * (n_cols // tn)
        specs, mt_res = _stacked_specs(res, (tm, tn), tm, lambda idx: idx[0], 1)
        mt_a = mt_res if mt_res is not None else mt_a
        in_specs += specs
        in_specs.append(pl.BlockSpec((1, 1, tn), lambda n, m, te, nu: (mod_row(m * tm), 0, blk0 + n)))
        args += list(res) + [mod]
    scratch = [pltpu.VMEM((k, tn), BF16) for _ in range(n_g) for k in ks]
    return pl.pallas_call(
        functools.partial(_mm_kernel, x_parts=tuple(len(x) for x in xs),
                          res_parts=len(res) if res is not None else 0, mt_a=mt_a, n_g=n_g, epi=epi),
        out_shape=jax.ShapeDtypeStruct((n_rows, n_cols), out_dtype),
        grid_spec=pltpu.PrefetchScalarGridSpec(
            num_scalar_prefetch=2,
            grid=(nt, mt),
            in_specs=in_specs,
            out_specs=pl.BlockSpec((tm, tn), lambda n, m, te, nu: (m, n)),
            scratch_shapes=scratch,
        ),
        compiler_params=_cparams(("arbitrary", "arbitrary")),
        name="matmul_" + epi,
    )(tile_expert, n_used, *args)


def _rope(x, cos, sin_signed, half):
    lane = lax.broadcasted_iota(jnp.int32, (x.shape[0], LANES), 1)
    first = (lane & (2 * half - 1)) < half
    blocks = []
    for j in range(x.shape[1] // LANES):
        xj = x[:, j * LANES:(j + 1) * LANES]
        blocks.append(jnp.where(first, pltpu.roll(xj, LANES - half, 1), pltpu.roll(xj, half, 1)))
    partner = blocks[0] if len(blocks) == 1 else jnp.concatenate(blocks, axis=-1)
    return x * cos + partner * sin_signed


def _group_mean_sq(x, g_ref):
    x2 = x * x
    hi = x2.astype(BF16)
    lo = (x2 - hi.astype(F32)).astype(BF16)
    g = g_ref[...]
    return (jnp.dot(hi, g, preferred_element_type=F32) + jnp.dot(lo, g, preferred_element_type=F32))


def _dup_halves(x):
    lane = lax.broadcasted_iota(jnp.int32, x.shape, 1)
    lo = lane < HEAD_DIM
    swapped = pltpu.roll(x, HEAD_DIM, 1)
    return jnp.concatenate([jnp.where(lo, x, swapped), jnp.where(lo, swapped, x)], axis=-1)


def _store_transposed(dst_ref, x):
    for j in range(x.shape[1] // LANES):
        dst_ref[0, j * LANES:(j + 1) * LANES, :] = x[:, j * LANES:(j + 1) * LANES].T.astype(BF16)


def _prep_kernel(p_ref, c64_ref, s64_ref, cq_ref, sq_ref, ck_ref, sk_ref,
                 bqn_ref, bkn_ref, cqn_ref, ckvn_ref, g512_ref, g128_ref,
                 wq_ref, wkk_ref, wkv_ref, place_ref,
                 qa_ref, qb_ref, qc_ref, qd_ref,
                 ka_ref, kb_ref, kc_ref, kd_ref, vd_ref, vta_ref, vtb_ref, vtc_ref):
    c64, s64 = c64_ref[...], s64_ref[...]
    c128, s128 = c64[:, :LANES], s64[:, :LANES]
    scale64 = HEAD_DIM ** -0.5
    scale_mla = (C_NOPE_DIM + C_ROPE_DIM) ** -0.5

    qa_ref[...] = (_rope(p_ref[:, OFF_QA:OFF_QA + 512], c64, s64, 16) * (scale64 * LOG2E)).astype(BF16)
    ka_ref[0] = _rope(p_ref[:, OFF_KA:OFF_KA + 512], c64, s64, 16).astype(BF16)
    _store_transposed(vta_ref, p_ref[:, OFF_VA:OFF_VA + 512])

    xb = p_ref[:, OFF_QB:OFF_QB + 512]
    yb = xb * lax.rsqrt(_group_mean_sq(xb, g512_ref) + EPS) * bqn_ref[...]
    qb_ref[...] = (_rope(yb, c64, s64, 16) * (scale64 * LOG2E)).astype(BF16)
    xk = p_ref[:, OFF_KB:OFF_KB + 128]
    yk = xk * lax.rsqrt(_group_mean_sq(xk, g128_ref) + EPS) * bkn_ref[...]
    kb_ref[0] = _dup_halves(_rope(yk, c128, s128, 16)).astype(BF16)
    _store_transposed(vtb_ref, p_ref[:, OFF_VB:OFF_VB + 128])

    xq = p_ref[:, OFF_QC:OFF_QC + C_Q_RANK]
    yq = xq * lax.rsqrt(jnp.mean(xq * xq, axis=-1, keepdims=True) + EPS) * cqn_ref[...]
    qf = jnp.dot(yq.astype(BF16), wq_ref[...], preferred_element_type=F32)
    qc_ref[...] = (_rope(qf, cq_ref[...], sq_ref[...], 8) * (scale_mla * LOG2E)).astype(BF16)
    xc = p_ref[:, OFF_CKV:OFF_CKV + C_KV_RANK]
    yc = (xc * lax.rsqrt(jnp.mean(xc * xc, axis=-1, keepdims=True) + EPS) * ckvn_ref[...]).astype(BF16)
    kr = _rope(p_ref[:, OFF_KR:OFF_KR + LANES], ck_ref[...], sk_ref[...], 8).astype(BF16)
    kc = (jnp.dot(yc, wkk_ref[...], preferred_element_type=F32)
          + jnp.dot(kr, place_ref[...], preferred_element_type=F32))
    kc_ref[0] = kc.astype(BF16)
    _store_transposed(vtc_ref, jnp.dot(yc, wkv_ref[...], preferred_element_type=F32))

    qd_ref[...] = (_rope(p_ref[:, OFF_QD:OFF_QD + 512], c64, s64, 16) * scale64).astype(BF16)
    kd_ref[0] = _dup_halves(_rope(p_ref[:, OFF_KD:OFF_KD + 128], c128, s128, 16)).astype(BF16)
    vd_ref[0] = _dup_halves(p_ref[:, OFF_VD:OFF_VD + 128]).astype(BF16)


def _prep(p, tables, consts, dims):
    bsz, s_len, n_ctx = dims
    n_rows = p.shape[0]
    sk = n_ctx + s_len
    tr = _pick(math.gcd(s_len, n_ctx), (TR, 128))
    n_lat_t, lat_pb, ctx_pb = bsz * s_len // tr, s_len // tr, n_ctx // tr

    def is_lat(t):
        return t < n_lat_t

    def tbl_idx(t):
        return jnp.where(is_lat(t), t % lat_pb, lat_pb)

    def kv_b(t):
        return jnp.where(is_lat(t), t // lat_pb, (t - n_lat_t) // ctx_pb)

    def kv_j(t):
        return jnp.where(is_lat(t), ctx_pb + t % lat_pb, (t - n_lat_t) % ctx_pb)

    def row_spec(w):
        return pl.BlockSpec((tr, w), lambda t: (t, 0))

    def tbl_spec(w):
        return pl.BlockSpec((tr, w), lambda t: (tbl_idx(t), 0))

    def const_spec(a):
        return pl.BlockSpec(a.shape, lambda t: (0,) * a.ndim)

    def kv_spec(w):
        return pl.BlockSpec((1, tr, w), lambda t: (kv_b(t), kv_j(t), 0))

    def vt_spec(w):
        return pl.BlockSpec((1, w, tr), lambda t: (kv_b(t), 0, kv_j(t)))

    q_widths = (512, 512, 1024, 512)
    kv_widths = (512, 256, 1024, 256, 256)
    vt_widths = (512, 128, 512)
    out_shape = ([jax.ShapeDtypeStruct((n_rows, w), BF16) for w in q_widths]
                 + [jax.ShapeDtypeStruct((bsz, sk, w), BF16) for w in kv_widths]
                 + [jax.ShapeDtypeStruct((bsz, w, sk), BF16) for w in vt_widths])
    out_specs = ([row_spec(w) for w in q_widths] + [kv_spec(w) for w in kv_widths]
                 + [vt_spec(w) for w in vt_widths])
    in_specs = ([row_spec(IN_COLS)] + [tbl_spec(t.shape[1]) for t in tables]
                + [const_spec(a) for a in consts])
    return pl.pallas_call(
        _prep_kernel,
        out_shape=out_shape,
        grid=(n_rows // tr,),
        in_specs=in_specs,
        out_specs=out_specs,
        compiler_params=_cparams(("arbitrary",)),
        name="attn_prep",
    )(p, *tables, *consts)


def _scores(a, b):
    return lax.dot_general(a, b, (((1,), (1,)), ((), ())), preferred_element_type=F32)


def _half_masks(shape):
    lane = lax.broadcasted_iota(jnp.int32, shape, 1)
    lo = lane < HEAD_DIM
    return lo, jnp.logical_not(lo)


def _softmax_pv_t(s, v_t):
    m = jnp.max(s, axis=0, keepdims=True)
    e = jnp.exp2(s - m).astype(BF16)
    dv = v_t.shape[0]
    v_ext = jnp.concatenate([v_t, jnp.ones((BF16_SUBLANES, v_t.shape[1]), BF16)], axis=0)
    oe = jnp.dot(v_ext, e, preferred_element_type=F32)
    return oe[:dv] / oe[dv:dv + 1]


def _run_units(score_fns, finish_fns):
    s = score_fns[0]()
    for i, finish in enumerate(finish_fns):
        s_next = score_fns[i + 1]() if i + 1 < len(score_fns) else None
        finish(s)
        s = s_next


def _gqa_t_kernel(q_ref, k_ref, vt_ref, o_ref):
    k, v_t = k_ref[0], vt_ref[0]
    masks = _half_masks((q_ref.shape[0], LANES))
    n_pairs = q_ref.shape[1] // LANES
    outs = {}

    def score(j, z):
        q = q_ref[:, j * LANES:(j + 1) * LANES]
        return lambda: _scores(k, jnp.where(masks[z], q, jnp.zeros_like(q)))

    def finish(j, z):
        def fin(s):
            outs[z] = _softmax_pv_t(s, v_t)
            if z == 1:
                pair = jnp.concatenate([outs[0], outs[1]], axis=0)
                o_ref[:, j * LANES:(j + 1) * LANES] = pair.T.astype(o_ref.dtype)
        return fin

    units = [(j, z) for j in range(n_pairs) for z in range(2)]
    _run_units([score(j, z) for j, z in units], [finish(j, z) for j, z in units])


def _mla_t_kernel(q_ref, k_ref, vt_ref, o_ref):
    n_heads = q_ref.shape[1] // LANES
    outs = {}

    def score(u):
        return lambda: _scores(k_ref[0, :, u * LANES:(u + 1) * LANES], q_ref[:, u * LANES:(u + 1) * LANES])

    def finish(u):
        def fin(s):
            outs[u % 2] = _softmax_pv_t(s, vt_ref[0, u * C_V_DIM:(u + 1) * C_V_DIM, :])
            if u % 2 == 1:
                pair = jnp.concatenate([outs[0], outs[1]], axis=0)
                j = u // 2
                o_ref[:, j * LANES:(j + 1) * LANES] = pair.T.astype(o_ref.dtype)
        return fin

    _run_units([score(u) for u in range(n_heads)], [finish(u) for u in range(n_heads)])


def _diff_t_kernel(lam_ref, subln_ref, q0_ref, q1_ref, k0_ref, k1_ref, vt_ref, o_ref, *, lam_init):
    t = lam_ref[...]
    lam = (jnp.exp(jnp.sum(t[0:1] * t[1:2], axis=-1, keepdims=True))
           - jnp.exp(jnp.sum(t[2:3] * t[3:4], axis=-1, keepdims=True)) + lam_init)
    q_refs, k_refs = (q0_ref, q1_ref), (k0_ref, k1_ref)
    masks = _half_masks(q0_ref.shape)
    first_map = {}

    def score(z, mp):
        q = q_refs[mp][...]
        return lambda: _scores(k_refs[mp][0], jnp.where(masks[z], q, jnp.zeros_like(q)))

    def finish(z, mp):
        def fin(s):
            a = _softmax_pv_t(s, vt_ref[0, z * A_V_DIM:(z + 1) * A_V_DIM, :])
            if mp == 0:
                first_map[z] = a
                return
            d = (first_map[z] - lam * a).T
            y = d * lax.rsqrt(jnp.mean(d * d, axis=-1, keepdims=True) + EPS) * subln_ref[...]
            o_ref[:, z * A_V_DIM:(z + 1) * A_V_DIM] = (y * (1.0 - lam_init)).astype(o_ref.dtype)
        return fin

    units = [(z, mp) for z in range(2) for mp in range(2)]
    _run_units([score(z, mp) for z, mp in units], [finish(z, mp) for z, mp in units])


def _attend_sink(q, k, v, sink):
    s = _scores(q, k)
    m = jnp.maximum(jnp.max(s, axis=-1, keepdims=True), sink)
    e = jnp.exp(s - m)
    l = jnp.sum(e, axis=-1, keepdims=True) + jnp.exp(sink - m)
    return jnp.dot(e.astype(BF16), v, preferred_element_type=F32) / l


def _window_kernel(sink_ref, q_ref, k_ref, v_ref, o_ref, *, s_len, n_ctx, n_q_lat, with_ctx):
    tq = q_ref.shape[0]
    band = tq + 2 * WINDOW
    group = pl.program_id(1)
    i = pl.program_id(2)
    heads_per_group = D_HEADS // D_KV_HEADS

    def body(is_lat):
        k_c, v_c = k_ref[0, :n_ctx, :], v_ref[0, :n_ctx, :]
        masks = _half_masks((tq, LANES))
        if is_lat:
            start = pl.multiple_of(jnp.clip(i * tq - WINDOW, 0, s_len - band), LANES)
            row0 = pl.multiple_of(n_ctx + start, LANES)
            k_b = k_ref[0, pl.ds(row0, band), :]
            v_b = v_ref[0, pl.ds(row0, band), :]
            qpos = i * tq + lax.broadcasted_iota(jnp.int32, (tq, band), 0)
            kpos = start + lax.broadcasted_iota(jnp.int32, (tq, band), 1)
            in_band = jnp.abs(qpos - kpos) <= WINDOW
        for j in range(q_ref.shape[1] // LANES):
            q = q_ref[:, j * LANES:(j + 1) * LANES]
            outs = []
            for z in range(2):
                qz = jnp.where(masks[z], q, jnp.zeros_like(q))
                sink = sink_ref[group * heads_per_group + 2 * j + z]
                if not is_lat:
                    outs.append(_attend_sink(qz, k_c, v_c, sink))
                    continue
                s_c = _scores(qz, k_c)
                s_b = jnp.where(in_band, _scores(qz, k_b), -jnp.inf)
                m = jnp.maximum(jnp.maximum(jnp.max(s_c, axis=-1, keepdims=True),
                                            jnp.max(s_b, axis=-1, keepdims=True)), sink)
                e_c, e_b = jnp.exp(s_c - m), jnp.exp(s_b - m)
                l = (jnp.sum(e_c, axis=-1, keepdims=True) + jnp.sum(e_b, axis=-1, keepdims=True)
                     + jnp.exp(sink - m))
                o = (jnp.dot(e_c.astype(BF16), v_c, preferred_element_type=F32)
                     + jnp.dot(e_b.astype(BF16), v_b, preferred_element_type=F32))
                outs.append(o / l)
            o_ref[:, j * LANES:(j + 1) * LANES] = jnp.where(masks[0], outs[0], outs[1]).astype(o_ref.dtype)

    if not with_ctx:
        body(True)
        return

    @pl.when(i < n_q_lat)
    def _():
        body(True)

    @pl.when(i >= n_q_lat)
    def _():
        body(False)


def _mixers(ops, dims, params, lam_init, with_ctx):
    bsz, s_len, n_ctx = dims
    qa, qb, qc, qd, ka, kb, kc, kd, vd, vta, vtb, vtc = ops
    lam_vecs, subln, sink = params
    n_lat = bsz * s_len
    sk_all = n_ctx + s_len

    def const_spec(a):
        return pl.BlockSpec(a.shape, lambda b, p, i: (0,) * a.ndim)

    def dense_set(tq, nk, row0_tiles, n_q, out_rows):
        def q_row(b, i):
            return row0_tiles + b * n_q + i

        def qspec(w, col):
            return pl.BlockSpec((tq, w), lambda b, p, i: (q_row(b, i), col(p)))

        def kspec(w, col):
            return pl.BlockSpec((1, nk, w), lambda b, p, i: (b, 0, col(p)))

        def vtspec(w, row):
            return pl.BlockSpec((1, w, nk), lambda b, p, i: (b, row(p), 0))

        def call(kernel, in_specs, args, name):
            return pl.pallas_call(
                kernel,
                out_shape=jax.ShapeDtypeStruct((out_rows, 512), BF16),
                grid=(bsz, 2, n_q),
                in_specs=in_specs,
                out_specs=pl.BlockSpec((tq, 256), lambda b, p, i: (b * n_q + i, p)),
                compiler_params=_cparams(("arbitrary", "arbitrary", "arbitrary")),
                name=name,
            )(*args)

        oa = call(functools.partial(_diff_t_kernel, lam_init=lam_init),
                  [const_spec(lam_vecs), const_spec(subln),
                   qspec(128, lambda p: p), qspec(128, lambda p: 2 + p),
                   kspec(128, lambda p: p), kspec(128, lambda p: 2 + p), vtspec(256, lambda p: p)],
                  [lam_vecs, subln, qa, qa, ka, ka, vta], "attn_diff")
        ob = call(_gqa_t_kernel,
                  [qspec(256, lambda p: p), kspec(128, lambda p: p), vtspec(64, lambda p: p)],
                  [qb, kb, vtb], "attn_qknorm")
        oc = call(_mla_t_kernel,
                  [qspec(512, lambda p: p), kspec(512, lambda p: p), vtspec(256, lambda p: p)],
                  [qc, kc, vtc], "attn_mla")
        return [oa, ob, oc]

    tq = _pick(s_len, (TQ, 256, 128))
    dense = [[o] for o in dense_set(tq, sk_all, 0, s_len // tq, n_lat)]
    if with_ctx:
        tq_c = _pick(n_ctx, (TQ, 256, 128))
        ctx = dense_set(tq_c, n_ctx, n_lat // tq_c, n_ctx // tq_c, bsz * n_ctx)
        dense = [a + [b] for a, b in zip(dense, ctx)]

    tw = _pick(math.gcd(s_len, n_ctx), (TQ_WIN, 128))
    n_q_lat, n_q_ctx = s_len // tw, n_ctx // tw
    n_q = n_q_lat + n_q_ctx if with_ctx else n_q_lat
    out_rows = n_lat + bsz * n_ctx if with_ctx else n_lat

    def w_row(b, i):
        return jnp.where(i < n_q_lat, b * n_q_lat + i, n_lat // tw + b * n_q_ctx + (i - n_q_lat))

    od = pl.pallas_call(
        functools.partial(_window_kernel, s_len=s_len, n_ctx=n_ctx, n_q_lat=n_q_lat, with_ctx=with_ctx),
        out_shape=jax.ShapeDtypeStruct((out_rows, 512), BF16),
        grid=(bsz, D_KV_HEADS, n_q),
        in_specs=[pl.BlockSpec(memory_space=pltpu.SMEM),
                  pl.BlockSpec((tw, 256), lambda b, g, i: (w_row(b, i), g)),
                  pl.BlockSpec((1, sk_all, 128), lambda b, g, i: (b, 0, g)),
                  pl.BlockSpec((1, sk_all, 128), lambda b, g, i: (b, 0, g))],
        out_specs=pl.BlockSpec((tw, 256), lambda b, g, i: (w_row(b, i), g)),
        compiler_params=_cparams(("arbitrary", "arbitrary", "arbitrary")),
        name="attn_window",
    )(sink, qd, kd, vd)
    return dense + [[od]]


def _route_kernel(h_ref, g_ref, sh_ref, sc_ref, w_ref, b_ref, u_ref, sel_ref, idx_ref, gw_ref):
    u = _rms_modulate(h_ref[...], g_ref[...], sc_ref[0], sh_ref[0])
    u_ref[...] = u
    w = w_ref[...]
    u_hi = u.astype(BF16)
    u_lo = (u - u_hi.astype(F32)).astype(BF16)
    w_hi = w.astype(BF16)
    w_lo = (w - w_hi.astype(F32)).astype(BF16)
    logits = (_scores(w_hi, u_hi) + _scores(w_hi, u_lo) + _scores(w_lo, u_hi) + _scores(w_lo, u_lo)
              + b_ref[...])
    ids = lax.broadcasted_iota(jnp.int32, logits.shape, 0).astype(F32)
    m1 = jnp.max(logits, axis=0, keepdims=True)
    i1 = jnp.min(jnp.where(logits == m1, ids, float(N_EXPERTS)), axis=0, keepdims=True)
    first = ids == i1
    rest = jnp.where(first, -jnp.inf, logits)
    m2 = jnp.max(rest, axis=0, keepdims=True)
    i2 = jnp.min(jnp.where(rest == m2, ids, float(N_EXPERTS)), axis=0, keepdims=True)
    second = ids == i2
    e = jnp.exp(m2 - m1)
    w1 = 1.0 / (1.0 + e)
    w2 = e / (1.0 + e)
    sel_ref[...] = jnp.where(first | second, 1, 0).astype(jnp.int32)
    idx_ref[...] = jnp.where(ids == 0.0, i1, jnp.where(ids == 1.0, i2, 0.0)).astype(jnp.int32)
    gw_ref[...] = jnp.where(ids == 0.0, w1, jnp.where(ids == 1.0, w2, 0.0))


def _route(h, g, mod, sh_blk, sc_blk, w_router_t, b_router, n_tok, s_len, mod_row):
    d = h.shape[1]
    tm = _row_tile(n_tok, s_len, (TM, 256, 128))
    outs = ([jax.ShapeDtypeStruct((n_tok, d), F32)]
            + [jax.ShapeDtypeStruct((N_EXPERTS, n_tok), dt) for dt in (jnp.int32, jnp.int32, F32)])
    return pl.pallas_call(
        _route_kernel,
        out_shape=outs,
        grid=(n_tok // tm,),
        in_specs=[pl.BlockSpec((tm, d), lambda m: (m, 0)),
                  pl.BlockSpec((1, d), lambda m: (0, 0)),
                  pl.BlockSpec((1, 1, d), lambda m: (mod_row(m * tm), 0, sh_blk)),
                  pl.BlockSpec((1, 1, d), lambda m: (mod_row(m * tm), 0, sc_blk)),
                  pl.BlockSpec((N_EXPERTS, d), lambda m: (0, 0)),
                  pl.BlockSpec((N_EXPERTS, 1), lambda m: (0, 0))],
        out_specs=[pl.BlockSpec((tm, d), lambda m: (m, 0))]
        + [pl.BlockSpec((N_EXPERTS, tm), lambda m: (0, m))] * 3,
        compiler_params=_cparams(("arbitrary",)),
        name="moe_route",
    )(h, g.reshape(1, d), mod, mod, w_router_t, b_router.reshape(N_EXPERTS, 1))


def _row_copy(src, dst, sem, src_row, dst_row):
    return pltpu.make_async_copy(src.at[pl.ds(src_row, 1)], dst.at[pl.ds(dst_row, 1)], sem)


def _gather_kernel(tok_ref, nxt_ref, src_ref, o_ref, buf, sem):
    i = pl.program_id(0)
    n = pl.num_programs(0)
    tg = buf.shape[1]
    slot = i % 2

    def issue(t_ref, s):
        def body(g, carry):
            for k in range(DMA_UNROLL):
                r = g * DMA_UNROLL + k
                _row_copy(src_ref, buf.at[s], sem.at[s], t_ref[0, 0, r], r).start(priority=k % 2)
            return carry
        lax.fori_loop(0, tg // DMA_UNROLL, body, 0)

    @pl.when(i == 0)
    def _():
        issue(tok_ref, 0)

    @pl.when(i + 1 < n)
    def _():
        issue(nxt_ref, 1 - slot)

    def wait(r, carry):
        _row_copy(src_ref, buf.at[slot], sem.at[slot], 0, r).wait()
        return carry

    lax.fori_loop(0, tg, wait, 0, unroll=DMA_UNROLL)
    o_ref[...] = buf[slot].astype(o_ref.dtype)


def _gather_rows(src, tok, tg):
    n_rows = tok.shape[0]
    d = src.shape[1]
    n_t = n_rows // tg
    tok = tok.reshape(n_t, 1, tg)
    return pl.pallas_call(
        _gather_kernel,
        out_shape=jax.ShapeDtypeStruct((n_rows, d), BF16),
        grid=(n_t,),
        in_specs=[pl.BlockSpec((1, 1, tg), lambda i: (i, 0, 0), memory_space=pltpu.SMEM),
                  pl.BlockSpec((1, 1, tg), lambda i: (jnp.minimum(i + 1, n_t - 1), 0, 0),
                               memory_space=pltpu.SMEM),
                  pl.BlockSpec(memory_space=pl.ANY)],
        out_specs=pl.BlockSpec((tg, d), lambda i: (i, 0)),
        scratch_shapes=[pltpu.VMEM((2, tg, d), F32), pltpu.SemaphoreType.DMA((2,))],
        compiler_params=_cparams(("arbitrary",)),
        name="moe_gather",
    )(tok, tok, src)


def _combine_kernel(pos_ref, nxt_ref, y_ref, h_ref, gw_ref, gt_ref, gf_ref, o_ref, buf, sem):
    i = pl.program_id(0)
    n = pl.num_programs(0)
    tc = buf.shape[2]
    slot = i % 2

    def issue(p_ref, s):
        def body(g, carry):
            for k in range(DMA_UNROLL):
                r = g * DMA_UNROLL + k
                _row_copy(y_ref, buf.at[s, 0], sem.at[s], p_ref[0, 0, r], r).start(priority=0)
                _row_copy(y_ref, buf.at[s, 1], sem.at[s], p_ref[0, 1, r], r).start(priority=1)
            return carry
        lax.fori_loop(0, tc // DMA_UNROLL, body, 0)

    @pl.when(i == 0)
    def _():
        issue(pos_ref, 0)

    @pl.when(i + 1 < n)
    def _():
        issue(nxt_ref, 1 - slot)

    def wait(r, carry):
        _row_copy(y_ref, buf.at[slot, 0], sem.at[slot], 0, r).wait()
        _row_copy(y_ref, buf.at[slot, 1], sem.at[slot], 0, r).wait()
        return carry

    lax.fori_loop(0, tc, wait, 0, unroll=DMA_UNROLL)
    gw = gw_ref[...]
    moe = gw[:, 0:1] * buf[slot, 0] + gw[:, 1:2] * buf[slot, 1]
    x = h_ref[...] + gt_ref[0] * moe
    ms = jnp.mean(x * x, axis=-1, keepdims=True)
    o_ref[...] = x * lax.rsqrt(ms + EPS) * gf_ref[...]


def _combine(y, h, pos, gw_t, mod, gt_blk, mod_row, g_final, n_tok):
    d = h.shape[1]
    n_t, _, tc = pos.shape
    return pl.pallas_call(
        _combine_kernel,
        out_shape=jax.ShapeDtypeStruct((n_tok, d), F32),
        grid=(n_t,),
        in_specs=[pl.BlockSpec((1, 2, tc), lambda i: (i, 0, 0), memory_space=pltpu.SMEM),
                  pl.BlockSpec((1, 2, tc), lambda i: (jnp.minimum(i + 1, n_t - 1), 0, 0),
                               memory_space=pltpu.SMEM),
                  pl.BlockSpec(memory_space=pl.ANY),
                  pl.BlockSpec((tc, d), lambda i: (i, 0)),
                  pl.BlockSpec((tc, N_EXPERTS), lambda i: (i, 0)),
                  pl.BlockSpec((1, 1, d), lambda i: (mod_row(i * tc), 0, gt_blk)),
                  pl.BlockSpec((1, d), lambda i: (0, 0))],
        out_specs=pl.BlockSpec((tc, d), lambda i: (i, 0)),
        scratch_shapes=[pltpu.VMEM((2, 2, tc, d), F32), pltpu.SemaphoreType.DMA((2,))],
        compiler_params=_cparams(("arbitrary",)),
        name="moe_combine",
    )(pos, pos, y, h, gw_t, mod, g_final.reshape(1, d))


def _dispatch_plan(sel, idx, tm, n_slots, tc):
    n_tok = sel.shape[1]
    n_tiles = n_slots // tm
    counts = jnp.sum(sel, axis=1)
    padded = ((counts + tm - 1) // tm) * tm
    ends = jnp.cumsum(padded)
    offs = ends - padded
    pos = offs[:, None] + jnp.cumsum(sel, axis=1) - sel
    pos0 = jnp.take_along_axis(pos, idx[0:1], axis=0)[0]
    pos1 = jnp.take_along_axis(pos, idx[1:2], axis=0)[0]
    pos_tiles = jnp.stack([pos0.reshape(n_tok // tc, tc), pos1.reshape(n_tok // tc, tc)], axis=1)
    tile_start = jnp.arange(n_tiles, dtype=jnp.int32) * tm
    tile_expert = jnp.minimum(jnp.sum(ends[None, :] <= tile_start[:, None], axis=1), N_EXPERTS - 1)
    n_used = ends[-1] // tm
    tok = jnp.arange(n_tok, dtype=jnp.int32)
    tok_of_row = jnp.zeros((n_slots,), jnp.int32).at[jnp.concatenate([pos0, pos1])].set(
        jnp.concatenate([tok, tok]), unique_indices=True)
    return (pos_tiles.astype(jnp.int32), tok_of_row, tile_expert.astype(jnp.int32),
            n_used.reshape(1).astype(jnp.int32))


def _rope_tables(s_len, pad_rows):
    t = jnp.arange(s_len)
    rows, cols = (t // GRID_W).astype(F32), (t % GRID_W).astype(F32)

    def axis_tables(rot_dim):
        axis_dim = rot_dim // 2
        inv = ROPE_THETA ** (-jnp.arange(0, axis_dim, 2, dtype=F32) / axis_dim)
        ar, ac = rows[:, None] * inv[None, :], cols[:, None] * inv[None, :]
        cos = jnp.concatenate([jnp.cos(ar), jnp.cos(ar), jnp.cos(ac), jnp.cos(ac)], axis=1)
        sin = jnp.concatenate([-jnp.sin(ar), jnp.sin(ar), -jnp.sin(ac), jnp.sin(ac)], axis=1)
        return cos, sin

    def with_identity(cos, sin):
        w = cos.shape[1]
        return (jnp.concatenate([cos, jnp.ones((pad_rows, w), F32)], axis=0),
                jnp.concatenate([sin, jnp.zeros((pad_rows, w), F32)], axis=0))

    c64, s64 = axis_tables(HEAD_DIM)
    c64, s64 = jnp.tile(c64, (1, 8)), jnp.tile(s64, (1, 8))
    c32, s32 = axis_tables(C_ROPE_DIM)
    ones, zeros = jnp.ones((s_len, 1), F32), jnp.zeros((s_len, 1), F32)
    cq = jnp.tile(jnp.concatenate([jnp.tile(ones, (1, 64)), c32, jnp.tile(ones, (1, 32))], axis=1), (1, 8))
    sq = jnp.tile(jnp.concatenate([jnp.tile(zeros, (1, 64)), s32, jnp.tile(zeros, (1, 32))], axis=1), (1, 8))
    ck = jnp.concatenate([c32, jnp.tile(ones, (1, 96))], axis=1)
    sk = jnp.concatenate([s32, jnp.tile(zeros, (1, 96))], axis=1)
    out = []
    for c, s in ((c64, s64), (cq, sq), (ck, sk)):
        out += list(with_identity(c, s))
    return out


def _mla_weights(w_q_up, w_kv_up):
    qd = C_NOPE_DIM + C_ROPE_DIM
    wq = jnp.pad(w_q_up.reshape(C_Q_RANK, C_HEADS, qd), ((0, 0), (0, 0), (0, LANES - qd)))
    wkv = w_kv_up.reshape(C_KV_RANK, C_HEADS, C_NOPE_DIM + C_V_DIM)
    wkk = jnp.pad(wkv[:, :, :C_NOPE_DIM], ((0, 0), (0, 0), (0, LANES - C_NOPE_DIM)))
    wkv_v = wkv[:, :, C_NOPE_DIM:]
    return (wq.reshape(C_Q_RANK, C_HEADS * LANES).astype(BF16),
            wkk.reshape(C_KV_RANK, C_HEADS * LANES).astype(BF16),
            wkv_v.reshape(C_KV_RANK, C_HEADS * C_V_DIM).astype(BF16))


def _static_mats():
    g = (np.arange(512)[:, None] // HEAD_DIM == np.arange(512)[None, :] // HEAD_DIM) / HEAD_DIM
    place = np.zeros((LANES, C_HEADS * LANES), np.float32)
    for h in range(C_HEADS):
        place[np.arange(C_ROPE_DIM), h * LANES + C_NOPE_DIM + np.arange(C_ROPE_DIM)] = 1.0
    return (jnp.asarray(g, BF16), jnp.asarray(g[:128, :128], BF16), jnp.asarray(place, BF16))


def kernel(x, c, ctx, c_ctx, w_mod, b_mod, g_mix, g_ffn, g_final, w_in, w_out, a_lam_q1, a_lam_k1, a_lam_q2, a_lam_k2, a_subln, b_q_norm, b_k_norm, c_q_norm, c_kv_norm, c_w_q_up, c_w_kv_up, d_sink, ffn_w_gate, ffn_w_up, ffn_w_down, moe_w_router, moe_b_router, moe_w_gate, moe_w_up, moe_w_down):
    bsz, s_len, d = x.shape
    n_ctx = ctx.shape[1]
    depth = w_mod.shape[0]
    n_lat = bsz * s_len
    n_all = n_lat + bsz * n_ctx
    dims = (bsz, s_len, n_ctx)

    def mod_row(row0):
        return jnp.where(row0 < n_lat, row0 // s_len, bsz)

    mod_rows = -(-(bsz + 1) // 8) * 8
    cc = jnp.zeros((mod_rows, d), F32).at[:bsz].set(c).at[bsz].set(c_ctx)
    mods = _modulation(cc, w_mod, b_mod)

    tr = _pick(math.gcd(s_len, n_ctx), (TR, 128))
    tables = _rope_tables(s_len, tr)
    g512, g128, place = _static_mats()

    h = [x.reshape(n_lat, d), ctx.reshape(bsz * n_ctx, d)]
    for l in range(depth):
        last = l == depth - 1
        lam_init = 0.8 - 0.6 * math.exp(-0.3 * l)
        n_rows = n_lat if last else n_all
        mod = mods[l].reshape(mod_rows, 1, 6 * d)
        nm = dict(s_len=s_len, mod_row=mod_row)

        p = _nm_matmul(h, g_mix[l], mod, 0, 1, [w_in[l].astype(BF16)], n_rows=n_all, out_dtype=F32, **nm)
        wq, wkk, wkv_v = _mla_weights(c_w_q_up[l], c_w_kv_up[l])
        consts = [jnp.tile(b_q_norm[l], 8)[None], jnp.tile(b_k_norm[l], 2)[None],
                  c_q_norm[l][None], c_kv_norm[l][None], g512, g128, wq, wkk, wkv_v, place]
        ops = _prep(p, tables, consts, dims)
        lam_vecs = jnp.stack([a_lam_q1[l], a_lam_k1[l], a_lam_q2[l], a_lam_k2[l]])
        mix = _mixers(ops, dims, (lam_vecs, a_subln[l][None], d_sink[l]), lam_init, not last)
        h = [_matmul(mix, [w_out[l][None]], n_rows=n_rows, out_dtype=F32, epi="resgate",
                     res=h, mod=mod, gt_blk=2, mod_row=mod_row, tn=1024)]

        i = l // 2
        if l % 2 == 0:
            mid = _nm_matmul(h, g_ffn[l], mod, 3, 4, [ffn_w_gate[i], ffn_w_up[i]], n_rows=n_rows,
                             out_dtype=BF16, **nm)
            h = [_matmul([[mid]], [ffn_w_down[i][None]], n_rows=n_rows, out_dtype=F32, epi="resgate",
                         res=h, mod=mod, gt_blk=5, mod_row=mod_row)]
        else:
            if not last:
                raise NotImplementedError("expert layers are only supported as the last layer")
            h = h[0]
            u2, sel, idx, gw = _route(h, g_ffn[l], mod, 3, 4, moe_w_router[i].T, moe_b_router[i],
                                      n_rows, s_len, mod_row)
            tm = _pick(n_rows, (TM, 256, 128))
            tc = _pick(s_len, (TC, 128))
            n_slots = 2 * n_rows + N_EXPERTS * tm
            pos, tok_of_row, tile_expert, n_used = _dispatch_plan(sel, idx, tm, n_slots, tc)
            xs = _gather_rows(u2, tok_of_row, tm)
            mid = _matmul([[xs]], [moe_w_gate[i], moe_w_up[i]], n_rows=n_slots, out_dtype=BF16,
                          epi="swiglu", tile_expert=tile_expert, n_used=n_used)
            y = _matmul([[mid]], [moe_w_down[i]], n_rows=n_slots, out_dtype=F32,
                        tile_expert=tile_expert, n_used=n_used)
            out = _combine(y, h, pos, gw.T, mod, 5, mod_row, g_final, n_rows)
            return out.reshape(bsz, s_len, d)

    tm = _pick(n_lat, (TM, 256, 128))
    out = pl.pallas_call(
        _final_norm_kernel,
        out_shape=jax.ShapeDtypeStruct((n_lat, d), F32),
        grid=(n_lat // tm,),
        in_specs=[pl.BlockSpec((tm, d), lambda m: (m, 0)), pl.BlockSpec((1, d), lambda m: (0, 0))],
        out_specs=pl.BlockSpec((tm, d), lambda m: (m, 0)),
        compiler_params=_cparams(("arbitrary",)),
        name="final_norm",
    )(h[0], g_final.reshape(1, d))
    return out.reshape(bsz, s_len, d)
```

```python
import functools
import math

import numpy as np
import jax
import jax.numpy as jnp
from jax import lax
from jax.experimental import pallas as pl
from jax.experimental.pallas import tpu as pltpu

F32 = jnp.float32
BF16 = jnp.bfloat16

GRID_W = 64
HEAD_DIM = 64
ROPE_THETA = 10000.0
EPS = 1e-6
A_HEADS, A_QK_DIM, A_V_DIM = 4, 64, 128
B_HEADS, B_KV_HEADS = 8, 2
C_HEADS, C_Q_RANK, C_KV_RANK, C_NOPE_DIM, C_ROPE_DIM, C_V_DIM = 8, 768, 256, 64, 32, 64
D_HEADS, D_KV_HEADS = 8, 2
WINDOW = 128
N_EXPERTS = 8
LOG2E = 1.4426950408889634

LANES = 128
BF16_SUBLANES = 16
VMEM_LIMIT = 48 * 2**20
VMEM_LIMIT_BIG = 56 * 2**20

TM = 512
TM_X = 1024
TN = 512
TR = 256
TQ = 512
TQ_WIN = 256
TC = 256
DMA_UNROLL = 8

OFF_QA, OFF_QB, OFF_QC, OFF_QD = 0, 512, 1024, 1792
OFF_KA, OFF_VA, OFF_KB, OFF_VB = 2304, 2816, 3328, 3456
OFF_CKV, OFF_KR, OFF_KD, OFF_VD = 3584, 3840, 3872, 4000
IN_COLS = 4128


def _cparams(sem, vmem=VMEM_LIMIT):
    return pltpu.CompilerParams(dimension_semantics=sem, vmem_limit_bytes=vmem)


def _pick(n, prefs):
    for p in prefs:
        if n % p == 0:
            return p
    return n


def _row_tile(n_rows, s_len, prefs):
    for p in prefs:
        if n_rows % p == 0 and s_len % p == 0:
            return p
    raise ValueError("no row tile fits")


def _rms_modulate(x, g, sc, sh):
    ms = jnp.mean(x * x, axis=-1, keepdims=True)
    return (x * lax.rsqrt(ms + EPS) * g) * (1.0 + sc) + sh


def _mod_kernel(c_ref, w_ref, b_ref, o_ref):
    a = c_ref[...]
    a = (a * jax.nn.sigmoid(a)).astype(BF16)
    o_ref[0] = jnp.dot(a, w_ref[0].astype(BF16), preferred_element_type=F32) + b_ref[0]


def _modulation(cc, w_mod, b_mod):
    depth, d, d6 = w_mod.shape
    rows = cc.shape[0]
    tn = _pick(d6, (1024, 512, 256, 128))
    return pl.pallas_call(
        _mod_kernel,
        out_shape=jax.ShapeDtypeStruct((depth, rows, d6), F32),
        grid=(depth, d6 // tn),
        in_specs=[
            pl.BlockSpec((rows, d), lambda l, n: (0, 0)),
            pl.BlockSpec((1, d, tn), lambda l, n: (l, 0, n)),
            pl.BlockSpec((1, 1, tn), lambda l, n: (l, 0, n)),
        ],
        out_specs=pl.BlockSpec((1, rows, tn), lambda l, n: (l, 0, n)),
        compiler_params=_cparams(("arbitrary", "arbitrary")),
        name="modulation",
    )(cc, w_mod, b_mod.reshape(depth, 1, d6))


def _stacked_specs(parts, block, tile_rows, col_of, n_lead):
    def grid_m(idx):
        return idx[n_lead]

    if len(parts) == 1:
        return [pl.BlockSpec(block, lambda *idx: (grid_m(idx), col_of(idx)))], None
    mt_a = parts[0].shape[0] // tile_rows
    return ([pl.BlockSpec(block, lambda *idx: (jnp.minimum(grid_m(idx), mt_a - 1), col_of(idx))),
             pl.BlockSpec(block, lambda *idx: (jnp.maximum(grid_m(idx) - mt_a, 0), col_of(idx)))], mt_a)


def _nm_mm_kernel(*refs, n_h, mt_a, n_g):
    h_refs = refs[:n_h]
    g_ref, sh_ref, sc_ref = refs[n_h:n_h + 3]
    w_refs, o_ref, u_ref = refs[n_h + 3:n_h + 3 + n_g], refs[n_h + 3 + n_g], refs[n_h + 4 + n_g]
    m = pl.program_id(0)

    def fill(h_ref):
        u_ref[...] = _rms_modulate(h_ref[...], g_ref[...], sc_ref[0], sh_ref[0]).astype(BF16)

    first = pl.program_id(1) == 0
    if n_h == 1:
        pl.when(first)(lambda: fill(h_refs[0]))
    else:
        pl.when(first & (m < mt_a))(lambda: fill(h_refs[0]))
        pl.when(first & (m >= mt_a))(lambda: fill(h_refs[1]))

    u = u_ref[...]
    accs = [jnp.dot(u, w[...].astype(BF16), preferred_element_type=F32) for w in w_refs]
    out = accs[0] if n_g == 1 else (accs[0] * jax.nn.sigmoid(accs[0])) * accs[1]
    o_ref[...] = out.astype(o_ref.dtype)


def _nm_matmul(hs, g, mod, sh_blk, sc_blk, ws, *, n_rows, s_len, mod_row, out_dtype):
    d = hs[0].shape[1]
    n_cols = ws[0].shape[1]
    tm = _row_tile(n_rows, s_len, (TM_X, 512, 256, 128))
    tn = TN
    h_specs, mt_a = _stacked_specs(hs, (tm, d), tm, lambda idx: 0, 0)
    in_specs = h_specs + [
        pl.BlockSpec((1, d), lambda m, n: (0, 0)),
        pl.BlockSpec((1, 1, d), lambda m, n: (mod_row(m * tm), 0, sh_blk)),
        pl.BlockSpec((1, 1, d), lambda m, n: (mod_row(m * tm), 0, sc_blk)),
    ] + [pl.BlockSpec((d, tn), lambda m, n: (0, n)) for _ in ws]
    return pl.pallas_call(
        functools.partial(_nm_mm_kernel, n_h=len(hs), mt_a=mt_a, n_g=len(ws)),
        out_shape=jax.ShapeDtypeStruct((n_rows, n_cols), out_dtype),
        grid=(n_rows // tm, pl.cdiv(n_cols, tn)),
        in_specs=in_specs,
        out_specs=pl.BlockSpec((tm, tn), lambda m, n: (m, n)),
        scratch_shapes=[pltpu.VMEM((tm, d), BF16)],
        compiler_params=_cparams(("arbitrary", "arbitrary"), VMEM_LIMIT_BIG),
        name="normmod_matmul",
    )(*hs, g.reshape(1, d), mod, mod, *ws)


def _final_norm_kernel(x_ref, g_ref, o_ref):
    x = x_ref[...]
    ms = jnp.mean(x * x, axis=-1, keepdims=True)
    o_ref[...] = x * lax.rsqrt(ms + EPS) * g_ref[...]


def _mm_kernel(te_ref, nu_ref, *refs, x_parts, res_parts, mt_a, n_g, epi):
    n_x = len(x_parts)
    refs = list(refs)

    def take(n):
        out = refs[:n]
        del refs[:n]
        return out

    x_refs = [take(p) for p in x_parts]
    w_refs = take(n_x * n_g)
    if epi == "resgate":
        res_refs = take(res_parts)
        gt_ref, = take(1)
    o_ref, = take(1)
    wc_refs = refs
    m = pl.program_id(1)

    def stacked(parts):
        if len(parts) == 1:
            return parts[0][...]
        return jnp.where(m < mt_a, parts[0][...], parts[1][...])

    panel_changed = (m == 0) | (te_ref[m] != te_ref[jnp.maximum(m - 1, 0)])

    @pl.when(panel_changed)
    def _():
        for w, wc in zip(w_refs, wc_refs):
            wc[...] = w[0].astype(BF16)

    @pl.when(m < nu_ref[0])
    def _():
        accs = []
        for g in range(n_g):
            acc = None
            for i in range(n_x):
                part = jnp.dot(stacked(x_refs[i]).astype(BF16), wc_refs[g * n_x + i][...],
                               preferred_element_type=F32)
                acc = part if acc is None else acc + part
            accs.append(acc)
        if epi == "plain":
            out = accs[0]
        elif epi == "swiglu":
            a = accs[0]
            out = (a * jax.nn.sigmoid(a)) * accs[1]
        else:
            out = stacked(res_refs) + gt_ref[0] * accs[0]
        o_ref[...] = out.astype(o_ref.dtype)

    @pl.when(m >= nu_ref[0])
    def _():
        o_ref[...] = jnp.zeros_like(o_ref)


def _matmul(xs, ws, *, n_rows, out_dtype, epi="plain", tile_expert=None, n_used=None,
            res=None, mod=None, gt_blk=None, mod_row=None, tm=TM, tn=TN):
    n_x, n_g = len(xs), len(ws)
    n_cols = ws[0].shape[2]
    tn = _pick(n_cols, (tn, 512, 256, 128))
    tm = _pick(n_rows, (tm, 256, 128))
    mt, nt = n_rows // tm, n_cols // tn
    if tile_expert is None:
        tile_expert = jnp.zeros((mt,), jnp.int32)
        n_used = jnp.full((1,), mt, jnp.int32)
    ks = [x[0].shape[1] for x in xs]
    in_specs, args, mt_a = [], [], None
    for x, k in zip(xs, ks):
        if len(x) == 1:
            in_specs.append(pl.BlockSpec((tm, k), lambda n, m, te, nu: (jnp.minimum(m, nu[0] - 1), 0)))
        else:
            specs, mt_a = _stacked_specs(x, (tm, k), tm, lambda idx: 0, 1)
            in_specs += specs
        args += list(x)
    for g in range(n_g):
        for i, k in enumerate(ks):
            in_specs.append(pl.BlockSpec((1, k, tn), lambda n, m, te, nu, i=i: (te[m], i, n)))
            args.append(ws[g])
    if epi == "resgate":
        blk0 = gt_blk * (n_cols // tn)
        specs, mt_res = _stacked_specs(res, (tm, tn), tm, lambda idx: idx[0], 1)
        mt_a = mt_res if mt_res is not None else mt_a
        in_specs += specs
        in_specs.append(pl.BlockSpec((1, 1, tn), lambda n, m, te, nu: (mod_row(m * tm), 0, blk0 + n)))
        args += list(res) + [mod]
    scratch = [pltpu.VMEM((k, tn), BF16) for _ in range(n_g) for k in ks]
    return pl.pallas_call(
        functools.partial(_mm_kernel, x_parts=tuple(len(x) for x in xs),
                          res_parts=len(res) if res is not None else 0, mt_a=mt_a, n_g=n_g, epi=epi),
        out_shape=jax.ShapeDtypeStruct((n_rows, n_cols), out_dtype),
        grid_spec=pltpu.PrefetchScalarGridSpec(
            num_scalar_prefetch=2,
            grid=(nt, mt),
            in_specs=in_specs,
            out_specs=pl.BlockSpec((tm, tn), lambda n, m, te, nu: (m, n)),
            scratch_shapes=scratch,
        ),
        compiler_params=_cparams(("arbitrary", "arbitrary")),
        name="matmul_" + epi,
    )(tile_expert, n_used, *args)


def _rope(x, cos, sin_signed, half):
    lane = lax.broadcasted_iota(jnp.int32, (x.shape[0], LANES), 1)
    first = (lane & (2 * half - 1)) < half
    blocks = []
    for j in range(x.shape[1] // LANES):
        xj = x[:, j * LANES:(j + 1) * LANES]
        blocks.append(jnp.where(first, pltpu.roll(xj, LANES - half, 1), pltpu.roll(xj, half, 1)))
    partner = blocks[0] if len(blocks) == 1 else jnp.concatenate(blocks, axis=-1)
    return x * cos + partner * sin_signed


def _group_mean_sq(x, g_ref):
    x2 = x * x
    hi = x2.astype(BF16)
    lo = (x2 - hi.astype(F32)).astype(BF16)
    g = g_ref[...]
    return (jnp.dot(hi, g, preferred_element_type=F32) + jnp.dot(lo, g, preferred_element_type=F32))


def _dup_halves(x):
    lane = lax.broadcasted_iota(jnp.int32, x.shape, 1)
    lo = lane < HEAD_DIM
    swapped = pltpu.roll(x, HEAD_DIM, 1)
    return jnp.concatenate([jnp.where(lo, x, swapped), jnp.where(lo, swapped, x)], axis=-1)


def _store_transposed(dst_ref, x):
    for j in range(x.shape[1] // LANES):
        dst_ref[0, j * LANES:(j + 1) * LANES, :] = x[:, j * LANES:(j + 1) * LANES].T.astype(BF16)


def _proj_prep_kernel(*refs, n_h, mt_a):
    refs = list(refs)
    h_refs = [refs.pop(0) for _ in range(n_h)]
    (g_ref, sh_ref, sc_ref, w_ref, c64_ref, s64_ref, cq_ref, sq_ref, ck_ref, sk_ref,
     bqn_ref, bkn_ref, cqn_ref, ckvn_ref, g512_ref, g128_ref, wq_ref, wkk_ref, wkv_ref, place_ref,
     qa_ref, qb_ref, qc_ref, qd_ref,
     ka_ref, kb_ref, kc_ref, kd_ref, vd_ref, vta_ref, vtb_ref, vtc_ref, u_ref) = refs

    def fill(h_ref):
        u_ref[...] = _rms_modulate(h_ref[...], g_ref[...], sc_ref[0], sh_ref[0]).astype(BF16)

    if n_h == 1:
        fill(h_refs[0])
    else:
        t = pl.program_id(0)
        pl.when(t < mt_a)(lambda: fill(h_refs[0]))
        pl.when(t >= mt_a)(lambda: fill(h_refs[1]))

    def proj(off, width):
        return jnp.dot(u_ref[...], w_ref[:, off:off + width], preferred_element_type=F32)

    c64, s64 = c64_ref[...], s64_ref[...]
    c128, s128 = c64[:, :LANES], s64[:, :LANES]
    scale64 = HEAD_DIM ** -0.5
    scale_mla = (C_NOPE_DIM + C_ROPE_DIM) ** -0.5

    qa_ref[...] = (_rope(proj(OFF_QA, 512), c64, s64, 16) * (scale64 * LOG2E)).astype(BF16)
    ka_ref[0] = _rope(proj(OFF_KA, 512), c64, s64, 16).astype(BF16)
    _store_transposed(vta_ref, proj(OFF_VA, 512))

    xb = proj(OFF_QB, 512)
    yb = xb * lax.rsqrt(_group_mean_sq(xb, g512_ref) + EPS) * bqn_ref[...]
    qb_ref[...] = (_rope(yb, c64, s64, 16) * (scale64 * LOG2E)).astype(BF16)
    kvb = proj(OFF_KB, 256)
    xk = kvb[:, :LANES]
    yk = xk * lax.rsqrt(_group_mean_sq(xk, g128_ref) + EPS) * bkn_ref[...]
    kb_ref[0] = _dup_halves(_rope(yk, c128, s128, 16)).astype(BF16)
    _store_transposed(vtb_ref, kvb[:, LANES:])

    xq = proj(OFF_QC, C_Q_RANK)
    yq = xq * lax.rsqrt(jnp.mean(xq * xq, axis=-1, keepdims=True) + EPS) * cqn_ref[...]
    qf = jnp.dot(yq.astype(BF16), wq_ref[...], preferred_element_type=F32)
    qc_ref[...] = (_rope(qf, cq_ref[...], sq_ref[...], 8) * (scale_mla * LOG2E)).astype(BF16)
    xc = proj(OFF_CKV, C_KV_RANK)
    yc = (xc * lax.rsqrt(jnp.mean(xc * xc, axis=-1, keepdims=True) + EPS) * ckvn_ref[...]).astype(BF16)
    tail = proj(OFF_KR, IN_COLS - OFF_KR)
    kr = _rope(tail[:, :LANES], ck_ref[...], sk_ref[...], 8).astype(BF16)
    kc = (jnp.dot(yc, wkk_ref[...], preferred_element_type=F32)
          + jnp.dot(kr, place_ref[...], preferred_element_type=F32))
    kc_ref[0] = kc.astype(BF16)
    _store_transposed(vtc_ref, jnp.dot(yc, wkv_ref[...], preferred_element_type=F32))

    qd_ref[...] = (_rope(proj(OFF_QD, 512), c64, s64, 16) * scale64).astype(BF16)
    kd0, vd0 = OFF_KD - OFF_KR, OFF_VD - OFF_KR
    kd_ref[0] = _dup_halves(_rope(tail[:, kd0:kd0 + LANES], c128, s128, 16)).astype(BF16)
    vd_ref[0] = _dup_halves(tail[:, vd0:vd0 + LANES]).astype(BF16)


def _proj_prep(hs, g, mod, w_in_bf16, tables, consts, dims, mod_row):
    bsz, s_len, n_ctx = dims
    d = hs[0].shape[1]
    n_rows = sum(a.shape[0] for a in hs)
    sk = n_ctx + s_len
    tr = _pick(math.gcd(s_len, n_ctx), (TR, 128))
    n_lat_t, lat_pb, ctx_pb = bsz * s_len // tr, s_len // tr, n_ctx // tr

    def is_lat(t):
        return t < n_lat_t

    def tbl_idx(t):
        return jnp.where(is_lat(t), t % lat_pb, lat_pb)

    def kv_b(t):
        return jnp.where(is_lat(t), t // lat_pb, (t - n_lat_t) // ctx_pb)

    def kv_j(t):
        return jnp.where(is_lat(t), ctx_pb + t % lat_pb, (t - n_lat_t) % ctx_pb)

    def row_spec(w):
        return pl.BlockSpec((tr, w), lambda t: (t, 0))

    def tbl_spec(w):
        return pl.BlockSpec((tr, w), lambda t: (tbl_idx(t), 0))

    def const_spec(a):
        return pl.BlockSpec(a.shape, lambda t: (0,) * a.ndim)

    def kv_spec(w):
        return pl.BlockSpec((1, tr, w), lambda t: (kv_b(t), kv_j(t), 0))

    def vt_spec(w):
        return pl.BlockSpec((1, w, tr), lambda t: (kv_b(t), 0, kv_j(t)))

    q_widths = (512, 512, 1024, 512)
    kv_widths = (512, 256, 1024, 256, 256)
    vt_widths = (512, 128, 512)
    out_shape = ([jax.ShapeDtypeStruct((n_rows, w), BF16) for w in q_widths]
                 + [jax.ShapeDtypeStruct((bsz, sk, w), BF16) for w in kv_widths]
                 + [jax.ShapeDtypeStruct((bsz, w, sk), BF16) for w in vt_widths])
    out_specs = ([row_spec(w) for w in q_widths] + [kv_spec(w) for w in kv_widths]
                 + [vt_spec(w) for w in vt_widths])
    h_specs, mt_a = _stacked_specs(hs, (tr, d), tr, lambda idx: 0, 0)
    in_specs = (h_specs
                + [pl.BlockSpec((1, d), lambda t: (0, 0)),
                   pl.BlockSpec((1, 1, d), lambda t: (mod_row(t * tr), 0, 0)),
                   pl.BlockSpec((1, 1, d), lambda t: (mod_row(t * tr), 0, 1)),
                   pl.BlockSpec(w_in_bf16.shape, lambda t: (0, 0), pipeline_mode=pl.Buffered(1))]
                + [tbl_spec(t.shape[1]) for t in tables] + [const_spec(a) for a in consts])
    return pl.pallas_call(
        functools.partial(_proj_prep_kernel, n_h=len(hs), mt_a=mt_a),
        out_shape=out_shape,
        grid=(n_rows // tr,),
        in_specs=in_specs,
        out_specs=out_specs,
        scratch_shapes=[pltpu.VMEM((tr, d), BF16)],
        compiler_params=_cparams(("arbitrary",), VMEM_LIMIT_BIG),
        name="proj_prep",
    )(*hs, g.reshape(1, d), mod, mod, w_in_bf16, *tables, *consts)


def _scores(a, b):
    return lax.dot_general(a, b, (((1,), (1,)), ((), ())), preferred_element_type=F32)


def _half_masks(shape):
    lane = lax.broadcasted_iota(jnp.int32, shape, 1)
    lo = lane < HEAD_DIM
    return lo, jnp.logical_not(lo)


def _softmax_pv_t(s, v_t):
    m = jnp.max(s, axis=0, keepdims=True)
    e = jnp.exp2(s - m).astype(BF16)
    dv = v_t.shape[0]
    v_ext = jnp.concatenate([v_t, jnp.ones((BF16_SUBLANES, v_t.shape[1]), BF16)], axis=0)
    oe = jnp.dot(v_ext, e, preferred_element_type=F32)
    return oe[:dv] / oe[dv:dv + 1]


def _run_units(score_fns, finish_fns):
    s = score_fns[0]()
    for i, finish in enumerate(finish_fns):
        s_next = score_fns[i + 1]() if i + 1 < len(score_fns) else None
        finish(s)
        s = s_next


def _gqa_t_kernel(q_ref, k_ref, vt_ref, o_ref):
    masks = _half_masks((q_ref.shape[0], LANES))
    n_pairs = q_ref.shape[1] // LANES
    pairs_per_group = B_HEADS // B_KV_HEADS // 2
    outs = {}

    def score(j, z):
        g = j // pairs_per_group
        q = q_ref[:, j * LANES:(j + 1) * LANES]
        return lambda: _scores(k_ref[0, :, g * LANES:(g + 1) * LANES],
                               jnp.where(masks[z], q, jnp.zeros_like(q)))

    def finish(j, z):
        g = j // pairs_per_group

        def fin(s):
            outs[z] = _softmax_pv_t(s, vt_ref[0, g * HEAD_DIM:(g + 1) * HEAD_DIM, :])
            if z == 1:
                pair = jnp.concatenate([outs[0], outs[1]], axis=0)
                o_ref[:, j * LANES:(j + 1) * LANES] = pair.T.astype(o_ref.dtype)
        return fin

    units = [(j, z) for j in range(n_pairs) for z in range(2)]
    _run_units([score(j, z) for j, z in units], [finish(j, z) for j, z in units])


def _mla_t_kernel(q_ref, k_ref, vt_ref, o_ref):
    n_heads = q_ref.shape[1] // LANES
    outs = {}

    def score(u):
        return lambda: _scores(k_ref[0, :, u * LANES:(u + 1) * LANES], q_ref[:, u * LANES:(u + 1) * LANES])

    def finish(u):
        def fin(s):
            outs[u % 2] = _softmax_pv_t(s, vt_ref[0, u * C_V_DIM:(u + 1) * C_V_DIM, :])
            if u % 2 == 1:
                pair = jnp.concatenate([outs[0], outs[1]], axis=0)
                j = u // 2
                o_ref[:, j * LANES:(j + 1) * LANES] = pair.T.astype(o_ref.dtype)
        return fin

    _run_units([score(u) for u in range(n_heads)], [finish(u) for u in range(n_heads)])


def _diff_t_kernel(lam_ref, subln_ref, q0_ref, q1_ref, k0_ref, k1_ref, vt_ref, o_ref, *, lam_init):
    t = lam_ref[...]
    lam = (jnp.exp(jnp.sum(t[0:1] * t[1:2], axis=-1, keepdims=True))
           - jnp.exp(jnp.sum(t[2:3] * t[3:4], axis=-1, keepdims=True)) + lam_init)
    q_refs, k_refs = (q0_ref, q1_ref), (k0_ref, k1_ref)
    masks = _half_masks((q0_ref.shape[0], LANES))
    first_map = {}

    def score(z, mp):
        j = z // 2
        q = q_refs[mp][:, j * LANES:(j + 1) * LANES]
        return lambda: _scores(k_refs[mp][0, :, j * LANES:(j + 1) * LANES],
                               jnp.where(masks[z % 2], q, jnp.zeros_like(q)))

    def finish(z, mp):
        def fin(s):
            a = _softmax_pv_t(s, vt_ref[0, z * A_V_DIM:(z + 1) * A_V_DIM, :])
            if mp == 0:
                first_map[z] = a
                return
            d = (first_map[z] - lam * a).T
            y = d * lax.rsqrt(jnp.mean(d * d, axis=-1, keepdims=True) + EPS) * subln_ref[...]
            o_ref[:, z * A_V_DIM:(z + 1) * A_V_DIM] = (y * (1.0 - lam_init)).astype(o_ref.dtype)
        return fin

    units = [(z, mp) for z in range(A_HEADS) for mp in range(2)]
    _run_units([score(z, mp) for z, mp in units], [finish(z, mp) for z, mp in units])


def _attend_sink(q, k, v, sink):
    s = _scores(q, k)
    m = jnp.maximum(jnp.max(s, axis=-1, keepdims=True), sink)
    e = jnp.exp(s - m)
    l = jnp.sum(e, axis=-1, keepdims=True) + jnp.exp(sink - m)
    return jnp.dot(e.astype(BF16), v, preferred_element_type=F32) / l


def _window_kernel(sink_ref, q_ref, k_ref, v_ref, o_ref, *, s_len, n_ctx, n_q_lat, with_ctx):
    tq = q_ref.shape[0]
    band = tq + 2 * WINDOW
    group = pl.program_id(1)
    i = pl.program_id(2)
    heads_per_group = D_HEADS // D_KV_HEADS

    def body(is_lat):
        k_c, v_c = k_ref[0, :n_ctx, :], v_ref[0, :n_ctx, :]
        masks = _half_masks((tq, LANES))
        if is_lat:
            start = pl.multiple_of(jnp.clip(i * tq - WINDOW, 0, s_len - band), LANES)
            row0 = pl.multiple_of(n_ctx + start, LANES)
            k_b = k_ref[0, pl.ds(row0, band), :]
            v_b = v_ref[0, pl.ds(row0, band), :]
            qpos = i * tq + lax.broadcasted_iota(jnp.int32, (tq, band), 0)
            kpos = start + lax.broadcasted_iota(jnp.int32, (tq, band), 1)
            in_band = jnp.abs(qpos - kpos) <= WINDOW
        for j in range(q_ref.shape[1] // LANES):
            q = q_ref[:, j * LANES:(j + 1) * LANES]
            outs = []
            for z in range(2):
                qz = jnp.where(masks[z], q, jnp.zeros_like(q))
                sink = sink_ref[group * heads_per_group + 2 * j + z]
                if not is_lat:
                    outs.append(_attend_sink(qz, k_c, v_c, sink))
                    continue
                s_c = _scores(qz, k_c)
                s_b = jnp.where(in_band, _scores(qz, k_b), -jnp.inf)
                m = jnp.maximum(jnp.maximum(jnp.max(s_c, axis=-1, keepdims=True),
                                            jnp.max(s_b, axis=-1, keepdims=True)), sink)
                e_c, e_b = jnp.exp(s_c - m), jnp.exp(s_b - m)
                l = (jnp.sum(e_c, axis=-1, keepdims=True) + jnp.sum(e_b, axis=-1, keepdims=True)
                     + jnp.exp(sink - m))
                o = (jnp.dot(e_c.astype(BF16), v_c, preferred_element_type=F32)
                     + jnp.dot(e_b.astype(BF16), v_b, preferred_element_type=F32))
                outs.append(o / l)
            o_ref[:, j * LANES:(j + 1) * LANES] = jnp.where(masks[0], outs[0], outs[1]).astype(o_ref.dtype)

    if not with_ctx:
        body(True)
        return

    @pl.when(i < n_q_lat)
    def _():
        body(True)

    @pl.when(i >= n_q_lat)
    def _():
        body(False)


def _mixers(ops, dims, params, lam_init, with_ctx):
    bsz, s_len, n_ctx = dims
    qa, qb, qc, qd, ka, kb, kc, kd, vd, vta, vtb, vtc = ops
    lam_vecs, subln, sink = params
    n_lat = bsz * s_len
    sk_all = n_ctx + s_len

    def const_spec(a):
        return pl.BlockSpec(a.shape, lambda b, i: (0,) * a.ndim)

    def dense_set(tq, nk, row0_tiles, n_q, out_rows):
        def q_row(b, i):
            return row0_tiles + b * n_q + i

        def qspec(w, col=0):
            return pl.BlockSpec((tq, w), lambda b, i: (q_row(b, i), col))

        def kspec(w, col=0):
            return pl.BlockSpec((1, nk, w), lambda b, i: (b, 0, col))

        def vtspec(w):
            return pl.BlockSpec((1, w, nk), lambda b, i: (b, 0, 0))

        def call(kernel, in_specs, args, name):
            return pl.pallas_call(
                kernel,
                out_shape=jax.ShapeDtypeStruct((out_rows, 512), BF16),
                grid=(bsz, n_q),
                in_specs=in_specs,
                out_specs=pl.BlockSpec((tq, 512), lambda b, i: (b * n_q + i, 0)),
                compiler_params=_cparams(("arbitrary", "arbitrary"), VMEM_LIMIT_BIG),
                name=name,
            )(*args)

        oa = call(functools.partial(_diff_t_kernel, lam_init=lam_init),
                  [const_spec(lam_vecs), const_spec(subln), qspec(256, 0), qspec(256, 1),
                   kspec(256, 0), kspec(256, 1), vtspec(512)],
                  [lam_vecs, subln, qa, qa, ka, ka, vta], "attn_diff")
        ob = call(_gqa_t_kernel, [qspec(512), kspec(256), vtspec(128)], [qb, kb, vtb], "attn_qknorm")
        oc = call(_mla_t_kernel, [qspec(1024), kspec(1024), vtspec(512)], [qc, kc, vtc], "attn_mla")
        return [oa, ob, oc]

    tq = _pick(s_len, (TQ, 256, 128))
    dense = [[o] for o in dense_set(tq, sk_all, 0, s_len // tq, n_lat)]
    if with_ctx:
        tq_c = _pick(n_ctx, (TQ, 256, 128))
        ctx = dense_set(tq_c, n_ctx, n_lat // tq_c, n_ctx // tq_c, bsz * n_ctx)
        dense = [a + [b] for a, b in zip(dense, ctx)]

    tw = _pick(math.gcd(s_len, n_ctx), (TQ_WIN, 128))
    n_q_lat, n_q_ctx = s_len // tw, n_ctx // tw
    n_q = n_q_lat + n_q_ctx if with_ctx else n_q_lat
    out_rows = n_lat + bsz * n_ctx if with_ctx else n_lat

    def w_row(b, i):
        return jnp.where(i < n_q_lat, b * n_q_lat + i, n_lat // tw + b * n_q_ctx + (i - n_q_lat))

    od = pl.pallas_call(
        functools.partial(_window_kernel, s_len=s_len, n_ctx=n_ctx, n_q_lat=n_q_lat, with_ctx=with_ctx),
        out_shape=jax.ShapeDtypeStruct((out_rows, 512), BF16),
        grid=(bsz, D_KV_HEADS, n_q),
        in_specs=[pl.BlockSpec(memory_space=pltpu.SMEM),
                  pl.BlockSpec((tw, 256), lambda b, g, i: (w_row(b, i), g)),
                  pl.BlockSpec((1, sk_all, 128), lambda b, g, i: (b, 0, g)),
                  pl.BlockSpec((1, sk_all, 128), lambda b, g, i: (b, 0, g))],
        out_specs=pl.BlockSpec((tw, 256), lambda b, g, i: (w_row(b, i), g)),
        compiler_params=_cparams(("arbitrary", "arbitrary", "arbitrary")),
        name="attn_window",
    )(sink, qd, kd, vd)
    return dense + [[od]]


def _route_kernel(h_ref, g_ref, sh_ref, sc_ref, w_ref, b_ref, u_ref, sel_ref, idx_ref, gw_ref):
    u = _rms_modulate(h_ref[...], g_ref[...], sc_ref[0], sh_ref[0])
    u_ref[...] = u
    w = w_ref[...]
    u_hi = u.astype(BF16)
    u_lo = (u - u_hi.astype(F32)).astype(BF16)
    w_hi = w.astype(BF16)
    w_lo = (w - w_hi.astype(F32)).astype(BF16)
    logits = (_scores(w_hi, u_hi) + _scores(w_hi, u_lo) + _scores(w_lo, u_hi) + _scores(w_lo, u_lo)
              + b_ref[...])
    ids = lax.broadcasted_iota(jnp.int32, logits.shape, 0).astype(F32)
    m1 = jnp.max(logits, axis=0, keepdims=True)
    i1 = jnp.min(jnp.where(logits == m1, ids, float(N_EXPERTS)), axis=0, keepdims=True)
    first = ids == i1
    rest = jnp.where(first, -jnp.inf, logits)
    m2 = jnp.max(rest, axis=0, keepdims=True)
    i2 = jnp.min(jnp.where(rest == m2, ids, float(N_EXPERTS)), axis=0, keepdims=True)
    second = ids == i2
    e = jnp.exp(m2 - m1)
    w1 = 1.0 / (1.0 + e)
    w2 = e / (1.0 + e)
    sel_ref[...] = jnp.where(first | second, 1, 0).astype(jnp.int32)
    idx_ref[...] = jnp.where(ids == 0.0, i1, jnp.where(ids == 1.0, i2, 0.0)).astype(jnp.int32)
    gw_ref[...] = jnp.where(ids == 0.0, w1, jnp.where(ids == 1.0, w2, 0.0))


def _route(h, g, mod, sh_blk, sc_blk, w_router_t, b_router, n_tok, s_len, mod_row):
    d = h.shape[1]
    tm = _row_tile(n_tok, s_len, (TM, 256, 128))
    outs = ([jax.ShapeDtypeStruct((n_tok, d), F32)]
            + [jax.ShapeDtypeStruct((N_EXPERTS, n_tok), dt) for dt in (jnp.int32, jnp.int32, F32)])
    return pl.pallas_call(
        _route_kernel,
        out_shape=outs,
        grid=(n_tok // tm,),
        in_specs=[pl.BlockSpec((tm, d), lambda m: (m, 0)),
                  pl.BlockSpec((1, d), lambda m: (0, 0)),
                  pl.BlockSpec((1, 1, d), lambda m: (mod_row(m * tm), 0, sh_blk)),
                  pl.BlockSpec((1, 1, d), lambda m: (mod_row(m * tm), 0, sc_blk)),
                  pl.BlockSpec((N_EXPERTS, d), lambda m: (0, 0)),
                  pl.BlockSpec((N_EXPERTS, 1), lambda m: (0, 0))],
        out_specs=[pl.BlockSpec((tm, d), lambda m: (m, 0))]
        + [pl.BlockSpec((N_EXPERTS, tm), lambda m: (0, m))] * 3,
        compiler_params=_cparams(("arbitrary",)),
        name="moe_route",
    )(h, g.reshape(1, d), mod, mod, w_router_t, b_router.reshape(N_EXPERTS, 1))


def _row_copy(src, dst, sem, src_row, dst_row):
    return pltpu.make_async_copy(src.at[pl.ds(src_row, 1)], dst.at[pl.ds(dst_row, 1)], sem)


def _gather_kernel(tok_ref, nxt_ref, src_ref, o_ref, buf, sem):
    i = pl.program_id(0)
    n = pl.num_programs(0)
    tg = buf.shape[1]
    slot = i % 2

    def issue(t_ref, s):
        def body(g, carry):
            for k in range(DMA_UNROLL):
                r = g * DMA_UNROLL + k
                _row_copy(src_ref, buf.at[s], sem.at[s], t_ref[0, 0, r], r).start(priority=k % 2)
            return carry
        lax.fori_loop(0, tg // DMA_UNROLL, body, 0)

    @pl.when(i == 0)
    def _():
        issue(tok_ref, 0)

    @pl.when(i + 1 < n)
    def _():
        issue(nxt_ref, 1 - slot)

    def wait(r, carry):
        _row_copy(src_ref, buf.at[slot], sem.at[slot], 0, r).wait()
        return carry

    lax.fori_loop(0, tg, wait, 0, unroll=DMA_UNROLL)
    o_ref[...] = buf[slot].astype(o_ref.dtype)


def _gather_rows(src, tok, tg):
    n_rows = tok.shape[0]
    d = src.shape[1]
    n_t = n_rows // tg
    tok = tok.reshape(n_t, 1, tg)
    return pl.pallas_call(
        _gather_kernel,
        out_shape=jax.ShapeDtypeStruct((n_rows, d), BF16),
        grid=(n_t,),
        in_specs=[pl.BlockSpec((1, 1, tg), lambda i: (i, 0, 0), memory_space=pltpu.SMEM),
                  pl.BlockSpec((1, 1, tg), lambda i: (jnp.minimum(i + 1, n_t - 1), 0, 0),
                               memory_space=pltpu.SMEM),
                  pl.BlockSpec(memory_space=pl.ANY)],
        out_specs=pl.BlockSpec((tg, d), lambda i: (i, 0)),
        scratch_shapes=[pltpu.VMEM((2, tg, d), F32), pltpu.SemaphoreType.DMA((2,))],
        compiler_params=_cparams(("arbitrary",)),
        name="moe_gather",
    )(tok, tok, src)


def _combine_kernel(pos_ref, nxt_ref, y_ref, h_ref, gw_ref, gt_ref, gf_ref, o_ref, buf, sem):
    i = pl.program_id(0)
    n = pl.num_programs(0)
    tc = buf.shape[2]
    slot = i % 2

    def issue(p_ref, s):
        def body(g, carry):
            for k in range(DMA_UNROLL):
                r = g * DMA_UNROLL + k
                _row_copy(y_ref, buf.at[s, 0], sem.at[s], p_ref[0, 0, r], r).start(priority=0)
                _row_copy(y_ref, buf.at[s, 1], sem.at[s], p_ref[0, 1, r], r).start(priority=1)
            return carry
        lax.fori_loop(0, tc // DMA_UNROLL, body, 0)

    @pl.when(i == 0)
    def _():
        issue(pos_ref, 0)

    @pl.when(i + 1 < n)
    def _():
        issue(nxt_ref, 1 - slot)

    def wait(r, carry):
        _row_copy(y_ref, buf.at[slot, 0], sem.at[slot], 0, r).wait()
        _row_copy(y_ref, buf.at[slot, 1], sem.at[slot], 0, r).wait()
        return carry

    lax.fori_loop(0, tc, wait, 0, unroll=DMA_UNROLL)
    gw = gw_ref[...]
    moe = gw[:, 0:1] * buf[slot, 0] + gw[:, 1:2] * buf[slot, 1]
    x = h_ref[...] + gt_ref[0] * moe
    ms = jnp.mean(x * x, axis=-1, keepdims=True)
    o_ref[...] = x * lax.rsqrt(ms + EPS) * gf_ref[...]


def _combine(y, h, pos, gw_t, mod, gt_blk, mod_row, g_final, n_tok):
    d = h.shape[1]
    n_t, _, tc = pos.shape
    return pl.pallas_call(
        _combine_kernel,
        out_shape=jax.ShapeDtypeStruct((n_tok, d), F32),
        grid=(n_t,),
        in_specs=[pl.BlockSpec((1, 2, tc), lambda i: (i, 0, 0), memory_space=pltpu.SMEM),
                  pl.BlockSpec((1, 2, tc), lambda i: (jnp.minimum(i + 1, n_t - 1), 0, 0),
                               memory_space=pltpu.SMEM),
                  pl.BlockSpec(memory_space=pl.ANY),
                  pl.BlockSpec((tc, d), lambda i: (i, 0)),
                  pl.BlockSpec((tc, N_EXPERTS), lambda i: (i, 0)),
                  pl.BlockSpec((1, 1, d), lambda i: (mod_row(i * tc), 0, gt_blk)),
                  pl.BlockSpec((1, d), lambda i: (0, 0))],
        out_specs=pl.BlockSpec((tc, d), lambda i: (i, 0)),
        scratch_shapes=[pltpu.VMEM((2, 2, tc, d), F32), pltpu.SemaphoreType.DMA((2,))],
        compiler_params=_cparams(("arbitrary",)),
        name="moe_combine",
    )(pos, pos, y, h, gw_t, mod, g_final.reshape(1, d))


def _dispatch_plan(sel, idx, tm, n_slots, tc):
    n_tok = sel.shape[1]
    n_tiles = n_slots // tm
    counts = jnp.sum(sel, axis=1)
    padded = ((counts + tm - 1) // tm) * tm
    ends = jnp.cumsum(padded)
    offs = ends - padded
    pos = offs[:, None] + jnp.cumsum(sel, axis=1) - sel
    pos0 = jnp.take_along_axis(pos, idx[0:1], axis=0)[0]
    pos1 = jnp.take_along_axis(pos, idx[1:2], axis=0)[0]
    pos_tiles = jnp.stack([pos0.reshape(n_tok // tc, tc), pos1.reshape(n_tok // tc, tc)], axis=1)
    tile_start = jnp.arange(n_tiles, dtype=jnp.int32) * tm
    tile_expert = jnp.minimum(jnp.sum(ends[None, :] <= tile_start[:, None], axis=1), N_EXPERTS - 1)
    n_used = ends[-1] // tm
    tok = jnp.arange(n_tok, dtype=jnp.int32)
    tok_of_row = jnp.zeros((n_slots,), jnp.int32).at[jnp.concatenate([pos0, pos1])].set(
        jnp.concatenate([tok, tok]), unique_indices=True)
    return (pos_tiles.astype(jnp.int32), tok_of_row, tile_expert.astype(jnp.int32),
            n_used.reshape(1).astype(jnp.int32))


def _rope_tables(s_len, pad_rows):
    t = jnp.arange(s_len)
    rows, cols = (t // GRID_W).astype(F32), (t % GRID_W).astype(F32)

    def axis_tables(rot_dim):
        axis_dim = rot_dim // 2
        inv = ROPE_THETA ** (-jnp.arange(0, axis_dim, 2, dtype=F32) / axis_dim)
        ar, ac = rows[:, None] * inv[None, :], cols[:, None] * inv[None, :]
        cos = jnp.concatenate([jnp.cos(ar), jnp.cos(ar), jnp.cos(ac), jnp.cos(ac)], axis=1)
        sin = jnp.concatenate([-jnp.sin(ar), jnp.sin(ar), -jnp.sin(ac), jnp.sin(ac)], axis=1)
        return cos, sin

    def with_identity(cos, sin):
        w = cos.shape[1]
        return (jnp.concatenate([cos, jnp.ones((pad_rows, w), F32)], axis=0),
                jnp.concatenate([sin, jnp.zeros((pad_rows, w), F32)], axis=0))

    c64, s64 = axis_tables(HEAD_DIM)
    c64, s64 = jnp.tile(c64, (1, 8)), jnp.tile(s64, (1, 8))
    c32, s32 = axis_tables(C_ROPE_DIM)
    ones, zeros = jnp.ones((s_len, 1), F32), jnp.zeros((s_len, 1), F32)
    cq = jnp.tile(jnp.concatenate([jnp.tile(ones, (1, 64)), c32, jnp.tile(ones, (1, 32))], axis=1), (1, 8))
    sq = jnp.tile(jnp.concatenate([jnp.tile(zeros, (1, 64)), s32, jnp.tile(zeros, (1, 32))], axis=1), (1, 8))
    ck = jnp.concatenate([c32, jnp.tile(ones, (1, 96))], axis=1)
    sk = jnp.concatenate([s32, jnp.tile(zeros, (1, 96))], axis=1)
    out = []
    for c, s in ((c64, s64), (cq, sq), (ck, sk)):
        out += list(with_identity(c, s))
    return out


def _mla_weights(w_q_up, w_kv_up):
    qd = C_NOPE_DIM + C_ROPE_DIM
    wq = jnp.pad(w_q_up.reshape(C_Q_RANK, C_HEADS, qd), ((0, 0), (0, 0), (0, LANES - qd)))
    wkv = w_kv_up.reshape(C_KV_RANK, C_HEADS, C_NOPE_DIM + C_V_DIM)
    wkk = jnp.pad(wkv[:, :, :C_NOPE_DIM], ((0, 0), (0, 0), (0, LANES - C_NOPE_DIM)))
    wkv_v = wkv[:, :, C_NOPE_DIM:]
    return (wq.reshape(C_Q_RANK, C_HEADS * LANES).astype(BF16),
            wkk.reshape(C_KV_RANK, C_HEADS * LANES).astype(BF16),
            wkv_v.reshape(C_KV_RANK, C_HEADS * C_V_DIM).astype(BF16))


def _static_mats():
    g = (np.arange(512)[:, None] // HEAD_DIM == np.arange(512)[None, :] // HEAD_DIM) / HEAD_DIM
    place = np.zeros((LANES, C_HEADS * LANES), np.float32)
    for h in range(C_HEADS):
        place[np.arange(C_ROPE_DIM), h * LANES + C_NOPE_DIM + np.arange(C_ROPE_DIM)] = 1.0
    return (jnp.asarray(g, BF16), jnp.asarray(g[:128, :128], BF16), jnp.asarray(place, BF16))


def kernel(x, c, ctx, c_ctx, w_mod, b_mod, g_mix, g_ffn, g_final, w_in, w_out, a_lam_q1, a_lam_k1, a_lam_q2, a_lam_k2, a_subln, b_q_norm, b_k_norm, c_q_norm, c_kv_norm, c_w_q_up, c_w_kv_up, d_sink, ffn_w_gate, ffn_w_up, ffn_w_down, moe_w_router, moe_b_router, moe_w_gate, moe_w_up, moe_w_down):
    bsz, s_len, d = x.shape
    n_ctx = ctx.shape[1]
    depth = w_mod.shape[0]
    n_lat = bsz * s_len
    n_all = n_lat + bsz * n_ctx
    dims = (bsz, s_len, n_ctx)

    def mod_row(row0):
        return jnp.where(row0 < n_lat, row0 // s_len, bsz)

    mod_rows = -(-(bsz + 1) // 8) * 8
    cc = jnp.zeros((mod_rows, d), F32).at[:bsz].set(c).at[bsz].set(c_ctx)
    mods = _modulation(cc, w_mod, b_mod)

    tr = _pick(math.gcd(s_len, n_ctx), (TR, 128))
    tables = _rope_tables(s_len, tr)
    g512, g128, place = _static_mats()

    h = [x.reshape(n_lat, d), ctx.reshape(bsz * n_ctx, d)]
    for l in range(depth):
        last = l == depth - 1
        lam_init = 0.8 - 0.6 * math.exp(-0.3 * l)
        n_rows = n_lat if last else n_all
        mod = mods[l].reshape(mod_rows, 1, 6 * d)
        nm = dict(s_len=s_len, mod_row=mod_row)

        wq, wkk, wkv_v = _mla_weights(c_w_q_up[l], c_w_kv_up[l])
        consts = [jnp.tile(b_q_norm[l], 8)[None], jnp.tile(b_k_norm[l], 2)[None],
                  c_q_norm[l][None], c_kv_norm[l][None], g512, g128, wq, wkk, wkv_v, place]
        ops = _proj_prep(h, g_mix[l], mod, w_in[l].astype(BF16), tables, consts, dims, mod_row)
        lam_vecs = jnp.stack([a_lam_q1[l], a_lam_k1[l], a_lam_q2[l], a_lam_k2[l]])
        mix = _mixers(ops, dims, (lam_vecs, a_subln[l][None], d_sink[l]), lam_init, not last)
        h = [_matmul(mix, [w_out[l][None]], n_rows=n_rows, out_dtype=F32, epi="resgate",
                     res=h, mod=mod, gt_blk=2, mod_row=mod_row, tn=1024)]

        i = l // 2
        if l % 2 == 0:
            mid = _nm_matmul(h, g_ffn[l], mod, 3, 4, [ffn_w_gate[i], ffn_w_up[i]], n_rows=n_rows,
                             out_dtype=BF16, **nm)
            h = [_matmul([[mid]], [ffn_w_down[i][None]], n_rows=n_rows, out_dtype=F32, epi="resgate",
                         res=h, mod=mod, gt_blk=5, mod_row=mod_row)]
        else:
            if not last:
                raise NotImplementedError("expert layers are only supported as the last layer")
            h = h[0]
            u2, sel, idx, gw = _route(h, g_ffn[l], mod, 3, 4, moe_w_router[i].T, moe_b_router[i],
                                      n_rows, s_len, mod_row)
            tm = _pick(n_rows, (TM, 256, 128))
            tc = _pick(s_len, (TC, 128))
            n_slots = 2 * n_rows + N_EXPERTS * tm
            pos, tok_of_row, tile_expert, n_used = _dispatch_plan(sel, idx, tm, n_slots, tc)
            xs = _gather_rows(u2, tok_of_row, tm)
            mid = _matmul([[xs]], [moe_w_gate[i], moe_w_up[i]], n_rows=n_slots, out_dtype=BF16,
                          epi="swiglu", tile_expert=tile_expert, n_used=n_used)
            y = _matmul([[mid]], [moe_w_down[i]], n_rows=n_slots, out_dtype=F32,
                        tile_expert=tile_expert, n_used=n_used)
            out = _combine(y, h, pos, gw.T, mod, 5, mod_row, g_final, n_rows)
            return out.reshape(bsz, s_len, d)

    tm = _pick(n_lat, (TM, 256, 128))
    out = pl.pallas_call(
        _final_norm_kernel,
        out_shape=jax.ShapeDtypeStruct((n_lat, d), F32),
        grid=(n_lat // tm,),
        in_specs=[pl.BlockSpec((tm, d), lambda m: (m, 0)), pl.BlockSpec((1, d), lambda m: (0, 0))],
        out_specs=pl.BlockSpec((tm, d), lambda m: (m, 0)),
        compiler_params=_cparams(("arbitrary",)),
        name="final_norm",
    )(h[0], g_final.reshape(1, d))
    return out.reshape(bsz, s_len, d)
```

```python
import functools
import math

import numpy as np
import jax
import jax.numpy as jnp
from jax import lax
from jax.experimental import pallas as pl
from jax.experimental.pallas import tpu as pltpu

F32 = jnp.float32
BF16 = jnp.bfloat16

GRID_W = 64
HEAD_DIM = 64
ROPE_THETA = 10000.0
EPS = 1e-6
A_HEADS, A_QK_DIM, A_V_DIM = 4, 64, 128
B_HEADS, B_KV_HEADS = 8, 2
C_HEADS, C_Q_RANK, C_KV_RANK, C_NOPE_DIM, C_ROPE_DIM, C_V_DIM = 8, 768, 256, 64, 32, 64
D_HEADS, D_KV_HEADS = 8, 2
WINDOW = 128
N_EXPERTS = 8
LOG2E = 1.4426950408889634

LANES = 128
BF16_SUBLANES = 16
VMEM_LIMIT = 48 * 2**20
VMEM_LIMIT_BIG = 56 * 2**20

TM = 512
TM_X = 1024
TN = 512
TR = 256
TQ = 512
TQ_WIN = 256
TC = 256
DMA_UNROLL = 8

OFF_QA, OFF_QB, OFF_QC, OFF_QD = 0, 512, 1024, 1792
OFF_KA, OFF_VA, OFF_KB, OFF_VB = 2304, 2816, 3328, 3456
OFF_CKV, OFF_KR, OFF_KD, OFF_VD = 3584, 3840, 3872, 4000
IN_COLS = 4128


def _cparams(sem, vmem=VMEM_LIMIT):
    return pltpu.CompilerParams(dimension_semantics=sem, vmem_limit_bytes=vmem)


def _pick(n, prefs):
    for p in prefs:
        if n % p == 0:
            return p
    return n


def _row_tile(n_rows, s_len, prefs):
    for p in prefs:
        if n_rows % p == 0 and s_len % p == 0:
            return p
    raise ValueError("no row tile fits")


def _rms_modulate(x, g, sc, sh):
    ms = jnp.mean(x * x, axis=-1, keepdims=True)
    return (x * lax.rsqrt(ms + EPS) * g) * (1.0 + sc) + sh


def _mod_kernel(c_ref, w_ref, b_ref, o_ref):
    a = c_ref[...]
    a = (a * jax.nn.sigmoid(a)).astype(BF16)
    o_ref[0] = jnp.dot(a, w_ref[0].astype(BF16), preferred_element_type=F32) + b_ref[0]


def _modulation(cc, w_mod, b_mod):
    depth, d, d6 = w_mod.shape
    rows = cc.shape[0]
    tn = _pick(d6, (1024, 512, 256, 128))
    return pl.pallas_call(
        _mod_kernel,
        out_shape=jax.ShapeDtypeStruct((depth, rows, d6), F32),
        grid=(depth, d6 // tn),
        in_specs=[
            pl.BlockSpec((rows, d), lambda l, n: (0, 0)),
            pl.BlockSpec((1, d, tn), lambda l, n: (l, 0, n)),
            pl.BlockSpec((1, 1, tn), lambda l, n: (l, 0, n)),
        ],
        out_specs=pl.BlockSpec((1, rows, tn), lambda l, n: (l, 0, n)),
        compiler_params=_cparams(("arbitrary", "arbitrary")),
        name="modulation",
    )(cc, w_mod, b_mod.reshape(depth, 1, d6))


def _stacked_specs(parts, block, tile_rows, col_of, n_lead):
    def grid_m(idx):
        return idx[n_lead]

    if len(parts) == 1:
        return [pl.BlockSpec(block, lambda *idx: (grid_m(idx), col_of(idx)))], None
    mt_a = parts[0].shape[0] // tile_rows
    return ([pl.BlockSpec(block, lambda *idx: (jnp.minimum(grid_m(idx), mt_a - 1), col_of(idx))),
             pl.BlockSpec(block, lambda *idx: (jnp.maximum(grid_m(idx) - mt_a, 0), col_of(idx)))], mt_a)


def _nm_mm_kernel(*refs, n_h, mt_a, n_g):
    h_refs = refs[:n_h]
    g_ref, sh_ref, sc_ref = refs[n_h:n_h + 3]
    w_refs, o_ref, u_ref = refs[n_h + 3:n_h + 3 + n_g], refs[n_h + 3 + n_g], refs[n_h + 4 + n_g]
    m = pl.program_id(0)

    def fill(h_ref):
        u_ref[...] = _rms_modulate(h_ref[...], g_ref[...], sc_ref[0], sh_ref[0]).astype(BF16)

    first = pl.program_id(1) == 0
    if n_h == 1:
        pl.when(first)(lambda: fill(h_refs[0]))
    else:
        pl.when(first & (m < mt_a))(lambda: fill(h_refs[0]))
        pl.when(first & (m >= mt_a))(lambda: fill(h_refs[1]))

    u = u_ref[...]
    accs = [jnp.dot(u, w[...].astype(BF16), preferred_element_type=F32) for w in w_refs]
    out = accs[0] if n_g == 1 else (accs[0] * jax.nn.sigmoid(accs[0])) * accs[1]
    o_ref[...] = out.astype(o_ref.dtype)


def _nm_matmul(hs, g, mod, sh_blk, sc_blk, ws, *, n_rows, s_len, mod_row, out_dtype):
    d = hs[0].shape[1]
    n_cols = ws[0].shape[1]
    tm = _row_tile(n_rows, s_len, (TM_X, 512, 256, 128))
    tn = TN
    h_specs, mt_a = _stacked_specs(hs, (tm, d), tm, lambda idx: 0, 0)
    in_specs = h_specs + [
        pl.BlockSpec((1, d), lambda m, n: (0, 0)),
        pl.BlockSpec((1, 1, d), lambda m, n: (mod_row(m * tm), 0, sh_blk)),
        pl.BlockSpec((1, 1, d), lambda m, n: (mod_row(m * tm), 0, sc_blk)),
    ] + [pl.BlockSpec((d, tn), lambda m, n: (0, n)) for _ in ws]
    return pl.pallas_call(
        functools.partial(_nm_mm_kernel, n_h=len(hs), mt_a=mt_a, n_g=len(ws)),
        out_shape=jax.ShapeDtypeStruct((n_rows, n_cols), out_dtype),
        grid=(n_rows // tm, pl.cdiv(n_cols, tn)),
        in_specs=in_specs,
        out_specs=pl.BlockSpec((tm, tn), lambda m, n: (m, n)),
        scratch_shapes=[pltpu.VMEM((tm, d), BF16)],
        compiler_params=_cparams(("arbitrary", "arbitrary"), VMEM_LIMIT_BIG),
        name="normmod_matmul",
    )(*hs, g.reshape(1, d), mod, mod, *ws)


def _final_norm_kernel(x_ref, g_ref, o_ref):
    x = x_ref[...]
    ms = jnp.mean(x * x, axis=-1, keepdims=True)
    o_ref[...] = x * lax.rsqrt(ms + EPS) * g_ref[...]


def _mm_kernel(te_ref, nu_ref, *refs, x_parts, res_parts, mt_a, n_g, epi):
    n_x = len(x_parts)
    refs = list(refs)

    def take(n):
        out = refs[:n]
        del refs[:n]
        return out

    x_refs = [take(p) for p in x_parts]
    w_refs = take(n_x * n_g)
    if epi == "resgate":
        res_refs = take(res_parts)
        gt_ref, = take(1)
    o_ref, = take(1)
    wc_refs = refs
    m = pl.program_id(1)

    def stacked(parts):
        if len(parts) == 1:
            return parts[0][...]
        return jnp.where(m < mt_a, parts[0][...], parts[1][...])

    panel_changed = (m == 0) | (te_ref[m] != te_ref[jnp.maximum(m - 1, 0)])

    @pl.when(panel_changed)
    def _():
        for w, wc in zip(w_refs, wc_refs):
            wc[...] = w[0].astype(BF16)

    @pl.when(m < nu_ref[0])
    def _():
        accs = []
        for g in range(n_g):
            acc = None
            for i in range(n_x):
                part = jnp.dot(stacked(x_refs[i]).astype(BF16), wc_refs[g * n_x + i][...],
                               preferred_element_type=F32)
                acc = part if acc is None else acc + part
            accs.append(acc)
        if epi == "plain":
            out = accs[0]
        elif epi == "swiglu":
            a = accs[0]
            out = (a * jax.nn.sigmoid(a)) * accs[1]
        else:
            out = stacked(res_refs) + gt_ref[0] * accs[0]
        o_ref[...] = out.astype(o_ref.dtype)

    @pl.when(m >= nu_ref[0])
    def _():
        o_ref[...] = jnp.zeros_like(o_ref)


def _matmul(xs, ws, *, n_rows, out_dtype, epi="plain", tile_expert=None, n_used=None,
            res=None, mod=None, gt_blk=None, mod_row=None, tm=TM, tn=TN):
    n_x, n_g = len(xs), len(ws)
    n_cols = ws[0].shape[2]
    tn = _pick(n_cols, (tn, 512, 256, 128))
    tm = _pick(n_rows, (tm, 256, 128))
    mt, nt = n_rows // tm, n_cols // tn
    if tile_expert is None:
        tile_expert = jnp.zeros((mt,), jnp.int32)
        n_used = jnp.full((1,), mt, jnp.int32)
    ks = [x[0].shape[1] for x in xs]
    in_specs, args, mt_a = [], [], None
    for x, k in zip(xs, ks):
        if len(x) == 1:
            in_specs.append(pl.BlockSpec((tm, k), lambda n, m, te, nu: (jnp.minimum(m, nu[0] - 1), 0)))
        else:
            specs, mt_a = _stacked_specs(x, (tm, k), tm, lambda idx: 0, 1)
            in_specs += specs
        args += list(x)
    for g in range(n_g):
        for i, k in enumerate(ks):
            in_specs.append(pl.BlockSpec((1, k, tn), lambda n, m, te, nu, i=i: (te[m], i, n)))
            args.append(ws[g])
    if epi == "resgate":
        blk0 = gt_blk * (n_cols // tn)
        specs, mt_res = _stacked_specs(res, (tm, tn), tm, lambda idx: idx[0], 1)
        mt_a = mt_res if mt_res is not None else mt_a
        in_specs += specs
        in_specs.append(pl.BlockSpec((1, 1, tn), lambda n, m, te, nu: (mod_row(m * tm), 0, blk0 + n)))
        args += list(res) + [mod]
    scratch = [pltpu.VMEM((k, tn), BF16) for _ in range(n_g) for k in ks]
    return pl.pallas_call(
        functools.partial(_mm_kernel, x_parts=tuple(len(x) for x in xs),
                          res_parts=len(res) if res is not None else 0, mt_a=mt_a, n_g=n_g, epi=epi),
        out_shape=jax.ShapeDtypeStruct((n_rows, n_cols), out_dtype),
        grid_spec=pltpu.PrefetchScalarGridSpec(
            num_scalar_prefetch=2,
            grid=(nt, mt),
            in_specs=in_specs,
            out_specs=pl.BlockSpec((tm, tn), lambda n, m, te, nu: (m, n)),
            scratch_shapes=scratch,
        ),
        compiler_params=_cparams(("arbitrary", "arbitrary")),
        name="matmul_" + epi,
    )(tile_expert, n_used, *args)


def _rope(x, cos, sin_signed, half):
    lane = lax.broadcasted_iota(jnp.int32, (x.shape[0], LANES), 1)
    first = (lane & (2 * half - 1)) < half
    blocks = []
    for j in range(x.shape[1] // LANES):
        xj = x[:, j * LANES:(j + 1) * LANES]
        partner = jnp.where(first, pltpu.roll(xj, LANES - half, 1), pltpu.roll(xj, half, 1))
        blocks.append(xj * cos + partner * sin_signed)
    return blocks[0] if len(blocks) == 1 else jnp.concatenate(blocks, axis=-1)


def _group_mean_sq(x, g_ref):
    x2 = x * x
    hi = x2.astype(BF16)
    lo = (x2 - hi.astype(F32)).astype(BF16)
    g = g_ref[...]
    return (jnp.dot(hi, g, preferred_element_type=F32) + jnp.dot(lo, g, preferred_element_type=F32))


def _dup_halves(x):
    lane = lax.broadcasted_iota(jnp.int32, x.shape, 1)
    lo = lane < HEAD_DIM
    swapped = pltpu.roll(x, HEAD_DIM, 1)
    return jnp.concatenate([jnp.where(lo, x, swapped), jnp.where(lo, swapped, x)], axis=-1)


def _store_transposed(dst_ref, x):
    for j in range(x.shape[1] // LANES):
        dst_ref[0, j * LANES:(j + 1) * LANES, :] = x[:, j * LANES:(j + 1) * LANES].T.astype(BF16)


def _proj_prep_kernel(*refs, n_h, mt_a):
    refs = list(refs)
    h_refs = [refs.pop(0) for _ in range(n_h)]
    (g_ref, sh_ref, sc_ref, w_ref, c64_ref, s64_ref, cq_ref, sq_ref, ck_ref, sk_ref,
     bqn_ref, bkn_ref, cqn_ref, ckvn_ref, g512_ref, g128_ref, wq_ref, wkk_ref, wkv_ref, place_ref,
     qa_ref, qb_ref, qc_ref, qd_ref,
     ka_ref, kb_ref, kc_ref, kd_ref, vd_ref, vta_ref, vtb_ref, vtc_ref, u_ref) = refs

    def fill(h_ref):
        u_ref[...] = _rms_modulate(h_ref[...], g_ref[...], sc_ref[0], sh_ref[0]).astype(BF16)

    if n_h == 1:
        fill(h_refs[0])
    else:
        t = pl.program_id(0)
        pl.when(t < mt_a)(lambda: fill(h_refs[0]))
        pl.when(t >= mt_a)(lambda: fill(h_refs[1]))

    def proj(off, width):
        return jnp.dot(u_ref[...], w_ref[:, off:off + width], preferred_element_type=F32)

    c64, s64 = c64_ref[...], s64_ref[...]
    c128, s128 = c64, s64
    scale64 = HEAD_DIM ** -0.5
    scale_mla = (C_NOPE_DIM + C_ROPE_DIM) ** -0.5

    qa_ref[...] = (_rope(proj(OFF_QA, 512), c64, s64, 16) * (scale64 * LOG2E)).astype(BF16)
    ka_ref[0] = _rope(proj(OFF_KA, 512), c64, s64, 16).astype(BF16)
    _store_transposed(vta_ref, proj(OFF_VA, 512))

    xb = proj(OFF_QB, 512)
    yb = xb * lax.rsqrt(_group_mean_sq(xb, g512_ref) + EPS) * bqn_ref[...]
    qb_ref[...] = (_rope(yb, c64, s64, 16) * (scale64 * LOG2E)).astype(BF16)
    kvb = proj(OFF_KB, 256)
    xk = kvb[:, :LANES]
    yk = xk * lax.rsqrt(_group_mean_sq(xk, g128_ref) + EPS) * bkn_ref[...]
    kb_ref[0] = _dup_halves(_rope(yk, c128, s128, 16)).astype(BF16)
    _store_transposed(vtb_ref, kvb[:, LANES:])

    xq = proj(OFF_QC, C_Q_RANK)
    yq = xq * lax.rsqrt(jnp.mean(xq * xq, axis=-1, keepdims=True) + EPS) * cqn_ref[...]
    qf = jnp.dot(yq.astype(BF16), wq_ref[...], preferred_element_type=F32)
    qc_ref[...] = (_rope(qf, cq_ref[...], sq_ref[...], 8) * (scale_mla * LOG2E)).astype(BF16)
    xc = proj(OFF_CKV, C_KV_RANK)
    yc = (xc * lax.rsqrt(jnp.mean(xc * xc, axis=-1, keepdims=True) + EPS) * ckvn_ref[...]).astype(BF16)
    tail = proj(OFF_KR, IN_COLS - OFF_KR)
    kr = _rope(tail[:, :LANES], ck_ref[...], sk_ref[...], 8).astype(BF16)
    kc = (jnp.dot(yc, wkk_ref[...], preferred_element_type=F32)
          + jnp.dot(kr, place_ref[...], preferred_element_type=F32))
    kc_ref[0] = kc.astype(BF16)
    _store_transposed(vtc_ref, jnp.dot(yc, wkv_ref[...], preferred_element_type=F32))

    qd_ref[...] = (_rope(proj(OFF_QD, 512), c64, s64, 16) * scale64).astype(BF16)
    kd0, vd0 = OFF_KD - OFF_KR, OFF_VD - OFF_KR
    kd_ref[0] = _dup_halves(_rope(tail[:, kd0:kd0 + LANES], c128, s128, 16)).astype(BF16)
    vd_ref[0] = _dup_halves(tail[:, vd0:vd0 + LANES]).astype(BF16)


def _proj_prep(hs, g, mod, w_in_bf16, tables, consts, dims, mod_row):
    bsz, s_len, n_ctx = dims
    d = hs[0].shape[1]
    n_rows = sum(a.shape[0] for a in hs)
    sk = n_ctx + s_len
    tr = _pick(math.gcd(s_len, n_ctx), (TR, 128))
    n_lat_t, lat_pb, ctx_pb = bsz * s_len // tr, s_len // tr, n_ctx // tr

    def is_lat(t):
        return t < n_lat_t

    def tbl_idx(t):
        return jnp.where(is_lat(t), t % lat_pb, lat_pb)

    def kv_b(t):
        return jnp.where(is_lat(t), t // lat_pb, (t - n_lat_t) // ctx_pb)

    def kv_j(t):
        return jnp.where(is_lat(t), ctx_pb + t % lat_pb, (t - n_lat_t) % ctx_pb)

    def row_spec(w):
        return pl.BlockSpec((tr, w), lambda t: (t, 0))

    def tbl_spec(w):
        return pl.BlockSpec((tr, w), lambda t: (tbl_idx(t), 0))

    def const_spec(a):
        return pl.BlockSpec(a.shape, lambda t: (0,) * a.ndim)

    def kv_spec(w):
        return pl.BlockSpec((1, tr, w), lambda t: (kv_b(t), kv_j(t), 0))

    def vt_spec(w):
        return pl.BlockSpec((1, w, tr), lambda t: (kv_b(t), 0, kv_j(t)))

    q_widths = (512, 512, 1024, 512)
    kv_widths = (512, 256, 1024, 256, 256)
    vt_widths = (512, 128, 512)
    out_shape = ([jax.ShapeDtypeStruct((n_rows, w), BF16) for w in q_widths]
                 + [jax.ShapeDtypeStruct((bsz, sk, w), BF16) for w in kv_widths]
                 + [jax.ShapeDtypeStruct((bsz, w, sk), BF16) for w in vt_widths])
    out_specs = ([row_spec(w) for w in q_widths] + [kv_spec(w) for w in kv_widths]
                 + [vt_spec(w) for w in vt_widths])
    h_specs, mt_a = _stacked_specs(hs, (tr, d), tr, lambda idx: 0, 0)
    in_specs = (h_specs
                + [pl.BlockSpec((1, d), lambda t: (0, 0)),
                   pl.BlockSpec((1, 1, d), lambda t: (mod_row(t * tr), 0, 0)),
                   pl.BlockSpec((1, 1, d), lambda t: (mod_row(t * tr), 0, 1)),
                   pl.BlockSpec(w_in_bf16.shape, lambda t: (0, 0), pipeline_mode=pl.Buffered(1))]
                + [tbl_spec(t.shape[1]) for t in tables] + [const_spec(a) for a in consts])
    return pl.pallas_call(
        functools.partial(_proj_prep_kernel, n_h=len(hs), mt_a=mt_a),
        out_shape=out_shape,
        grid=(n_rows // tr,),
        in_specs=in_specs,
        out_specs=out_specs,
        scratch_shapes=[pltpu.VMEM((tr, d), BF16)],
        compiler_params=_cparams(("arbitrary",), VMEM_LIMIT_BIG),
        name="proj_prep",
    )(*hs, g.reshape(1, d), mod, mod, w_in_bf16, *tables, *consts)


def _scores(a, b):
    return lax.dot_general(a, b, (((1,), (1,)), ((), ())), preferred_element_type=F32)


def _half_masks(shape):
    lane = lax.broadcasted_iota(jnp.int32, shape, 1)
    lo = lane < HEAD_DIM
    return lo, jnp.logical_not(lo)


def _softmax_pv_t(s, v_t):
    m = jnp.max(s, axis=0, keepdims=True)
    e = jnp.exp2(s - m).astype(BF16)
    dv = v_t.shape[0]
    v_ext = jnp.concatenate([v_t, jnp.ones((BF16_SUBLANES, v_t.shape[1]), BF16)], axis=0)
    oe = jnp.dot(v_ext, e, preferred_element_type=F32)
    return oe[:dv] / oe[dv:dv + 1]


def _run_units(score_fns, finish_fns):
    s = score_fns[0]()
    for i, finish in enumerate(finish_fns):
        s_next = score_fns[i + 1]() if i + 1 < len(score_fns) else None
        finish(s)
        s = s_next


def _gqa_t_kernel(q_ref, k_ref, vt_ref, o_ref):
    masks = _half_masks((q_ref.shape[0], LANES))
    n_pairs = q_ref.shape[1] // LANES
    pairs_per_group = B_HEADS // B_KV_HEADS // 2
    outs = {}

    def score(j, z):
        g = j // pairs_per_group
        q = q_ref[:, j * LANES:(j + 1) * LANES]
        return lambda: _scores(k_ref[0, :, g * LANES:(g + 1) * LANES],
                               jnp.where(masks[z], q, jnp.zeros_like(q)))

    def finish(j, z):
        g = j // pairs_per_group

        def fin(s):
            outs[z] = _softmax_pv_t(s, vt_ref[0, g * HEAD_DIM:(g + 1) * HEAD_DIM, :])
            if z == 1:
                pair = jnp.concatenate([outs[0], outs[1]], axis=0)
                o_ref[:, j * LANES:(j + 1) * LANES] = pair.T.astype(o_ref.dtype)
        return fin

    units = [(j, z) for j in range(n_pairs) for z in range(2)]
    _run_units([score(j, z) for j, z in units], [finish(j, z) for j, z in units])


def _mla_t_kernel(q_ref, k_ref, vt_ref, o_ref):
    n_heads = q_ref.shape[1] // LANES
    outs = {}

    def score(u):
        return lambda: _scores(k_ref[0, :, u * LANES:(u + 1) * LANES], q_ref[:, u * LANES:(u + 1) * LANES])

    def finish(u):
        def fin(s):
            outs[u % 2] = _softmax_pv_t(s, vt_ref[0, u * C_V_DIM:(u + 1) * C_V_DIM, :])
            if u % 2 == 1:
                pair = jnp.concatenate([outs[0], outs[1]], axis=0)
                j = u // 2
                o_ref[:, j * LANES:(j + 1) * LANES] = pair.T.astype(o_ref.dtype)
        return fin

    _run_units([score(u) for u in range(n_heads)], [finish(u) for u in range(n_heads)])


def _diff_t_kernel(lam_ref, subln_ref, q0_ref, q1_ref, k0_ref, k1_ref, vt_ref, o_ref, *, lam_init):
    t = lam_ref[...]
    lam = (jnp.exp(jnp.sum(t[0:1] * t[1:2], axis=-1, keepdims=True))
           - jnp.exp(jnp.sum(t[2:3] * t[3:4], axis=-1, keepdims=True)) + lam_init)
    q_refs, k_refs = (q0_ref, q1_ref), (k0_ref, k1_ref)
    masks = _half_masks((q0_ref.shape[0], LANES))
    first_map = {}

    def score(z, mp):
        j = z // 2
        q = q_refs[mp][:, j * LANES:(j + 1) * LANES]
        return lambda: _scores(k_refs[mp][0, :, j * LANES:(j + 1) * LANES],
                               jnp.where(masks[z % 2], q, jnp.zeros_like(q)))

    def finish(z, mp):
        def fin(s):
            a = _softmax_pv_t(s, vt_ref[0, z * A_V_DIM:(z + 1) * A_V_DIM, :])
            if mp == 0:
                first_map[z] = a
                return
            d = (first_map[z] - lam * a).T
            y = d * lax.rsqrt(jnp.mean(d * d, axis=-1, keepdims=True) + EPS) * subln_ref[...]
            o_ref[:, z * A_V_DIM:(z + 1) * A_V_DIM] = (y * (1.0 - lam_init)).astype(o_ref.dtype)
        return fin

    units = [(z, mp) for z in range(A_HEADS) for mp in range(2)]
    _run_units([score(z, mp) for z, mp in units], [finish(z, mp) for z, mp in units])


def _attend_sink(q, k, v, sink):
    s = _scores(q, k)
    m = jnp.maximum(jnp.max(s, axis=-1, keepdims=True), sink)
    e = jnp.exp(s - m)
    l = jnp.sum(e, axis=-1, keepdims=True) + jnp.exp(sink - m)
    return jnp.dot(e.astype(BF16), v, preferred_element_type=F32) / l


def _window_kernel(sink_ref, q_ref, k_ref, v_ref, o_ref, *, s_len, n_ctx, n_q_lat, with_ctx):
    tq = q_ref.shape[0]
    band = tq + 2 * WINDOW
    group = pl.program_id(1)
    i = pl.program_id(2)
    heads_per_group = D_HEADS // D_KV_HEADS

    def body(is_lat):
        k_c, v_c = k_ref[0, :n_ctx, :], v_ref[0, :n_ctx, :]
        masks = _half_masks((tq, LANES))
        if is_lat:
            start = pl.multiple_of(jnp.clip(i * tq - WINDOW, 0, s_len - band), LANES)
            row0 = pl.multiple_of(n_ctx + start, LANES)
            k_b = k_ref[0, pl.ds(row0, band), :]
            v_b = v_ref[0, pl.ds(row0, band), :]
            qpos = i * tq + lax.broadcasted_iota(jnp.int32, (tq, band), 0)
            kpos = start + lax.broadcasted_iota(jnp.int32, (tq, band), 1)
            in_band = jnp.abs(qpos - kpos) <= WINDOW
        for j in range(q_ref.shape[1] // LANES):
            q = q_ref[:, j * LANES:(j + 1) * LANES]
            outs = []
            for z in range(2):
                qz = jnp.where(masks[z], q, jnp.zeros_like(q))
                sink = sink_ref[group * heads_per_group + 2 * j + z]
                if not is_lat:
                    outs.append(_attend_sink(qz, k_c, v_c, sink))
                    continue
                s_c = _scores(qz, k_c)
                s_b = jnp.where(in_band, _scores(qz, k_b), -jnp.inf)
                m = jnp.maximum(jnp.maximum(jnp.max(s_c, axis=-1, keepdims=True),
                                            jnp.max(s_b, axis=-1, keepdims=True)), sink)
                e_c, e_b = jnp.exp(s_c - m), jnp.exp(s_b - m)
                l = (jnp.sum(e_c, axis=-1, keepdims=True) + jnp.sum(e_b, axis=-1, keepdims=True)
                     + jnp.exp(sink - m))
                o = (jnp.dot(e_c.astype(BF16), v_c, preferred_element_type=F32)
                     + jnp.dot(e_b.astype(BF16), v_b, preferred_element_type=F32))
                outs.append(o / l)
            o_ref[:, j * LANES:(j + 1) * LANES] = jnp.where(masks[0], outs[0], outs[1]).astype(o_ref.dtype)

    if not with_ctx:
        body(True)
        return

    @pl.when(i < n_q_lat)
    def _():
        body(True)

    @pl.when(i >= n_q_lat)
    def _():
        body(False)


def _mixers(ops, dims, params, lam_init, with_ctx):
    bsz, s_len, n_ctx = dims
    qa, qb, qc, qd, ka, kb, kc, kd, vd, vta, vtb, vtc = ops
    lam_vecs, subln, sink = params
    n_lat = bsz * s_len
    sk_all = n_ctx + s_len

    def const_spec(a):
        return pl.BlockSpec(a.shape, lambda b, i: (0,) * a.ndim)

    def dense_set(tq, nk, row0_tiles, n_q, out_rows):
        def q_row(b, i):
            return row0_tiles + b * n_q + i

        def qspec(w, col=0):
            return pl.BlockSpec((tq, w), lambda b, i: (q_row(b, i), col))

        def kspec(w, col=0):
            return pl.BlockSpec((1, nk, w), lambda b, i: (b, 0, col))

        def vtspec(w):
            return pl.BlockSpec((1, w, nk), lambda b, i: (b, 0, 0))

        def call(kernel, in_specs, args, name):
            return pl.pallas_call(
                kernel,
                out_shape=jax.ShapeDtypeStruct((out_rows, 512), BF16),
                grid=(bsz, n_q),
                in_specs=in_specs,
                out_specs=pl.BlockSpec((tq, 512), lambda b, i: (b * n_q + i, 0)),
                compiler_params=_cparams(("arbitrary", "arbitrary"), VMEM_LIMIT_BIG),
                name=name,
            )(*args)

        oa = call(functools.partial(_diff_t_kernel, lam_init=lam_init),
                  [const_spec(lam_vecs), const_spec(subln), qspec(256, 0), qspec(256, 1),
                   kspec(256, 0), kspec(256, 1), vtspec(512)],
                  [lam_vecs, subln, qa, qa, ka, ka, vta], "attn_diff")
        ob = call(_gqa_t_kernel, [qspec(512), kspec(256), vtspec(128)], [qb, kb, vtb], "attn_qknorm")
        oc = call(_mla_t_kernel, [qspec(1024), kspec(1024), vtspec(512)], [qc, kc, vtc], "attn_mla")
        return [oa, ob, oc]

    tq = _pick(s_len, (TQ, 256, 128))
    dense = [[o] for o in dense_set(tq, sk_all, 0, s_len // tq, n_lat)]
    if with_ctx:
        tq_c = _pick(n_ctx, (TQ, 256, 128))
        ctx = dense_set(tq_c, n_ctx, n_lat // tq_c, n_ctx // tq_c, bsz * n_ctx)
        dense = [a + [b] for a, b in zip(dense, ctx)]

    tw = _pick(math.gcd(s_len, n_ctx), (TQ_WIN, 128))
    n_q_lat, n_q_ctx = s_len // tw, n_ctx // tw
    n_q = n_q_lat + n_q_ctx if with_ctx else n_q_lat
    out_rows = n_lat + bsz * n_ctx if with_ctx else n_lat

    def w_row(b, i):
        return jnp.where(i < n_q_lat, b * n_q_lat + i, n_lat // tw + b * n_q_ctx + (i - n_q_lat))

    od = pl.pallas_call(
        functools.partial(_window_kernel, s_len=s_len, n_ctx=n_ctx, n_q_lat=n_q_lat, with_ctx=with_ctx),
        out_shape=jax.ShapeDtypeStruct((out_rows, 512), BF16),
        grid=(bsz, D_KV_HEADS, n_q),
        in_specs=[pl.BlockSpec(memory_space=pltpu.SMEM),
                  pl.BlockSpec((tw, 256), lambda b, g, i: (w_row(b, i), g)),
                  pl.BlockSpec((1, sk_all, 128), lambda b, g, i: (b, 0, g)),
                  pl.BlockSpec((1, sk_all, 128), lambda b, g, i: (b, 0, g))],
        out_specs=pl.BlockSpec((tw, 256), lambda b, g, i: (w_row(b, i), g)),
        compiler_params=_cparams(("arbitrary", "arbitrary", "arbitrary")),
        name="attn_window",
    )(sink, qd, kd, vd)
    return dense + [[od]]


def _route_kernel(h_ref, g_ref, sh_ref, sc_ref, w_ref, b_ref, u_ref, sel_ref, idx_ref, gw_ref):
    u = _rms_modulate(h_ref[...], g_ref[...], sc_ref[0], sh_ref[0])
    half = u.shape[1] // 2
    ub = u.astype(BF16).astype(F32)
    u_ref[...] = (pltpu.bitcast(ub[:, :half], jnp.uint32) >> 16) | pltpu.bitcast(ub[:, half:], jnp.uint32)
    w = w_ref[...]
    u_hi = u.astype(BF16)
    u_lo = (u - u_hi.astype(F32)).astype(BF16)
    w_hi = w.astype(BF16)
    w_lo = (w - w_hi.astype(F32)).astype(BF16)
    logits = (_scores(w_hi, u_hi) + _scores(w_hi, u_lo) + _scores(w_lo, u_hi) + _scores(w_lo, u_lo)
              + b_ref[...])
    ids = lax.broadcasted_iota(jnp.int32, logits.shape, 0).astype(F32)
    m1 = jnp.max(logits, axis=0, keepdims=True)
    i1 = jnp.min(jnp.where(logits == m1, ids, float(N_EXPERTS)), axis=0, keepdims=True)
    first = ids == i1
    rest = jnp.where(first, -jnp.inf, logits)
    m2 = jnp.max(rest, axis=0, keepdims=True)
    i2 = jnp.min(jnp.where(rest == m2, ids, float(N_EXPERTS)), axis=0, keepdims=True)
    second = ids == i2
    e = jnp.exp(m2 - m1)
    w1 = 1.0 / (1.0 + e)
    w2 = e / (1.0 + e)
    sel_ref[...] = jnp.where(first | second, 1, 0).astype(jnp.int32)
    idx_ref[...] = jnp.where(ids == 0.0, i1, jnp.where(ids == 1.0, i2, 0.0)).astype(jnp.int32)
    gw_ref[...] = jnp.where(ids == 0.0, w1, jnp.where(ids == 1.0, w2, 0.0))


def _route(h, g, mod, sh_blk, sc_blk, w_router_t, b_router, n_tok, s_len, mod_row):
    d = h.shape[1]
    tm = _row_tile(n_tok, s_len, (TM, 256, 128))
    outs = ([jax.ShapeDtypeStruct((n_tok, d // 2), jnp.uint32)]
            + [jax.ShapeDtypeStruct((N_EXPERTS, n_tok), dt) for dt in (jnp.int32, jnp.int32, F32)])
    return pl.pallas_call(
        _route_kernel,
        out_shape=outs,
        grid=(n_tok // tm,),
        in_specs=[pl.BlockSpec((tm, d), lambda m: (m, 0)),
                  pl.BlockSpec((1, d), lambda m: (0, 0)),
                  pl.BlockSpec((1, 1, d), lambda m: (mod_row(m * tm), 0, sh_blk)),
                  pl.BlockSpec((1, 1, d), lambda m: (mod_row(m * tm), 0, sc_blk)),
                  pl.BlockSpec((N_EXPERTS, d), lambda m: (0, 0)),
                  pl.BlockSpec((N_EXPERTS, 1), lambda m: (0, 0))],
        out_specs=[pl.BlockSpec((tm, d // 2), lambda m: (m, 0))]
        + [pl.BlockSpec((N_EXPERTS, tm), lambda m: (0, m))] * 3,
        compiler_params=_cparams(("arbitrary",)),
        name="moe_route",
    )(h, g.reshape(1, d), mod, mod, w_router_t, b_router.reshape(N_EXPERTS, 1))


def _row_copy(src, dst, sem, src_row, dst_row):
    return pltpu.make_async_copy(src.at[pl.ds(src_row, 1)], dst.at[pl.ds(dst_row, 1)], sem)


def _gather_kernel(tok_ref, nxt_ref, src_ref, o_ref, buf, sem):
    i = pl.program_id(0)
    n = pl.num_programs(0)
    tg = buf.shape[1]
    slot = i % 2

    def issue(t_ref, s):
        def body(r, carry):
            _row_copy(src_ref, buf.at[s], sem.at[s], t_ref[0, 0, r], r).start()
            return carry
        lax.fori_loop(0, tg, body, 0, unroll=DMA_UNROLL)

    @pl.when(i == 0)
    def _():
        issue(tok_ref, 0)

    @pl.when(i + 1 < n)
    def _():
        issue(nxt_ref, 1 - slot)

    def wait(r, carry):
        _row_copy(src_ref, buf.at[slot], sem.at[slot], 0, r).wait()
        return carry

    lax.fori_loop(0, tg, wait, 0, unroll=DMA_UNROLL)
    words = buf[slot]
    half = words.shape[1]
    o_ref[:, :half] = pltpu.bitcast(words << 16, F32).astype(o_ref.dtype)
    o_ref[:, half:] = pltpu.bitcast(words & jnp.uint32(0xFFFF0000), F32).astype(o_ref.dtype)


def _gather_rows(src, tok, tg):
    n_rows = tok.shape[0]
    dw = src.shape[1]
    d = 2 * dw
    n_t = n_rows // tg
    tok = tok.reshape(n_t, 1, tg)
    return pl.pallas_call(
        _gather_kernel,
        out_shape=jax.ShapeDtypeStruct((n_rows, d), BF16),
        grid=(n_t,),
        in_specs=[pl.BlockSpec((1, 1, tg), lambda i: (i, 0, 0), memory_space=pltpu.SMEM),
                  pl.BlockSpec((1, 1, tg), lambda i: (jnp.minimum(i + 1, n_t - 1), 0, 0),
                               memory_space=pltpu.SMEM),
                  pl.BlockSpec(memory_space=pl.ANY)],
        out_specs=pl.BlockSpec((tg, d), lambda i: (i, 0)),
        scratch_shapes=[pltpu.VMEM((2, tg, dw), jnp.uint32), pltpu.SemaphoreType.DMA((2,))],
        compiler_params=_cparams(("arbitrary",)),
        name="moe_gather",
    )(tok, tok, src)


def _combine_kernel(pos_ref, nxt_ref, y_ref, h_ref, gw_ref, gt_ref, gf_ref, o_ref, buf, sem):
    i = pl.program_id(0)
    n = pl.num_programs(0)
    tc = buf.shape[2]
    slot = i % 2

    def issue(p_ref, s):
        def body(r, carry):
            _row_copy(y_ref, buf.at[s, 0], sem.at[s], p_ref[0, 0, r], r).start()
            _row_copy(y_ref, buf.at[s, 1], sem.at[s], p_ref[0, 1, r], r).start()
            return carry
        lax.fori_loop(0, tc, body, 0, unroll=DMA_UNROLL)

    @pl.when(i == 0)
    def _():
        issue(pos_ref, 0)

    @pl.when(i + 1 < n)
    def _():
        issue(nxt_ref, 1 - slot)

    def wait(r, carry):
        _row_copy(y_ref, buf.at[slot, 0], sem.at[slot], 0, r).wait()
        _row_copy(y_ref, buf.at[slot, 1], sem.at[slot], 0, r).wait()
        return carry

    lax.fori_loop(0, tc, wait, 0, unroll=DMA_UNROLL)
    gw = gw_ref[...]
    moe = gw[:, 0:1] * buf[slot, 0] + gw[:, 1:2] * buf[slot, 1]
    x = h_ref[...] + gt_ref[0] * moe
    ms = jnp.mean(x * x, axis=-1, keepdims=True)
    o_ref[...] = x * lax.rsqrt(ms + EPS) * gf_ref[...]


def _combine(y, h, pos, gw_t, mod, gt_blk, mod_row, g_final, n_tok):
    d = h.shape[1]
    n_t, _, tc = pos.shape
    return pl.pallas_call(
        _combine_kernel,
        out_shape=jax.ShapeDtypeStruct((n_tok, d), F32),
        grid=(n_t,),
        in_specs=[pl.BlockSpec((1, 2, tc), lambda i: (i, 0, 0), memory_space=pltpu.SMEM),
                  pl.BlockSpec((1, 2, tc), lambda i: (jnp.minimum(i + 1, n_t - 1), 0, 0),
                               memory_space=pltpu.SMEM),
                  pl.BlockSpec(memory_space=pl.ANY),
                  pl.BlockSpec((tc, d), lambda i: (i, 0)),
                  pl.BlockSpec((tc, N_EXPERTS), lambda i: (i, 0)),
                  pl.BlockSpec((1, 1, d), lambda i: (mod_row(i * tc), 0, gt_blk)),
                  pl.BlockSpec((1, d), lambda i: (0, 0))],
        out_specs=pl.BlockSpec((tc, d), lambda i: (i, 0)),
        scratch_shapes=[pltpu.VMEM((2, 2, tc, d), F32), pltpu.SemaphoreType.DMA((2,))],
        compiler_params=_cparams(("arbitrary",)),
        name="moe_combine",
    )(pos, pos, y, h, gw_t, mod, g_final.reshape(1, d))


def _dispatch_plan(sel, idx, tm, n_slots, tc):
    n_tok = sel.shape[1]
    n_tiles = n_slots // tm
    counts = jnp.sum(sel, axis=1)
    padded = ((counts + tm - 1) // tm) * tm
    ends = jnp.cumsum(padded)
    offs = ends - padded
    pos = offs[:, None] + jnp.cumsum(sel, axis=1) - sel
    pos0 = jnp.take_along_axis(pos, idx[0:1], axis=0)[0]
    pos1 = jnp.take_along_axis(pos, idx[1:2], axis=0)[0]
    pos_tiles = jnp.stack([pos0.reshape(n_tok // tc, tc), pos1.reshape(n_tok // tc, tc)], axis=1)
    tile_start = jnp.arange(n_tiles, dtype=jnp.int32) * tm
    tile_expert = jnp.minimum(jnp.sum(ends[None, :] <= tile_start[:, None], axis=1), N_EXPERTS - 1)
    n_used = ends[-1] // tm
    tok = jnp.arange(n_tok, dtype=jnp.int32)
    tok_of_row = jnp.zeros((n_slots,), jnp.int32).at[jnp.concatenate([pos0, pos1])].set(
        jnp.concatenate([tok, tok]), unique_indices=True)
    return (pos_tiles.astype(jnp.int32), tok_of_row, tile_expert.astype(jnp.int32),
            n_used.reshape(1).astype(jnp.int32))


def _rope_tables(s_len, pad_rows):
    t = jnp.arange(s_len)
    rows, cols = (t // GRID_W).astype(F32), (t % GRID_W).astype(F32)

    def axis_tables(rot_dim):
        axis_dim = rot_dim // 2
        inv = ROPE_THETA ** (-jnp.arange(0, axis_dim, 2, dtype=F32) / axis_dim)
        ar, ac = rows[:, None] * inv[None, :], cols[:, None] * inv[None, :]
        cos = jnp.concatenate([jnp.cos(ar), jnp.cos(ar), jnp.cos(ac), jnp.cos(ac)], axis=1)
        sin = jnp.concatenate([-jnp.sin(ar), jnp.sin(ar), -jnp.sin(ac), jnp.sin(ac)], axis=1)
        return cos, sin

    def with_identity(cos, sin):
        w = cos.shape[1]
        return (jnp.concatenate([cos, jnp.ones((pad_rows, w), F32)], axis=0),
                jnp.concatenate([sin, jnp.zeros((pad_rows, w), F32)], axis=0))

    c64, s64 = axis_tables(HEAD_DIM)
    c64, s64 = jnp.tile(c64, (1, 2)), jnp.tile(s64, (1, 2))
    c32, s32 = axis_tables(C_ROPE_DIM)
    ones, zeros = jnp.ones((s_len, 1), F32), jnp.zeros((s_len, 1), F32)
    cq = jnp.concatenate([jnp.tile(ones, (1, 64)), c32, jnp.tile(ones, (1, 32))], axis=1)
    sq = jnp.concatenate([jnp.tile(zeros, (1, 64)), s32, jnp.tile(zeros, (1, 32))], axis=1)
    ck = jnp.concatenate([c32, jnp.tile(ones, (1, 96))], axis=1)
    sk = jnp.concatenate([s32, jnp.tile(zeros, (1, 96))], axis=1)
    out = []
    for c, s in ((c64, s64), (cq, sq), (ck, sk)):
        out += list(with_identity(c, s))
    return out


def _mla_weights(w_q_up, w_kv_up):
    qd = C_NOPE_DIM + C_ROPE_DIM
    wq = jnp.pad(w_q_up.reshape(C_Q_RANK, C_HEADS, qd), ((0, 0), (0, 0), (0, LANES - qd)))
    wkv = w_kv_up.reshape(C_KV_RANK, C_HEADS, C_NOPE_DIM + C_V_DIM)
    wkk = jnp.pad(wkv[:, :, :C_NOPE_DIM], ((0, 0), (0, 0), (0, LANES - C_NOPE_DIM)))
    wkv_v = wkv[:, :, C_NOPE_DIM:]
    return (wq.reshape(C_Q_RANK, C_HEADS * LANES).astype(BF16),
            wkk.reshape(C_KV_RANK, C_HEADS * LANES).astype(BF16),
            wkv_v.reshape(C_KV_RANK, C_HEADS * C_V_DIM).astype(BF16))


def _static_mats():
    g = (np.arange(512)[:, None] // HEAD_DIM == np.arange(512)[None, :] // HEAD_DIM) / HEAD_DIM
    place = np.zeros((LANES, C_HEADS * LANES), np.float32)
    for h in range(C_HEADS):
        place[np.arange(C_ROPE_DIM), h * LANES + C_NOPE_DIM + np.arange(C_ROPE_DIM)] = 1.0
    return (jnp.asarray(g, BF16), jnp.asarray(g[:128, :128], BF16), jnp.asarray(place, BF16))


def kernel(x, c, ctx, c_ctx, w_mod, b_mod, g_mix, g_ffn, g_final, w_in, w_out, a_lam_q1, a_lam_k1, a_lam_q2, a_lam_k2, a_subln, b_q_norm, b_k_norm, c_q_norm, c_kv_norm, c_w_q_up, c_w_kv_up, d_sink, ffn_w_gate, ffn_w_up, ffn_w_down, moe_w_router, moe_b_router, moe_w_gate, moe_w_up, moe_w_down):
    bsz, s_len, d = x.shape
    n_ctx = ctx.shape[1]
    depth = w_mod.shape[0]
    n_lat = bsz * s_len
    n_all = n_lat + bsz * n_ctx
    dims = (bsz, s_len, n_ctx)

    def mod_row(row0):
        return jnp.where(row0 < n_lat, row0 // s_len, bsz)

    mod_rows = -(-(bsz + 1) // 8) * 8
    cc = jnp.zeros((mod_rows, d), F32).at[:bsz].set(c).at[bsz].set(c_ctx)
    mods = _modulation(cc, w_mod, b_mod)

    tr = _pick(math.gcd(s_len, n_ctx), (TR, 128))
    tables = _rope_tables(s_len, tr)
    g512, g128, place = _static_mats()

    h = [x.reshape(n_lat, d), ctx.reshape(bsz * n_ctx, d)]
    for l in range(depth):
        last = l == depth - 1
        lam_init = 0.8 - 0.6 * math.exp(-0.3 * l)
        n_rows = n_lat if last else n_all
        mod = mods[l].reshape(mod_rows, 1, 6 * d)
        nm = dict(s_len=s_len, mod_row=mod_row)

        wq, wkk, wkv_v = _mla_weights(c_w_q_up[l], c_w_kv_up[l])
        consts = [jnp.tile(b_q_norm[l], 8)[None], jnp.tile(b_k_norm[l], 2)[None],
                  c_q_norm[l][None], c_kv_norm[l][None], g512, g128, wq, wkk, wkv_v, place]
        ops = _proj_prep(h, g_mix[l], mod, w_in[l].astype(BF16), tables, consts, dims, mod_row)
        lam_vecs = jnp.stack([a_lam_q1[l], a_lam_k1[l], a_lam_q2[l], a_lam_k2[l]])
        mix = _mixers(ops, dims, (lam_vecs, a_subln[l][None], d_sink[l]), lam_init, not last)
        h = [_matmul(mix, [w_out[l][None]], n_rows=n_rows, out_dtype=F32, epi="resgate",
                     res=h, mod=mod, gt_blk=2, mod_row=mod_row, tn=1024)]

        i = l // 2
        if l % 2 == 0:
            mid = _nm_matmul(h, g_ffn[l], mod, 3, 4, [ffn_w_gate[i], ffn_w_up[i]], n_rows=n_rows,
                             out_dtype=BF16, **nm)
            h = [_matmul([[mid]], [ffn_w_down[i][None]], n_rows=n_rows, out_dtype=F32, epi="resgate",
                         res=h, mod=mod, gt_blk=5, mod_row=mod_row)]
        else:
            if not last:
                raise NotImplementedError("expert layers are only supported as the last layer")
            h = h[0]
            u2, sel, idx, gw = _route(h, g_ffn[l], mod, 3, 4, moe_w_router[i].T, moe_b_router[i],
                                      n_rows, s_len, mod_row)
            tm = _pick(n_rows, (TM, 256, 128))
            tc = _pick(s_len, (TC, 128))
            n_slots = 2 * n_rows + N_EXPERTS * tm
            pos, tok_of_row, tile_expert, n_used = _dispatch_plan(sel, idx, tm, n_slots, tc)
            xs = _gather_rows(u2, tok_of_row, tm)
            mid = _matmul([[xs]], [moe_w_gate[i], moe_w_up[i]], n_rows=n_slots, out_dtype=BF16,
                          epi="swiglu", tile_expert=tile_expert, n_used=n_used)
            y = _matmul([[mid]], [moe_w_down[i]], n_rows=n_slots, out_dtype=F32,
                        tile_expert=tile_expert, n_used=n_used)
            out = _combine(y, h, pos, gw.T, mod, 5, mod_row, g_final, n_rows)
            return out.reshape(bsz, s_len, d)

    tm = _pick(n_lat, (TM, 256, 128))
    out = pl.pallas_call(
        _final_norm_kernel,
        out_shape=jax.ShapeDtypeStruct((n_lat, d), F32),
        grid=(n_lat // tm,),
        in_specs=[pl.BlockSpec((tm, d), lambda m: (m, 0)), pl.BlockSpec((1, d), lambda m: (0, 0))],
        out_specs=pl.BlockSpec((tm, d), lambda m: (m, 0)),
        compiler_params=_cparams(("arbitrary",)),
        name="final_norm",
    )(h[0], g_final.reshape(1, d))
    return out.reshape(bsz, s_len, d)
```

```python
import functools
import math

import numpy as np
import jax
import jax.numpy as jnp
from jax import lax
from jax.experimental import pallas as pl
from jax.experimental.pallas import tpu as pltpu

F32 = jnp.float32
BF16 = jnp.bfloat16

GRID_W = 64
HEAD_DIM = 64
ROPE_THETA = 10000.0
EPS = 1e-6
A_HEADS, A_QK_DIM, A_V_DIM = 4, 64, 128
B_HEADS, B_KV_HEADS = 8, 2
C_HEADS, C_Q_RANK, C_KV_RANK, C_NOPE_DIM, C_ROPE_DIM, C_V_DIM = 8, 768, 256, 64, 32, 64
D_HEADS, D_KV_HEADS = 8, 2
WINDOW = 128
N_EXPERTS = 8
LOG2E = 1.4426950408889634

LANES = 128
BF16_SUBLANES = 16
VMEM_LIMIT = 48 * 2**20
VMEM_LIMIT_BIG = 56 * 2**20

TM = 512
TM_X = 1024
TN = 512
TR = 256
TQ = 512
TQ_WIN = 256
TC = 256
DMA_UNROLL = 8

OFF_QA, OFF_QB, OFF_QC, OFF_QD = 0, 512, 1024, 1792
OFF_KA, OFF_VA, OFF_KB, OFF_VB = 2304, 2816, 3328, 3456
OFF_CKV, OFF_KR, OFF_KD, OFF_VD = 3584, 3840, 3872, 4000
IN_COLS = 4128


def _cparams(sem, vmem=VMEM_LIMIT):
    return pltpu.CompilerParams(dimension_semantics=sem, vmem_limit_bytes=vmem)


def _pick(n, prefs):
    for p in prefs:
        if n % p == 0:
            return p
    return n


def _row_tile(n_rows, s_len, prefs):
    for p in prefs:
        if n_rows % p == 0 and s_len % p == 0:
            return p
    raise ValueError("no row tile fits")


def _pack_bf16_pairs(x):
    w = x.shape[1] // 2
    xb = x.astype(BF16).astype(F32)
    return (pltpu.bitcast(xb[:, :w], jnp.uint32) >> 16) | pltpu.bitcast(xb[:, w:], jnp.uint32)


def _unpack_bf16_pairs(words):
    return (pltpu.bitcast(words << 16, F32), pltpu.bitcast(words & jnp.uint32(0xFFFF0000), F32))


def _rms_modulate(x, g, sc, sh):
    ms = jnp.mean(x * x, axis=-1, keepdims=True)
    return (x * lax.rsqrt(ms + EPS) * g) * (1.0 + sc) + sh


def _mod_kernel(c_ref, w_ref, b_ref, o_ref):
    a = c_ref[...]
    a = (a * jax.nn.sigmoid(a)).astype(BF16)
    o_ref[0] = jnp.dot(a, w_ref[0].astype(BF16), preferred_element_type=F32) + b_ref[0]


def _modulation(cc, w_mod, b_mod):
    depth, d, d6 = w_mod.shape
    rows = cc.shape[0]
    tn = _pick(d6, (1024, 512, 256, 128))
    return pl.pallas_call(
        _mod_kernel,
        out_shape=jax.ShapeDtypeStruct((depth, rows, d6), F32),
        grid=(depth, d6 // tn),
        in_specs=[
            pl.BlockSpec((rows, d), lambda l, n: (0, 0)),
            pl.BlockSpec((1, d, tn), lambda l, n: (l, 0, n)),
            pl.BlockSpec((1, 1, tn), lambda l, n: (l, 0, n)),
        ],
        out_specs=pl.BlockSpec((1, rows, tn), lambda l, n: (l, 0, n)),
        compiler_params=_cparams(("arbitrary", "arbitrary")),
        name="modulation",
    )(cc, w_mod, b_mod.reshape(depth, 1, d6))


def _stacked_specs(parts, block, tile_rows, col_of, n_lead):
    def grid_m(idx):
        return idx[n_lead]

    if len(parts) == 1:
        return [pl.BlockSpec(block, lambda *idx: (grid_m(idx), col_of(idx)))], None
    mt_a = parts[0].shape[0] // tile_rows
    return ([pl.BlockSpec(block, lambda *idx: (jnp.minimum(grid_m(idx), mt_a - 1), col_of(idx))),
             pl.BlockSpec(block, lambda *idx: (jnp.maximum(grid_m(idx) - mt_a, 0), col_of(idx)))], mt_a)


def _nm_mm_kernel(*refs, n_h, mt_a, n_g):
    h_refs = refs[:n_h]
    g_ref, sh_ref, sc_ref = refs[n_h:n_h + 3]
    w_refs, o_ref, u_ref = refs[n_h + 3:n_h + 3 + n_g], refs[n_h + 3 + n_g], refs[n_h + 4 + n_g]
    m = pl.program_id(0)

    def fill(h_ref):
        u_ref[...] = _rms_modulate(h_ref[...], g_ref[...], sc_ref[0], sh_ref[0]).astype(BF16)

    first = pl.program_id(1) == 0
    if n_h == 1:
        pl.when(first)(lambda: fill(h_refs[0]))
    else:
        pl.when(first & (m < mt_a))(lambda: fill(h_refs[0]))
        pl.when(first & (m >= mt_a))(lambda: fill(h_refs[1]))

    u = u_ref[...]
    accs = [jnp.dot(u, w[...].astype(BF16), preferred_element_type=F32) for w in w_refs]
    out = accs[0] if n_g == 1 else (accs[0] * jax.nn.sigmoid(accs[0])) * accs[1]
    o_ref[...] = out.astype(o_ref.dtype)


def _nm_matmul(hs, g, mod, sh_blk, sc_blk, ws, *, n_rows, s_len, mod_row, out_dtype):
    d = hs[0].shape[1]
    n_cols = ws[0].shape[1]
    tm = _row_tile(n_rows, s_len, (TM_X, 512, 256, 128))
    tn = TN
    h_specs, mt_a = _stacked_specs(hs, (tm, d), tm, lambda idx: 0, 0)
    in_specs = h_specs + [
        pl.BlockSpec((1, d), lambda m, n: (0, 0)),
        pl.BlockSpec((1, 1, d), lambda m, n: (mod_row(m * tm), 0, sh_blk)),
        pl.BlockSpec((1, 1, d), lambda m, n: (mod_row(m * tm), 0, sc_blk)),
    ] + [pl.BlockSpec((d, tn), lambda m, n: (0, n)) for _ in ws]
    return pl.pallas_call(
        functools.partial(_nm_mm_kernel, n_h=len(hs), mt_a=mt_a, n_g=len(ws)),
        out_shape=jax.ShapeDtypeStruct((n_rows, n_cols), out_dtype),
        grid=(n_rows // tm, pl.cdiv(n_cols, tn)),
        in_specs=in_specs,
        out_specs=pl.BlockSpec((tm, tn), lambda m, n: (m, n)),
        scratch_shapes=[pltpu.VMEM((tm, d), BF16)],
        compiler_params=_cparams(("arbitrary", "arbitrary"), VMEM_LIMIT_BIG),
        name="normmod_matmul",
    )(*hs, g.reshape(1, d), mod, mod, *ws)


def _final_norm_kernel(x_ref, g_ref, o_ref):
    x = x_ref[...]
    ms = jnp.mean(x * x, axis=-1, keepdims=True)
    o_ref[...] = x * lax.rsqrt(ms + EPS) * g_ref[...]


def _mm_kernel(te_ref, nu_ref, *refs, x_parts, res_parts, mt_a, n_g, epi):
    n_x = len(x_parts)
    refs = list(refs)

    def take(n):
        out = refs[:n]
        del refs[:n]
        return out

    x_refs = [take(p) for p in x_parts]
    w_refs = take(n_x * n_g)
    if epi == "resgate":
        res_refs = take(res_parts)
        gt_ref, = take(1)
    o_ref, = take(1)
    wc_refs = refs
    m = pl.program_id(1)

    def stacked(parts):
        if len(parts) == 1:
            return parts[0][...]
        return jnp.where(m < mt_a, parts[0][...], parts[1][...])

    panel_changed = (m == 0) | (te_ref[m] != te_ref[jnp.maximum(m - 1, 0)])

    @pl.when(panel_changed)
    def _():
        for w, wc in zip(w_refs, wc_refs):
            wc[...] = w[0].astype(BF16)

    @pl.when(m < nu_ref[0])
    def _():
        accs = []
        for g in range(n_g):
            acc = None
            for i in range(n_x):
                part = jnp.dot(stacked(x_refs[i]).astype(BF16), wc_refs[g * n_x + i][...],
                               preferred_element_type=F32)
                acc = part if acc is None else acc + part
            accs.append(acc)
        if epi == "plain":
            out = accs[0]
        elif epi == "packed":
            out = _pack_bf16_pairs(accs[0])
        elif epi == "swiglu":
            a = accs[0]
            out = (a * jax.nn.sigmoid(a)) * accs[1]
        else:
            out = stacked(res_refs) + gt_ref[0] * accs[0]
        o_ref[...] = out.astype(o_ref.dtype)

    @pl.when(m >= nu_ref[0])
    def _():
        o_ref[...] = jnp.zeros_like(o_ref)


def _matmul(xs, ws, *, n_rows, out_dtype, epi="plain", tile_expert=None, n_used=None,
            res=None, mod=None, gt_blk=None, mod_row=None, tm=TM, tn=TN):
    n_x, n_g = len(xs), len(ws)
    n_cols = ws[0].shape[2]
    tn = _pick(n_cols, (tn, 512, 256, 128))
    tm = _pick(n_rows, (tm, 256, 128))
    mt, nt = n_rows // tm, n_cols // tn
    if tile_expert is None:
        tile_expert = jnp.zeros((mt,), jnp.int32)
        n_used = jnp.full((1,), mt, jnp.int32)
    ks = [x[0].shape[1] for x in xs]
    in_specs, args, mt_a = [], [], None
    for x, k in zip(xs, ks):
        if len(x) == 1:
            in_specs.append(pl.BlockSpec((tm, k), lambda n, m, te, nu: (jnp.minimum(m, nu[0] - 1), 0)))
        else:
            specs, mt_a = _stacked_specs(x, (tm, k), tm, lambda idx: 0, 1)
            in_specs += specs
        args += list(x)
    for g in range(n_g):
        for i, k in enumerate(ks):
            in_specs.append(pl.BlockSpec((1, k, tn), lambda n, m, te, nu, i=i: (te[m], i, n)))
            args.append(ws[g])
    if epi == "resgate":
        blk0 = gt_blk * (n_cols // tn)
        specs, mt_res = _stacked_specs(res, (tm, tn), tm, lambda idx: idx[0], 1)
        mt_a = mt_res if mt_res is not None else mt_a
        in_specs += specs
        in_specs.append(pl.BlockSpec((1, 1, tn), lambda n, m, te, nu: (mod_row(m * tm), 0, blk0 + n)))
        args += list(res) + [mod]
    scratch = [pltpu.VMEM((k, tn), BF16) for _ in range(n_g) for k in ks]
    out_div = 2 if epi == "packed" else 1
    return pl.pallas_call(
        functools.partial(_mm_kernel, x_parts=tuple(len(x) for x in xs),
                          res_parts=len(res) if res is not None else 0, mt_a=mt_a, n_g=n_g, epi=epi),
        out_shape=jax.ShapeDtypeStruct((n_rows, n_cols // out_div), out_dtype),
        grid_spec=pltpu.PrefetchScalarGridSpec(
            num_scalar_prefetch=2,
            grid=(nt, mt),
            in_specs=in_specs,
            out_specs=pl.BlockSpec((tm, tn // out_div), lambda n, m, te, nu: (m, n)),
            scratch_shapes=scratch,
        ),
        compiler_params=_cparams(("arbitrary", "arbitrary")),
        name="matmul_" + epi,
    )(tile_expert, n_used, *args)


def _rope(x, cos, sin_signed, half):
    lane = lax.broadcasted_iota(jnp.int32, (x.shape[0], LANES), 1)
    first = (lane & (2 * half - 1)) < half
    blocks = []
    for j in range(x.shape[1] // LANES):
        xj = x[:, j * LANES:(j + 1) * LANES]
        partner = jnp.where(first, pltpu.roll(xj, LANES - half, 1), pltpu.roll(xj, half, 1))
        blocks.append(xj * cos + partner * sin_signed)
    return blocks[0] if len(blocks) == 1 else jnp.concatenate(blocks, axis=-1)


def _group_mean_sq(x, g_ref):
    x2 = x * x
    hi = x2.astype(BF16)
    lo = (x2 - hi.astype(F32)).astype(BF16)
    g = g_ref[...]
    return (jnp.dot(hi, g, preferred_element_type=F32) + jnp.dot(lo, g, preferred_element_type=F32))


def _dup_halves(x):
    lane = lax.broadcasted_iota(jnp.int32, x.shape, 1)
    lo = lane < HEAD_DIM
    swapped = pltpu.roll(x, HEAD_DIM, 1)
    return jnp.concatenate([jnp.where(lo, x, swapped), jnp.where(lo, swapped, x)], axis=-1)


def _store_transposed(dst_ref, x):
    for j in range(x.shape[1] // LANES):
        dst_ref[0, j * LANES:(j + 1) * LANES, :] = x[:, j * LANES:(j + 1) * LANES].T.astype(BF16)


def _proj_prep_kernel(*refs, n_h, mt_a):
    refs = list(refs)
    h_refs = [refs.pop(0) for _ in range(n_h)]
    (g_ref, sh_ref, sc_ref, w_ref, c64_ref, s64_ref, cq_ref, sq_ref, ck_ref, sk_ref,
     bqn_ref, bkn_ref, cqn_ref, ckvn_ref, g512_ref, g128_ref, wq_ref, wkk_ref, wkv_ref, place_ref,
     qa_ref, qb_ref, qc_ref, qd_ref,
     ka_ref, kb_ref, kc_ref, kd_ref, vd_ref, vta_ref, vtb_ref, vtc_ref, u_ref) = refs

    def fill(h_ref):
        u_ref[...] = _rms_modulate(h_ref[...], g_ref[...], sc_ref[0], sh_ref[0]).astype(BF16)

    if n_h == 1:
        fill(h_refs[0])
    else:
        t = pl.program_id(0)
        pl.when(t < mt_a)(lambda: fill(h_refs[0]))
        pl.when(t >= mt_a)(lambda: fill(h_refs[1]))

    def proj(off, width):
        return jnp.dot(u_ref[...], w_ref[:, off:off + width], preferred_element_type=F32)

    c64, s64 = c64_ref[...], s64_ref[...]
    c128, s128 = c64, s64
    scale64 = HEAD_DIM ** -0.5
    scale_mla = (C_NOPE_DIM + C_ROPE_DIM) ** -0.5

    qa_ref[...] = (_rope(proj(OFF_QA, 512), c64, s64, 16) * (scale64 * LOG2E)).astype(BF16)
    ka_ref[0] = _rope(proj(OFF_KA, 512), c64, s64, 16).astype(BF16)
    _store_transposed(vta_ref, proj(OFF_VA, 512))

    xb = proj(OFF_QB, 512)
    yb = xb * lax.rsqrt(_group_mean_sq(xb, g512_ref) + EPS) * bqn_ref[...]
    qb_ref[...] = (_rope(yb, c64, s64, 16) * (scale64 * LOG2E)).astype(BF16)
    kvb = proj(OFF_KB, 256)
    xk = kvb[:, :LANES]
    yk = xk * lax.rsqrt(_group_mean_sq(xk, g128_ref) + EPS) * bkn_ref[...]
    kb_ref[0] = _dup_halves(_rope(yk, c128, s128, 16)).astype(BF16)
    _store_transposed(vtb_ref, kvb[:, LANES:])

    xq = proj(OFF_QC, C_Q_RANK)
    yq = xq * lax.rsqrt(jnp.mean(xq * xq, axis=-1, keepdims=True) + EPS) * cqn_ref[...]
    qf = jnp.dot(yq.astype(BF16), wq_ref[...], preferred_element_type=F32)
    qc_ref[...] = (_rope(qf, cq_ref[...], sq_ref[...], 8) * (scale_mla * LOG2E)).astype(BF16)
    xc = proj(OFF_CKV, C_KV_RANK)
    yc = (xc * lax.rsqrt(jnp.mean(xc * xc, axis=-1, keepdims=True) + EPS) * ckvn_ref[...]).astype(BF16)
    tail = proj(OFF_KR, IN_COLS - OFF_KR)
    kr = _rope(tail[:, :LANES], ck_ref[...], sk_ref[...], 8).astype(BF16)
    kc = (jnp.dot(yc, wkk_ref[...], preferred_element_type=F32)
          + jnp.dot(kr, place_ref[...], preferred_element_type=F32))
    kc_ref[0] = kc.astype(BF16)
    _store_transposed(vtc_ref, jnp.dot(yc, wkv_ref[...], preferred_element_type=F32))

    qd_ref[...] = (_rope(proj(OFF_QD, 512), c64, s64, 16) * scale64).astype(BF16)
    kd0, vd0 = OFF_KD - OFF_KR, OFF_VD - OFF_KR
    kd_ref[0] = _dup_halves(_rope(tail[:, kd0:kd0 + LANES], c128, s128, 16)).astype(BF16)
    vd_ref[0] = _dup_halves(tail[:, vd0:vd0 + LANES]).astype(BF16)


def _proj_prep(hs, g, mod, w_in_bf16, tables, consts, dims, mod_row):
    bsz, s_len, n_ctx = dims
    d = hs[0].shape[1]
    n_rows = sum(a.shape[0] for a in hs)
    sk = n_ctx + s_len
    tr = _pick(math.gcd(s_len, n_ctx), (TR, 128))
    n_lat_t, lat_pb, ctx_pb = bsz * s_len // tr, s_len // tr, n_ctx // tr

    def is_lat(t):
        return t < n_lat_t

    def tbl_idx(t):
        return jnp.where(is_lat(t), t % lat_pb, lat_pb)

    def kv_b(t):
        return jnp.where(is_lat(t), t // lat_pb, (t - n_lat_t) // ctx_pb)

    def kv_j(t):
        return jnp.where(is_lat(t), ctx_pb + t % lat_pb, (t - n_lat_t) % ctx_pb)

    def row_spec(w):
        return pl.BlockSpec((tr, w), lambda t: (t, 0))

    def tbl_spec(w):
        return pl.BlockSpec((tr, w), lambda t: (tbl_idx(t), 0))

    def const_spec(a):
        return pl.BlockSpec(a.shape, lambda t: (0,) * a.ndim)

    def kv_spec(w):
        return pl.BlockSpec((1, tr, w), lambda t: (kv_b(t), kv_j(t), 0))

    def vt_spec(w):
        return pl.BlockSpec((1, w, tr), lambda t: (kv_b(t), 0, kv_j(t)))

    q_widths = (512, 512, 1024, 512)
    kv_widths = (512, 256, 1024, 256, 256)
    vt_widths = (512, 128, 512)
    out_shape = ([jax.ShapeDtypeStruct((n_rows, w), BF16) for w in q_widths]
                 + [jax.ShapeDtypeStruct((bsz, sk, w), BF16) for w in kv_widths]
                 + [jax.ShapeDtypeStruct((bsz, w, sk), BF16) for w in vt_widths])
    out_specs = ([row_spec(w) for w in q_widths] + [kv_spec(w) for w in kv_widths]
                 + [vt_spec(w) for w in vt_widths])
    h_specs, mt_a = _stacked_specs(hs, (tr, d), tr, lambda idx: 0, 0)
    in_specs = (h_specs
                + [pl.BlockSpec((1, d), lambda t: (0, 0)),
                   pl.BlockSpec((1, 1, d), lambda t: (mod_row(t * tr), 0, 0)),
                   pl.BlockSpec((1, 1, d), lambda t: (mod_row(t * tr), 0, 1)),
                   pl.BlockSpec(w_in_bf16.shape, lambda t: (0, 0), pipeline_mode=pl.Buffered(1))]
                + [tbl_spec(t.shape[1]) for t in tables] + [const_spec(a) for a in consts])
    return pl.pallas_call(
        functools.partial(_proj_prep_kernel, n_h=len(hs), mt_a=mt_a),
        out_shape=out_shape,
        grid=(n_rows // tr,),
        in_specs=in_specs,
        out_specs=out_specs,
        scratch_shapes=[pltpu.VMEM((tr, d), BF16)],
        compiler_params=_cparams(("arbitrary",), VMEM_LIMIT_BIG),
        name="proj_prep",
    )(*hs, g.reshape(1, d), mod, mod, w_in_bf16, *tables, *consts)


def _scores(a, b):
    return lax.dot_general(a, b, (((1,), (1,)), ((), ())), preferred_element_type=F32)


def _half_masks(shape):
    lane = lax.broadcasted_iota(jnp.int32, shape, 1)
    lo = lane < HEAD_DIM
    return lo, jnp.logical_not(lo)


def _softmax_pv_t(s, v_t):
    m = jnp.max(s, axis=0, keepdims=True)
    e = jnp.exp2(s - m).astype(BF16)
    dv = v_t.shape[0]
    v_ext = jnp.concatenate([v_t, jnp.ones((BF16_SUBLANES, v_t.shape[1]), BF16)], axis=0)
    oe = jnp.dot(v_ext, e, preferred_element_type=F32)
    return oe[:dv] / oe[dv:dv + 1]


def _run_units(score_fns, finish_fns):
    s = score_fns[0]()
    for i, finish in enumerate(finish_fns):
        s_next = score_fns[i + 1]() if i + 1 < len(score_fns) else None
        finish(s)
        s = s_next


def _gqa_t_kernel(q_ref, k_ref, vt_ref, o_ref):
    masks = _half_masks((q_ref.shape[0], LANES))
    n_pairs = q_ref.shape[1] // LANES
    pairs_per_group = B_HEADS // B_KV_HEADS // 2
    outs = {}

    def score(j, z):
        g = j // pairs_per_group
        q = q_ref[:, j * LANES:(j + 1) * LANES]
        return lambda: _scores(k_ref[0, :, g * LANES:(g + 1) * LANES],
                               jnp.where(masks[z], q, jnp.zeros_like(q)))

    def finish(j, z):
        g = j // pairs_per_group

        def fin(s):
            outs[z] = _softmax_pv_t(s, vt_ref[0, g * HEAD_DIM:(g + 1) * HEAD_DIM, :])
            if z == 1:
                pair = jnp.concatenate([outs[0], outs[1]], axis=0)
                o_ref[:, j * LANES:(j + 1) * LANES] = pair.T.astype(o_ref.dtype)
        return fin

    units = [(j, z) for j in range(n_pairs) for z in range(2)]
    _run_units([score(j, z) for j, z in units], [finish(j, z) for j, z in units])


def _mla_t_kernel(q_ref, k_ref, vt_ref, o_ref):
    n_heads = q_ref.shape[1] // LANES
    outs = {}

    def score(u):
        return lambda: _scores(k_ref[0, :, u * LANES:(u + 1) * LANES], q_ref[:, u * LANES:(u + 1) * LANES])

    def finish(u):
        def fin(s):
            outs[u % 2] = _softmax_pv_t(s, vt_ref[0, u * C_V_DIM:(u + 1) * C_V_DIM, :])
            if u % 2 == 1:
                pair = jnp.concatenate([outs[0], outs[1]], axis=0)
                j = u // 2
                o_ref[:, j * LANES:(j + 1) * LANES] = pair.T.astype(o_ref.dtype)
        return fin

    _run_units([score(u) for u in range(n_heads)], [finish(u) for u in range(n_heads)])


def _diff_t_kernel(lam_ref, subln_ref, q0_ref, q1_ref, k0_ref, k1_ref, vt_ref, o_ref, *, lam_init):
    t = lam_ref[...]
    lam = (jnp.exp(jnp.sum(t[0:1] * t[1:2], axis=-1, keepdims=True))
           - jnp.exp(jnp.sum(t[2:3] * t[3:4], axis=-1, keepdims=True)) + lam_init)
    q_refs, k_refs = (q0_ref, q1_ref), (k0_ref, k1_ref)
    masks = _half_masks((q0_ref.shape[0], LANES))
    first_map = {}

    def score(z, mp):
        j = z // 2
        q = q_refs[mp][:, j * LANES:(j + 1) * LANES]
        return lambda: _scores(k_refs[mp][0, :, j * LANES:(j + 1) * LANES],
                               jnp.where(masks[z % 2], q, jnp.zeros_like(q)))

    def finish(z, mp):
        def fin(s):
            a = _softmax_pv_t(s, vt_ref[0, z * A_V_DIM:(z + 1) * A_V_DIM, :])
            if mp == 0:
                first_map[z] = a
                return
            d = (first_map[z] - lam * a).T
            y = d * lax.rsqrt(jnp.mean(d * d, axis=-1, keepdims=True) + EPS) * subln_ref[...]
            o_ref[:, z * A_V_DIM:(z + 1) * A_V_DIM] = (y * (1.0 - lam_init)).astype(o_ref.dtype)
        return fin

    units = [(z, mp) for z in range(A_HEADS) for mp in range(2)]
    _run_units([score(z, mp) for z, mp in units], [finish(z, mp) for z, mp in units])


def _attend_sink(q, k, v, sink):
    s = _scores(q, k)
    m = jnp.maximum(jnp.max(s, axis=-1, keepdims=True), sink)
    e = jnp.exp(s - m)
    l = jnp.sum(e, axis=-1, keepdims=True) + jnp.exp(sink - m)
    return jnp.dot(e.astype(BF16), v, preferred_element_type=F32) / l


def _window_kernel(sink_ref, q_ref, k_ref, v_ref, o_ref, *, s_len, n_ctx, n_q_lat, with_ctx):
    tq = q_ref.shape[0]
    band = tq + 2 * WINDOW
    i = pl.program_id(1)
    pairs_per_group = D_HEADS // D_KV_HEADS // 2

    def body(is_lat):
        masks = _half_masks((tq, LANES))
        if is_lat:
            start = pl.multiple_of(jnp.clip(i * tq - WINDOW, 0, s_len - band), LANES)
            row0 = pl.multiple_of(n_ctx + start, LANES)
            qpos = i * tq + lax.broadcasted_iota(jnp.int32, (tq, band), 0)
            kpos = start + lax.broadcasted_iota(jnp.int32, (tq, band), 1)
            in_band = jnp.abs(qpos - kpos) <= WINDOW
        for j in range(q_ref.shape[1] // LANES):
            g = j // pairs_per_group
            lanes = slice(g * LANES, (g + 1) * LANES)
            k_c, v_c = k_ref[0, :n_ctx, lanes], v_ref[0, :n_ctx, lanes]
            if is_lat:
                k_b = k_ref[0, pl.ds(row0, band), lanes]
                v_b = v_ref[0, pl.ds(row0, band), lanes]
            q = q_ref[:, j * LANES:(j + 1) * LANES]
            outs = []
            for z in range(2):
                qz = jnp.where(masks[z], q, jnp.zeros_like(q))
                sink = sink_ref[2 * j + z]
                if not is_lat:
                    outs.append(_attend_sink(qz, k_c, v_c, sink))
                    continue
                s_c = _scores(qz, k_c)
                s_b = jnp.where(in_band, _scores(qz, k_b), -jnp.inf)
                m = jnp.maximum(jnp.maximum(jnp.max(s_c, axis=-1, keepdims=True),
                                            jnp.max(s_b, axis=-1, keepdims=True)), sink)
                e_c, e_b = jnp.exp(s_c - m), jnp.exp(s_b - m)
                l = (jnp.sum(e_c, axis=-1, keepdims=True) + jnp.sum(e_b, axis=-1, keepdims=True)
                     + jnp.exp(sink - m))
                o = (jnp.dot(e_c.astype(BF16), v_c, preferred_element_type=F32)
                     + jnp.dot(e_b.astype(BF16), v_b, preferred_element_type=F32))
                outs.append(o / l)
            o_ref[:, j * LANES:(j + 1) * LANES] = jnp.where(masks[0], outs[0], outs[1]).astype(o_ref.dtype)

    if not with_ctx:
        body(True)
        return

    @pl.when(i < n_q_lat)
    def _():
        body(True)

    @pl.when(i >= n_q_lat)
    def _():
        body(False)


def _mixers(ops, dims, params, lam_init, with_ctx):
    bsz, s_len, n_ctx = dims
    qa, qb, qc, qd, ka, kb, kc, kd, vd, vta, vtb, vtc = ops
    lam_vecs, subln, sink = params
    n_lat = bsz * s_len
    sk_all = n_ctx + s_len

    def const_spec(a):
        return pl.BlockSpec(a.shape, lambda b, i: (0,) * a.ndim)

    def dense_set(tq, nk, row0_tiles, n_q, out_rows):
        def q_row(b, i):
            return row0_tiles + b * n_q + i

        def qspec(w, col=0):
            return pl.BlockSpec((tq, w), lambda b, i: (q_row(b, i), col))

        def kspec(w, col=0):
            return pl.BlockSpec((1, nk, w), lambda b, i: (b, 0, col))

        def vtspec(w):
            return pl.BlockSpec((1, w, nk), lambda b, i: (b, 0, 0))

        def call(kernel, in_specs, args, name):
            return pl.pallas_call(
                kernel,
                out_shape=jax.ShapeDtypeStruct((out_rows, 512), BF16),
                grid=(bsz, n_q),
                in_specs=in_specs,
                out_specs=pl.BlockSpec((tq, 512), lambda b, i: (b * n_q + i, 0)),
                compiler_params=_cparams(("arbitrary", "arbitrary"), VMEM_LIMIT_BIG),
                name=name,
            )(*args)

        oa = call(functools.partial(_diff_t_kernel, lam_init=lam_init),
                  [const_spec(lam_vecs), const_spec(subln), qspec(256, 0), qspec(256, 1),
                   kspec(256, 0), kspec(256, 1), vtspec(512)],
                  [lam_vecs, subln, qa, qa, ka, ka, vta], "attn_diff")
        ob = call(_gqa_t_kernel, [qspec(512), kspec(256), vtspec(128)], [qb, kb, vtb], "attn_qknorm")
        oc = call(_mla_t_kernel, [qspec(1024), kspec(1024), vtspec(512)], [qc, kc, vtc], "attn_mla")
        return [oa, ob, oc]

    tq = _pick(s_len, (TQ, 256, 128))
    dense = [[o] for o in dense_set(tq, sk_all, 0, s_len // tq, n_lat)]
    if with_ctx:
        tq_c = _pick(n_ctx, (TQ, 256, 128))
        ctx = dense_set(tq_c, n_ctx, n_lat // tq_c, n_ctx // tq_c, bsz * n_ctx)
        dense = [a + [b] for a, b in zip(dense, ctx)]

    tw = _pick(math.gcd(s_len, n_ctx), (TQ_WIN, 128))
    n_q_lat, n_q_ctx = s_len // tw, n_ctx // tw
    n_q = n_q_lat + n_q_ctx if with_ctx else n_q_lat
    out_rows = n_lat + bsz * n_ctx if with_ctx else n_lat

    def w_row(b, i):
        return jnp.where(i < n_q_lat, b * n_q_lat + i, n_lat // tw + b * n_q_ctx + (i - n_q_lat))

    od = pl.pallas_call(
        functools.partial(_window_kernel, s_len=s_len, n_ctx=n_ctx, n_q_lat=n_q_lat, with_ctx=with_ctx),
        out_shape=jax.ShapeDtypeStruct((out_rows, 512), BF16),
        grid=(bsz, n_q),
        in_specs=[pl.BlockSpec(memory_space=pltpu.SMEM),
                  pl.BlockSpec((tw, 512), lambda b, i: (w_row(b, i), 0)),
                  pl.BlockSpec((1, sk_all, 256), lambda b, i: (b, 0, 0)),
                  pl.BlockSpec((1, sk_all, 256), lambda b, i: (b, 0, 0))],
        out_specs=pl.BlockSpec((tw, 512), lambda b, i: (w_row(b, i), 0)),
        compiler_params=_cparams(("arbitrary", "arbitrary")),
        name="attn_window",
    )(sink, qd, kd, vd)
    return dense + [[od]]


def _route_kernel(h_ref, g_ref, sh_ref, sc_ref, w_ref, b_ref, u_ref, sel_ref, idx_ref, gw_ref):
    u = _rms_modulate(h_ref[...], g_ref[...], sc_ref[0], sh_ref[0])
    u_ref[...] = _pack_bf16_pairs(u)
    w = w_ref[...]
    u_hi = u.astype(BF16)
    u_lo = (u - u_hi.astype(F32)).astype(BF16)
    w_hi = w.astype(BF16)
    w_lo = (w - w_hi.astype(F32)).astype(BF16)
    logits = (_scores(w_hi, u_hi) + _scores(w_hi, u_lo) + _scores(w_lo, u_hi) + _scores(w_lo, u_lo)
              + b_ref[...])
    ids = lax.broadcasted_iota(jnp.int32, logits.shape, 0).astype(F32)
    m1 = jnp.max(logits, axis=0, keepdims=True)
    i1 = jnp.min(jnp.where(logits == m1, ids, float(N_EXPERTS)), axis=0, keepdims=True)
    first = ids == i1
    rest = jnp.where(first, -jnp.inf, logits)
    m2 = jnp.max(rest, axis=0, keepdims=True)
    i2 = jnp.min(jnp.where(rest == m2, ids, float(N_EXPERTS)), axis=0, keepdims=True)
    second = ids == i2
    e = jnp.exp(m2 - m1)
    w1 = 1.0 / (1.0 + e)
    w2 = e / (1.0 + e)
    sel_ref[...] = jnp.where(first | second, 1, 0).astype(jnp.int32)
    idx_ref[...] = jnp.where(ids == 0.0, i1, jnp.where(ids == 1.0, i2, 0.0)).astype(jnp.int32)
    gw_ref[...] = jnp.where(ids == 0.0, w1, jnp.where(ids == 1.0, w2, 0.0))


def _route(h, g, mod, sh_blk, sc_blk, w_router_t, b_router, n_tok, s_len, mod_row):
    d = h.shape[1]
    tm = _row_tile(n_tok, s_len, (TM, 256, 128))
    outs = ([jax.ShapeDtypeStruct((n_tok, d // 2), jnp.uint32)]
            + [jax.ShapeDtypeStruct((N_EXPERTS, n_tok), dt) for dt in (jnp.int32, jnp.int32, F32)])
    return pl.pallas_call(
        _route_kernel,
        out_shape=outs,
        grid=(n_tok // tm,),
        in_specs=[pl.BlockSpec((tm, d), lambda m: (m, 0)),
                  pl.BlockSpec((1, d), lambda m: (0, 0)),
                  pl.BlockSpec((1, 1, d), lambda m: (mod_row(m * tm), 0, sh_blk)),
                  pl.BlockSpec((1, 1, d), lambda m: (mod_row(m * tm), 0, sc_blk)),
                  pl.BlockSpec((N_EXPERTS, d), lambda m: (0, 0)),
                  pl.BlockSpec((N_EXPERTS, 1), lambda m: (0, 0))],
        out_specs=[pl.BlockSpec((tm, d // 2), lambda m: (m, 0))]
        + [pl.BlockSpec((N_EXPERTS, tm), lambda m: (0, m))] * 3,
        compiler_params=_cparams(("arbitrary",)),
        name="moe_route",
    )(h, g.reshape(1, d), mod, mod, w_router_t, b_router.reshape(N_EXPERTS, 1))


def _row_copy(src, dst, sem, src_row, dst_row):
    return pltpu.make_async_copy(src.at[pl.ds(src_row, 1)], dst.at[pl.ds(dst_row, 1)], sem)


def _gather_kernel(tok_ref, nxt_ref, src_ref, o_ref, buf, sem):
    i = pl.program_id(0)
    n = pl.num_programs(0)
    tg = buf.shape[1]
    slot = i % 2

    def issue(t_ref, s):
        def body(r, carry):
            _row_copy(src_ref, buf.at[s], sem.at[s], t_ref[0, 0, r], r).start()
            return carry
        lax.fori_loop(0, tg, body, 0, unroll=DMA_UNROLL)

    @pl.when(i == 0)
    def _():
        issue(tok_ref, 0)

    @pl.when(i + 1 < n)
    def _():
        issue(nxt_ref, 1 - slot)

    def wait(r, carry):
        _row_copy(src_ref, buf.at[slot], sem.at[slot], 0, r).wait()
        return carry

    lax.fori_loop(0, tg, wait, 0, unroll=DMA_UNROLL)
    lo, hi = _unpack_bf16_pairs(buf[slot])
    half = lo.shape[1]
    o_ref[:, :half] = lo.astype(o_ref.dtype)
    o_ref[:, half:] = hi.astype(o_ref.dtype)


def _gather_rows(src, tok, tg):
    n_rows = tok.shape[0]
    dw = src.shape[1]
    d = 2 * dw
    n_t = n_rows // tg
    tok = tok.reshape(n_t, 1, tg)
    return pl.pallas_call(
        _gather_kernel,
        out_shape=jax.ShapeDtypeStruct((n_rows, d), BF16),
        grid=(n_t,),
        in_specs=[pl.BlockSpec((1, 1, tg), lambda i: (i, 0, 0), memory_space=pltpu.SMEM),
                  pl.BlockSpec((1, 1, tg), lambda i: (jnp.minimum(i + 1, n_t - 1), 0, 0),
                               memory_space=pltpu.SMEM),
                  pl.BlockSpec(memory_space=pl.ANY)],
        out_specs=pl.BlockSpec((tg, d), lambda i: (i, 0)),
        scratch_shapes=[pltpu.VMEM((2, tg, dw), jnp.uint32), pltpu.SemaphoreType.DMA((2,))],
        compiler_params=_cparams(("arbitrary",)),
        name="moe_gather",
    )(tok, tok, src)


def _combine_kernel(pos_ref, nxt_ref, y_ref, h_ref, gw_ref, gt_ref, gf_ref, o_ref, buf, sem, *, pack_tile):
    i = pl.program_id(0)
    n = pl.num_programs(0)
    tc = buf.shape[2]
    slot = i % 2

    def issue(p_ref, s):
        def body(r, carry):
            _row_copy(y_ref, buf.at[s, 0], sem.at[s], p_ref[0, 0, r], r).start()
            _row_copy(y_ref, buf.at[s, 1], sem.at[s], p_ref[0, 1, r], r).start()
            return carry
        lax.fori_loop(0, tc, body, 0, unroll=DMA_UNROLL)

    @pl.when(i == 0)
    def _():
        issue(pos_ref, 0)

    @pl.when(i + 1 < n)
    def _():
        issue(nxt_ref, 1 - slot)

    def wait(r, carry):
        _row_copy(y_ref, buf.at[slot, 0], sem.at[slot], 0, r).wait()
        _row_copy(y_ref, buf.at[slot, 1], sem.at[slot], 0, r).wait()
        return carry

    lax.fori_loop(0, tc, wait, 0, unroll=DMA_UNROLL)
    def expert_rows(k):
        lo, hi = _unpack_bf16_pairs(buf[slot, k])
        hw = pack_tile // 2
        cols = []
        for j in range(lo.shape[1] // hw):
            cols += [lo[:, j * hw:(j + 1) * hw], hi[:, j * hw:(j + 1) * hw]]
        return jnp.concatenate(cols, axis=-1)

    gw = gw_ref[...]
    moe = gw[:, 0:1] * expert_rows(0) + gw[:, 1:2] * expert_rows(1)
    x = h_ref[...] + gt_ref[0] * moe
    ms = jnp.mean(x * x, axis=-1, keepdims=True)
    o_ref[...] = x * lax.rsqrt(ms + EPS) * gf_ref[...]


def _combine(y, h, pos, gw_t, mod, gt_blk, mod_row, g_final, n_tok, pack_tile):
    d = h.shape[1]
    n_t, _, tc = pos.shape
    return pl.pallas_call(
        functools.partial(_combine_kernel, pack_tile=pack_tile),
        out_shape=jax.ShapeDtypeStruct((n_tok, d), F32),
        grid=(n_t,),
        in_specs=[pl.BlockSpec((1, 2, tc), lambda i: (i, 0, 0), memory_space=pltpu.SMEM),
                  pl.BlockSpec((1, 2, tc), lambda i: (jnp.minimum(i + 1, n_t - 1), 0, 0),
                               memory_space=pltpu.SMEM),
                  pl.BlockSpec(memory_space=pl.ANY),
                  pl.BlockSpec((tc, d), lambda i: (i, 0)),
                  pl.BlockSpec((tc, N_EXPERTS), lambda i: (i, 0)),
                  pl.BlockSpec((1, 1, d), lambda i: (mod_row(i * tc), 0, gt_blk)),
                  pl.BlockSpec((1, d), lambda i: (0, 0))],
        out_specs=pl.BlockSpec((tc, d), lambda i: (i, 0)),
        scratch_shapes=[pltpu.VMEM((2, 2, tc, d // 2), jnp.uint32), pltpu.SemaphoreType.DMA((2,))],
        compiler_params=_cparams(("arbitrary",)),
        name="moe_combine",
    )(pos, pos, y, h, gw_t, mod, g_final.reshape(1, d))


def _dispatch_plan(sel, idx, tm, n_slots, tc):
    n_tok = sel.shape[1]
    n_tiles = n_slots // tm
    counts = jnp.sum(sel, axis=1)
    padded = ((counts + tm - 1) // tm) * tm
    ends = jnp.cumsum(padded)
    offs = ends - padded
    pos = offs[:, None] + jnp.cumsum(sel, axis=1) - sel
    pos0 = jnp.take_along_axis(pos, idx[0:1], axis=0)[0]
    pos1 = jnp.take_along_axis(pos, idx[1:2], axis=0)[0]
    pos_tiles = jnp.stack([pos0.reshape(n_tok // tc, tc), pos1.reshape(n_tok // tc, tc)], axis=1)
    tile_start = jnp.arange(n_tiles, dtype=jnp.int32) * tm
    tile_expert = jnp.minimum(jnp.sum(ends[None, :] <= tile_start[:, None], axis=1), N_EXPERTS - 1)
    n_used = ends[-1] // tm
    tok = jnp.arange(n_tok, dtype=jnp.int32)
    tok_of_row = jnp.zeros((n_slots,), jnp.int32).at[jnp.concatenate([pos0, pos1])].set(
        jnp.concatenate([tok, tok]), unique_indices=True)
    return (pos_tiles.astype(jnp.int32), tok_of_row, tile_expert.astype(jnp.int32),
            n_used.reshape(1).astype(jnp.int32))


def _rope_tables(s_len, pad_rows):
    t = jnp.arange(s_len)
    rows, cols = (t // GRID_W).astype(F32), (t % GRID_W).astype(F32)

    def axis_tables(rot_dim):
        axis_dim = rot_dim // 2
        inv = ROPE_THETA ** (-jnp.arange(0, axis_dim, 2, dtype=F32) / axis_dim)
        ar, ac = rows[:, None] * inv[None, :], cols[:, None] * inv[None, :]
        cos = jnp.concatenate([jnp.cos(ar), jnp.cos(ar), jnp.cos(ac), jnp.cos(ac)], axis=1)
        sin = jnp.concatenate([-jnp.sin(ar), jnp.sin(ar), -jnp.sin(ac), jnp.sin(ac)], axis=1)
        return cos, sin

    def with_identity(cos, sin):
        w = cos.shape[1]
        return (jnp.concatenate([cos, jnp.ones((pad_rows, w), F32)], axis=0),
                jnp.concatenate([sin, jnp.zeros((pad_rows, w), F32)], axis=0))

    c64, s64 = axis_tables(HEAD_DIM)
    c64, s64 = jnp.tile(c64, (1, 2)), jnp.tile(s64, (1, 2))
    c32, s32 = axis_tables(C_ROPE_DIM)
    ones, zeros = jnp.ones((s_len, 1), F32), jnp.zeros((s_len, 1), F32)
    cq = jnp.concatenate([jnp.tile(ones, (1, 64)), c32, jnp.tile(ones, (1, 32))], axis=1)
    sq = jnp.concatenate([jnp.tile(zeros, (1, 64)), s32, jnp.tile(zeros, (1, 32))], axis=1)
    ck = jnp.concatenate([c32, jnp.tile(ones, (1, 96))], axis=1)
    sk = jnp.concatenate([s32, jnp.tile(zeros, (1, 96))], axis=1)
    out = []
    for c, s in ((c64, s64), (cq, sq), (ck, sk)):
        out += list(with_identity(c, s))
    return out


def _mla_weights(w_q_up, w_kv_up):
    qd = C_NOPE_DIM + C_ROPE_DIM
    wq = jnp.pad(w_q_up.reshape(C_Q_RANK, C_HEADS, qd), ((0, 0), (0, 0), (0, LANES - qd)))
    wkv = w_kv_up.reshape(C_KV_RANK, C_HEADS, C_NOPE_DIM + C_V_DIM)
    wkk = jnp.pad(wkv[:, :, :C_NOPE_DIM], ((0, 0), (0, 0), (0, LANES - C_NOPE_DIM)))
    wkv_v = wkv[:, :, C_NOPE_DIM:]
    return (wq.reshape(C_Q_RANK, C_HEADS * LANES).astype(BF16),
            wkk.reshape(C_KV_RANK, C_HEADS * LANES).astype(BF16),
            wkv_v.reshape(C_KV_RANK, C_HEADS * C_V_DIM).astype(BF16))


def _static_mats():
    g = (np.arange(512)[:, None] // HEAD_DIM == np.arange(512)[None, :] // HEAD_DIM) / HEAD_DIM
    place = np.zeros((LANES, C_HEADS * LANES), np.float32)
    for h in range(C_HEADS):
        place[np.arange(C_ROPE_DIM), h * LANES + C_NOPE_DIM + np.arange(C_ROPE_DIM)] = 1.0
    return (jnp.asarray(g, BF16), jnp.asarray(g[:128, :128], BF16), jnp.asarray(place, BF16))


def kernel(x, c, ctx, c_ctx, w_mod, b_mod, g_mix, g_ffn, g_final, w_in, w_out, a_lam_q1, a_lam_k1, a_lam_q2, a_lam_k2, a_subln, b_q_norm, b_k_norm, c_q_norm, c_kv_norm, c_w_q_up, c_w_kv_up, d_sink, ffn_w_gate, ffn_w_up, ffn_w_down, moe_w_router, moe_b_router, moe_w_gate, moe_w_up, moe_w_down):
    bsz, s_len, d = x.shape
    n_ctx = ctx.shape[1]
    depth = w_mod.shape[0]
    n_lat = bsz * s_len
    n_all = n_lat + bsz * n_ctx
    dims = (bsz, s_len, n_ctx)

    def mod_row(row0):
        return jnp.where(row0 < n_lat, row0 // s_len, bsz)

    mod_rows = -(-(bsz + 1) // 8) * 8
    cc = jnp.zeros((mod_rows, d), F32).at[:bsz].set(c).at[bsz].set(c_ctx)
    mods = _modulation(cc, w_mod, b_mod)

    tr = _pick(math.gcd(s_len, n_ctx), (TR, 128))
    tables = _rope_tables(s_len, tr)
    g512, g128, place = _static_mats()

    h = [x.reshape(n_lat, d), ctx.reshape(bsz * n_ctx, d)]
    for l in range(depth):
        last = l == depth - 1
        lam_init = 0.8 - 0.6 * math.exp(-0.3 * l)
        n_rows = n_lat if last else n_all
        mod = mods[l].reshape(mod_rows, 1, 6 * d)
        nm = dict(s_len=s_len, mod_row=mod_row)

        wq, wkk, wkv_v = _mla_weights(c_w_q_up[l], c_w_kv_up[l])
        consts = [jnp.tile(b_q_norm[l], 8)[None], jnp.tile(b_k_norm[l], 2)[None],
                  c_q_norm[l][None], c_kv_norm[l][None], g512, g128, wq, wkk, wkv_v, place]
        ops = _proj_prep(h, g_mix[l], mod, w_in[l].astype(BF16), tables, consts, dims, mod_row)
        lam_vecs = jnp.stack([a_lam_q1[l], a_lam_k1[l], a_lam_q2[l], a_lam_k2[l]])
        mix = _mixers(ops, dims, (lam_vecs, a_subln[l][None], d_sink[l]), lam_init, not last)
        h = [_matmul(mix, [w_out[l][None]], n_rows=n_rows, out_dtype=F32, epi="resgate",
                     res=h, mod=mod, gt_blk=2, mod_row=mod_row, tn=1024)]

        i = l // 2
        if l % 2 == 0:
            mid = _nm_matmul(h, g_ffn[l], mod, 3, 4, [ffn_w_gate[i], ffn_w_up[i]], n_rows=n_rows,
                             out_dtype=BF16, **nm)
            h = [_matmul([[mid]], [ffn_w_down[i][None]], n_rows=n_rows, out_dtype=F32, epi="resgate",
                         res=h, mod=mod, gt_blk=5, mod_row=mod_row)]
        else:
            if not last:
                raise NotImplementedError("expert layers are only supported as the last layer")
            h = h[0]
            u2, sel, idx, gw = _route(h, g_ffn[l], mod, 3, 4, moe_w_router[i].T, moe_b_router[i],
                                      n_rows, s_len, mod_row)
            tm = _pick(n_rows, (TM, 256, 128))
            tc = _pick(s_len, (TC, 128))
            n_slots = 2 * n_rows + N_EXPERTS * tm
            pos, tok_of_row, tile_expert, n_used = _dispatch_plan(sel, idx, tm, n_slots, tc)
            xs = _gather_rows(u2, tok_of_row, tm)
            mid = _matmul([[xs]], [moe_w_gate[i], moe_w_up[i]], n_rows=n_slots, out_dtype=BF16,
                          epi="swiglu", tile_expert=tile_expert, n_used=n_used)
            y = _matmul([[mid]], [moe_w_down[i]], n_rows=n_slots, out_dtype=jnp.uint32, epi="packed",
                        tile_expert=tile_expert, n_used=n_used)
            out = _combine(y, h, pos, gw.T, mod, 5, mod_row, g_final, n_rows, _pick(d, (TN, 256, 128)))
            return out.reshape(bsz, s_len, d)

    tm = _pick(n_lat, (TM, 256, 128))
    out = pl.pallas_call(
        _final_norm_kernel,
        out_shape=jax.ShapeDtypeStruct((n_lat, d), F32),
        grid=(n_lat // tm,),
        in_specs=[pl.BlockSpec((tm, d), lambda m: (m, 0)), pl.BlockSpec((1, d), lambda m: (0, 0))],
        out_specs=pl.BlockSpec((tm, d), lambda m: (m, 0)),
        compiler_params=_cparams(("arbitrary",)),
        name="final_norm",
    )(h[0], g_final.reshape(1, d))
    return out.reshape(bsz, s_len, d)
```

```python
import functools
import math

import numpy as np
import jax
import jax.numpy as jnp
from jax import lax
from jax.experimental import pallas as pl
from jax.experimental.pallas import tpu as pltpu

F32 = jnp.float32
BF16 = jnp.bfloat16

GRID_W = 64
HEAD_DIM = 64
ROPE_THETA = 10000.0
EPS = 1e-6
A_HEADS, A_QK_DIM, A_V_DIM = 4, 64, 128
B_HEADS, B_KV_HEADS = 8, 2
C_HEADS, C_Q_RANK, C_KV_RANK, C_NOPE_DIM, C_ROPE_DIM, C_V_DIM = 8, 768, 256, 64, 32, 64
D_HEADS, D_KV_HEADS = 8, 2
WINDOW = 128
N_EXPERTS = 8
LOG2E = 1.4426950408889634

LANES = 128
BF16_SUBLANES = 16
VMEM_LIMIT = 48 * 2**20
VMEM_LIMIT_BIG = 56 * 2**20

TM = 512
TM_X = 1024
TN = 512
TR = 256
TQ = 512
TQ_WIN = 256
TC = 256
DMA_UNROLL = 8

OFF_QA, OFF_QB, OFF_QC, OFF_QD = 0, 512, 1024, 1792
OFF_KA, OFF_VA, OFF_KB, OFF_VB = 2304, 2816, 3328, 3456
OFF_CKV, OFF_KR, OFF_KD, OFF_VD = 3584, 3840, 3872, 4000
IN_COLS = 4128


def _cparams(sem, vmem=VMEM_LIMIT):
    return pltpu.CompilerParams(dimension_semantics=sem, vmem_limit_bytes=vmem)


def _pick(n, prefs):
    for p in prefs:
        if n % p == 0:
            return p
    return n


def _row_tile(n_rows, s_len, prefs):
    for p in prefs:
        if n_rows % p == 0 and s_len % p == 0:
            return p
    raise ValueError("no row tile fits")


def _pack_bf16_pairs(x):
    w = x.shape[1] // 2
    xb = x.astype(BF16).astype(F32)
    return (pltpu.bitcast(xb[:, :w], jnp.uint32) >> 16) | pltpu.bitcast(xb[:, w:], jnp.uint32)


def _unpack_bf16_pairs(words):
    return (pltpu.bitcast(words << 16, F32), pltpu.bitcast(words & jnp.uint32(0xFFFF0000), F32))


def _rms_modulate(x, g, sc, sh):
    ms = jnp.mean(x * x, axis=-1, keepdims=True)
    return (x * lax.rsqrt(ms + EPS) * g) * (1.0 + sc) + sh


def _mod_kernel(c_ref, w_ref, b_ref, o_ref):
    a = c_ref[...]
    a = (a * jax.nn.sigmoid(a)).astype(BF16)
    o_ref[0] = jnp.dot(a, w_ref[0].astype(BF16), preferred_element_type=F32) + b_ref[0]


def _modulation(cc, w_mod, b_mod):
    depth, d, d6 = w_mod.shape
    rows = cc.shape[0]
    tn = _pick(d6, (1024, 512, 256, 128))
    return pl.pallas_call(
        _mod_kernel,
        out_shape=jax.ShapeDtypeStruct((depth, rows, d6), F32),
        grid=(depth, d6 // tn),
        in_specs=[
            pl.BlockSpec((rows, d), lambda l, n: (0, 0)),
            pl.BlockSpec((1, d, tn), lambda l, n: (l, 0, n)),
            pl.BlockSpec((1, 1, tn), lambda l, n: (l, 0, n)),
        ],
        out_specs=pl.BlockSpec((1, rows, tn), lambda l, n: (l, 0, n)),
        compiler_params=_cparams(("arbitrary", "arbitrary")),
        name="modulation",
    )(cc, w_mod, b_mod.reshape(depth, 1, d6))


def _stacked_specs(parts, block, tile_rows, col_of, n_lead):
    def grid_m(idx):
        return idx[n_lead]

    if len(parts) == 1:
        return [pl.BlockSpec(block, lambda *idx: (grid_m(idx), col_of(idx)))], None
    mt_a = parts[0].shape[0] // tile_rows
    return ([pl.BlockSpec(block, lambda *idx: (jnp.minimum(grid_m(idx), mt_a - 1), col_of(idx))),
             pl.BlockSpec(block, lambda *idx: (jnp.maximum(grid_m(idx) - mt_a, 0), col_of(idx)))], mt_a)


def _nm_mm_kernel(*refs, n_h, mt_a, n_g):
    h_refs = refs[:n_h]
    g_ref, sh_ref, sc_ref = refs[n_h:n_h + 3]
    w_refs, o_ref, u_ref = refs[n_h + 3:n_h + 3 + n_g], refs[n_h + 3 + n_g], refs[n_h + 4 + n_g]
    m = pl.program_id(0)

    def fill(h_ref):
        u_ref[...] = _rms_modulate(h_ref[...], g_ref[...], sc_ref[0], sh_ref[0]).astype(BF16)

    first = pl.program_id(1) == 0
    if n_h == 1:
        pl.when(first)(lambda: fill(h_refs[0]))
    else:
        pl.when(first & (m < mt_a))(lambda: fill(h_refs[0]))
        pl.when(first & (m >= mt_a))(lambda: fill(h_refs[1]))

    u = u_ref[...]
    accs = [jnp.dot(u, w[...].astype(BF16), preferred_element_type=F32) for w in w_refs]
    out = accs[0] if n_g == 1 else (accs[0] * jax.nn.sigmoid(accs[0])) * accs[1]
    o_ref[...] = out.astype(o_ref.dtype)


def _nm_matmul(hs, g, mod, sh_blk, sc_blk, ws, *, n_rows, s_len, mod_row, out_dtype):
    d = hs[0].shape[1]
    n_cols = ws[0].shape[1]
    tm = _row_tile(n_rows, s_len, (TM_X, 512, 256, 128))
    tn = TN
    h_specs, mt_a = _stacked_specs(hs, (tm, d), tm, lambda idx: 0, 0)
    in_specs = h_specs + [
        pl.BlockSpec((1, d), lambda m, n: (0, 0)),
        pl.BlockSpec((1, 1, d), lambda m, n: (mod_row(m * tm), 0, sh_blk)),
        pl.BlockSpec((1, 1, d), lambda m, n: (mod_row(m * tm), 0, sc_blk)),
    ] + [pl.BlockSpec((d, tn), lambda m, n: (0, n)) for _ in ws]
    return pl.pallas_call(
        functools.partial(_nm_mm_kernel, n_h=len(hs), mt_a=mt_a, n_g=len(ws)),
        out_shape=jax.ShapeDtypeStruct((n_rows, n_cols), out_dtype),
        grid=(n_rows // tm, pl.cdiv(n_cols, tn)),
        in_specs=in_specs,
        out_specs=pl.BlockSpec((tm, tn), lambda m, n: (m, n)),
        scratch_shapes=[pltpu.VMEM((tm, d), BF16)],
        compiler_params=_cparams(("arbitrary", "arbitrary"), VMEM_LIMIT_BIG),
        name="normmod_matmul",
    )(*hs, g.reshape(1, d), mod, mod, *ws)


def _final_norm_kernel(x_ref, g_ref, o_ref):
    x = x_ref[...]
    ms = jnp.mean(x * x, axis=-1, keepdims=True)
    o_ref[...] = x * lax.rsqrt(ms + EPS) * g_ref[...]


def _mm_kernel(te_ref, nu_ref, *refs, x_parts, res_parts, mt_a, n_g, epi):
    n_x = len(x_parts)
    refs = list(refs)

    def take(n):
        out = refs[:n]
        del refs[:n]
        return out

    x_refs = [take(p) for p in x_parts]
    w_refs = take(n_x * n_g)
    if epi == "resgate":
        res_refs = take(res_parts)
        gt_ref, = take(1)
    o_ref, = take(1)
    wc_refs = refs
    m = pl.program_id(1)

    def stacked(parts):
        if len(parts) == 1:
            return parts[0][...]
        return jnp.where(m < mt_a, parts[0][...], parts[1][...])

    panel_changed = (m == 0) | (te_ref[m] != te_ref[jnp.maximum(m - 1, 0)])

    @pl.when(panel_changed)
    def _():
        for w, wc in zip(w_refs, wc_refs):
            wc[...] = w[0].astype(BF16)

    @pl.when(m < nu_ref[0])
    def _():
        accs = []
        for g in range(n_g):
            acc = None
            for i in range(n_x):
                part = jnp.dot(stacked(x_refs[i]).astype(BF16), wc_refs[g * n_x + i][...],
                               preferred_element_type=F32)
                acc = part if acc is None else acc + part
            accs.append(acc)
        if epi == "plain":
            out = accs[0]
        elif epi == "packed":
            out = _pack_bf16_pairs(accs[0])
        elif epi == "swiglu":
            a = accs[0]
            out = (a * jax.nn.sigmoid(a)) * accs[1]
        else:
            out = stacked(res_refs) + gt_ref[0] * accs[0]
        o_ref[...] = out.astype(o_ref.dtype)

    @pl.when(m >= nu_ref[0])
    def _():
        o_ref[...] = jnp.zeros_like(o_ref)


def _matmul(xs, ws, *, n_rows, out_dtype, epi="plain", tile_expert=None, n_used=None,
            res=None, mod=None, gt_blk=None, mod_row=None, tm=TM, tn=TN):
    n_x, n_g = len(xs), len(ws)
    n_cols = ws[0].shape[2]
    tn = _pick(n_cols, (tn, 512, 256, 128))
    tm = _pick(n_rows, (tm, 256, 128))
    mt, nt = n_rows // tm, n_cols // tn
    if tile_expert is None:
        tile_expert = jnp.zeros((mt,), jnp.int32)
        n_used = jnp.full((1,), mt, jnp.int32)
    ks = [x[0].shape[1] for x in xs]
    in_specs, args, mt_a = [], [], None
    for x, k in zip(xs, ks):
        if len(x) == 1:
            in_specs.append(pl.BlockSpec((tm, k), lambda n, m, te, nu: (jnp.minimum(m, nu[0] - 1), 0)))
        else:
            specs, mt_a = _stacked_specs(x, (tm, k), tm, lambda idx: 0, 1)
            in_specs += specs
        args += list(x)
    for g in range(n_g):
        for i, k in enumerate(ks):
            in_specs.append(pl.BlockSpec((1, k, tn), lambda n, m, te, nu, i=i: (te[m], i, n)))
            args.append(ws[g])
    if epi == "resgate":
        blk0 = gt_blk * (n_cols // tn)
        specs, mt_res = _stacked_specs(res, (tm, tn), tm, lambda idx: idx[0], 1)
        mt_a = mt_res if mt_res is not None else mt_a
        in_specs += specs
        in_specs.append(pl.BlockSpec((1, 1, tn), lambda n, m, te, nu: (mod_row(m * tm), 0, blk0 + n)))
        args += list(res) + [mod]
    scratch = [pltpu.VMEM((k, tn), BF16) for _ in range(n_g) for k in ks]
    out_div = 2 if epi == "packed" else 1
    return pl.pallas_call(
        functools.partial(_mm_kernel, x_parts=tuple(len(x) for x in xs),
                          res_parts=len(res) if res is not None else 0, mt_a=mt_a, n_g=n_g, epi=epi),
        out_shape=jax.ShapeDtypeStruct((n_rows, n_cols // out_div), out_dtype),
        grid_spec=pltpu.PrefetchScalarGridSpec(
            num_scalar_prefetch=2,
            grid=(nt, mt),
            in_specs=in_specs,
            out_specs=pl.BlockSpec((tm, tn // out_div), lambda n, m, te, nu: (m, n)),
            scratch_shapes=scratch,
        ),
        compiler_params=_cparams(("arbitrary", "arbitrary")),
        name="matmul_" + epi,
    )(tile_expert, n_used, *args)


def _rope(x, cos, sin_signed, half):
    lane = lax.broadcasted_iota(jnp.int32, (x.shape[0], LANES), 1)
    first = (lane & (2 * half - 1)) < half
    blocks = []
    for j in range(x.shape[1] // LANES):
        xj = x[:, j * LANES:(j + 1) * LANES]
        partner = jnp.where(first, pltpu.roll(xj, LANES - half, 1), pltpu.roll(xj, half, 1))
        blocks.append(xj * cos + partner * sin_signed)
    return blocks[0] if len(blocks) == 1 else jnp.concatenate(blocks, axis=-1)


def _group_mean_sq(x, g_ref):
    x2 = x * x
    hi = x2.astype(BF16)
    lo = (x2 - hi.astype(F32)).astype(BF16)
    g = g_ref[...]
    return (jnp.dot(hi, g, preferred_element_type=F32) + jnp.dot(lo, g, preferred_element_type=F32))


def _dup_halves(x):
    lane = lax.broadcasted_iota(jnp.int32, x.shape, 1)
    lo = lane < HEAD_DIM
    swapped = pltpu.roll(x, HEAD_DIM, 1)
    return jnp.concatenate([jnp.where(lo, x, swapped), jnp.where(lo, swapped, x)], axis=-1)


def _store_transposed(dst_ref, x):
    for j in range(x.shape[1] // LANES):
        dst_ref[0, j * LANES:(j + 1) * LANES, :] = x[:, j * LANES:(j + 1) * LANES].T.astype(BF16)


def _proj_prep_kernel(*refs, n_h, mt_a):
    refs = list(refs)
    h_refs = [refs.pop(0) for _ in range(n_h)]
    (g_ref, sh_ref, sc_ref, w_ref, c64_ref, s64_ref, cq_ref, sq_ref, ck_ref, sk_ref,
     bqn_ref, bkn_ref, cqn_ref, ckvn_ref, g512_ref, g128_ref, wq_ref, wkk_ref, wkv_ref, place_ref,
     qa_ref, qb_ref, qc_ref, qd_ref,
     ka_ref, kb_ref, kc_ref, kd_ref, vd_ref, vta_ref, vtb_ref, vtc_ref, u_ref) = refs

    def fill(h_ref):
        u_ref[...] = _rms_modulate(h_ref[...], g_ref[...], sc_ref[0], sh_ref[0]).astype(BF16)

    if n_h == 1:
        fill(h_refs[0])
    else:
        t = pl.program_id(0)
        pl.when(t < mt_a)(lambda: fill(h_refs[0]))
        pl.when(t >= mt_a)(lambda: fill(h_refs[1]))

    def proj(off, width):
        return jnp.dot(u_ref[...], w_ref[:, off:off + width], preferred_element_type=F32)

    c64, s64 = c64_ref[...], s64_ref[...]
    c128, s128 = c64, s64
    scale64 = HEAD_DIM ** -0.5
    scale_mla = (C_NOPE_DIM + C_ROPE_DIM) ** -0.5

    qa_ref[...] = (_rope(proj(OFF_QA, 512), c64, s64, 16) * (scale64 * LOG2E)).astype(BF16)
    ka_ref[0] = _rope(proj(OFF_KA, 512), c64, s64, 16).astype(BF16)
    _store_transposed(vta_ref, proj(OFF_VA, 512))

    xb = proj(OFF_QB, 512)
    yb = xb * lax.rsqrt(_group_mean_sq(xb, g512_ref) + EPS) * bqn_ref[...]
    qb_ref[...] = (_rope(yb, c64, s64, 16) * (scale64 * LOG2E)).astype(BF16)
    kvb = proj(OFF_KB, 256)
    xk = kvb[:, :LANES]
    yk = xk * lax.rsqrt(_group_mean_sq(xk, g128_ref) + EPS) * bkn_ref[...]
    kb_ref[0] = _dup_halves(_rope(yk, c128, s128, 16)).astype(BF16)
    _store_transposed(vtb_ref, kvb[:, LANES:])

    xq = proj(OFF_QC, C_Q_RANK)
    yq = xq * lax.rsqrt(jnp.mean(xq * xq, axis=-1, keepdims=True) + EPS) * cqn_ref[...]
    qf = jnp.dot(yq.astype(BF16), wq_ref[...], preferred_element_type=F32)
    qc_ref[...] = (_rope(qf, cq_ref[...], sq_ref[...], 8) * (scale_mla * LOG2E)).astype(BF16)
    xc = proj(OFF_CKV, C_KV_RANK)
    yc = (xc * lax.rsqrt(jnp.mean(xc * xc, axis=-1, keepdims=True) + EPS) * ckvn_ref[...]).astype(BF16)
    tail = proj(OFF_KR, IN_COLS - OFF_KR)
    kr = _rope(tail[:, :LANES], ck_ref[...], sk_ref[...], 8).astype(BF16)
    kc = (jnp.dot(yc, wkk_ref[...], preferred_element_type=F32)
          + jnp.dot(kr, place_ref[...], preferred_element_type=F32))
    kc_ref[0] = kc.astype(BF16)
    _store_transposed(vtc_ref, jnp.dot(yc, wkv_ref[...], preferred_element_type=F32))

    qd_ref[...] = (_rope(proj(OFF_QD, 512), c64, s64, 16) * scale64).astype(BF16)
    kd0, vd0 = OFF_KD - OFF_KR, OFF_VD - OFF_KR
    kd_ref[0] = _dup_halves(_rope(tail[:, kd0:kd0 + LANES], c128, s128, 16)).astype(BF16)
    vd_ref[0] = _dup_halves(tail[:, vd0:vd0 + LANES]).astype(BF16)


def _proj_prep(hs, g, mod, w_in_bf16, tables, consts, dims, mod_row):
    bsz, s_len, n_ctx = dims
    d = hs[0].shape[1]
    n_rows = sum(a.shape[0] for a in hs)
    sk = n_ctx + s_len
    tr = _pick(math.gcd(s_len, n_ctx), (TR, 128))
    n_lat_t, lat_pb, ctx_pb = bsz * s_len // tr, s_len // tr, n_ctx // tr

    def is_lat(t):
        return t < n_lat_t

    def tbl_idx(t):
        return jnp.where(is_lat(t), t % lat_pb, lat_pb)

    def kv_b(t):
        return jnp.where(is_lat(t), t // lat_pb, (t - n_lat_t) // ctx_pb)

    def kv_j(t):
        return jnp.where(is_lat(t), ctx_pb + t % lat_pb, (t - n_lat_t) % ctx_pb)

    def row_spec(w):
        return pl.BlockSpec((tr, w), lambda t: (t, 0))

    def tbl_spec(w):
        return pl.BlockSpec((tr, w), lambda t: (tbl_idx(t), 0))

    def const_spec(a):
        return pl.BlockSpec(a.shape, lambda t: (0,) * a.ndim)

    def kv_spec(w):
        return pl.BlockSpec((1, tr, w), lambda t: (kv_b(t), kv_j(t), 0))

    def vt_spec(w):
        return pl.BlockSpec((1, w, tr), lambda t: (kv_b(t), 0, kv_j(t)))

    q_widths = (512, 512, 1024, 512)
    kv_widths = (512, 256, 1024, 256, 256)
    vt_widths = (512, 128, 512)
    out_shape = ([jax.ShapeDtypeStruct((n_rows, w), BF16) for w in q_widths]
                 + [jax.ShapeDtypeStruct((bsz, sk, w), BF16) for w in kv_widths]
                 + [jax.ShapeDtypeStruct((bsz, w, sk), BF16) for w in vt_widths])
    out_specs = ([row_spec(w) for w in q_widths] + [kv_spec(w) for w in kv_widths]
                 + [vt_spec(w) for w in vt_widths])
    h_specs, mt_a = _stacked_specs(hs, (tr, d), tr, lambda idx: 0, 0)
    in_specs = (h_specs
                + [pl.BlockSpec((1, d), lambda t: (0, 0)),
                   pl.BlockSpec((1, 1, d), lambda t: (mod_row(t * tr), 0, 0)),
                   pl.BlockSpec((1, 1, d), lambda t: (mod_row(t * tr), 0, 1)),
                   pl.BlockSpec(w_in_bf16.shape, lambda t: (0, 0), pipeline_mode=pl.Buffered(1))]
                + [tbl_spec(t.shape[1]) for t in tables] + [const_spec(a) for a in consts])
    return pl.pallas_call(
        functools.partial(_proj_prep_kernel, n_h=len(hs), mt_a=mt_a),
        out_shape=out_shape,
        grid=(n_rows // tr,),
        in_specs=in_specs,
        out_specs=out_specs,
        scratch_shapes=[pltpu.VMEM((tr, d), BF16)],
        compiler_params=_cparams(("arbitrary",), VMEM_LIMIT_BIG),
        name="proj_prep",
    )(*hs, g.reshape(1, d), mod, mod, w_in_bf16, *tables, *consts)


def _scores(a, b):
    return lax.dot_general(a, b, (((1,), (1,)), ((), ())), preferred_element_type=F32)


def _half_masks(shape):
    lane = lax.broadcasted_iota(jnp.int32, shape, 1)
    lo = lane < HEAD_DIM
    return lo, jnp.logical_not(lo)


def _softmax_pv_t(s, v_t):
    m = jnp.max(s, axis=0, keepdims=True)
    e = jnp.exp2(s - m).astype(BF16)
    dv = v_t.shape[0]
    v_ext = jnp.concatenate([v_t, jnp.ones((BF16_SUBLANES, v_t.shape[1]), BF16)], axis=0)
    oe = jnp.dot(v_ext, e, preferred_element_type=F32)
    return oe[:dv] / oe[dv:dv + 1]


def _run_units(score_fns, finish_fns):
    s = score_fns[0]()
    for i, finish in enumerate(finish_fns):
        s_next = score_fns[i + 1]() if i + 1 < len(score_fns) else None
        finish(s)
        s = s_next


def _gqa_t_kernel(q_ref, k_ref, vt_ref, o_ref):
    masks = _half_masks((q_ref.shape[0], LANES))
    n_pairs = q_ref.shape[1] // LANES
    pairs_per_group = B_HEADS // B_KV_HEADS // 2
    outs = {}

    def score(j, z):
        g = j // pairs_per_group
        q = q_ref[:, j * LANES:(j + 1) * LANES]
        return lambda: _scores(k_ref[0, :, g * LANES:(g + 1) * LANES],
                               jnp.where(masks[z], q, jnp.zeros_like(q)))

    def finish(j, z):
        g = j // pairs_per_group

        def fin(s):
            outs[z] = _softmax_pv_t(s, vt_ref[0, g * HEAD_DIM:(g + 1) * HEAD_DIM, :])
            if z == 1:
                pair = jnp.concatenate([outs[0], outs[1]], axis=0)
                o_ref[:, j * LANES:(j + 1) * LANES] = pair.T.astype(o_ref.dtype)
        return fin

    units = [(j, z) for j in range(n_pairs) for z in range(2)]
    _run_units([score(j, z) for j, z in units], [finish(j, z) for j, z in units])


def _mla_t_kernel(q_ref, k_ref, vt_ref, o_ref):
    n_heads = q_ref.shape[1] // LANES
    outs = {}

    def score(u):
        return lambda: _scores(k_ref[0, :, u * LANES:(u + 1) * LANES], q_ref[:, u * LANES:(u + 1) * LANES])

    def finish(u):
        def fin(s):
            outs[u % 2] = _softmax_pv_t(s, vt_ref[0, u * C_V_DIM:(u + 1) * C_V_DIM, :])
            if u % 2 == 1:
                pair = jnp.concatenate([outs[0], outs[1]], axis=0)
                j = u // 2
                o_ref[:, j * LANES:(j + 1) * LANES] = pair.T.astype(o_ref.dtype)
        return fin

    _run_units([score(u) for u in range(n_heads)], [finish(u) for u in range(n_heads)])


def _diff_t_kernel(lam_ref, subln_ref, q0_ref, q1_ref, k0_ref, k1_ref, vt_ref, o_ref, *, lam_init):
    t = lam_ref[...]
    lam = (jnp.exp(jnp.sum(t[0:1] * t[1:2], axis=-1, keepdims=True))
           - jnp.exp(jnp.sum(t[2:3] * t[3:4], axis=-1, keepdims=True)) + lam_init)
    q_refs, k_refs = (q0_ref, q1_ref), (k0_ref, k1_ref)
    masks = _half_masks((q0_ref.shape[0], LANES))
    first_map = {}

    def score(z, mp):
        j = z // 2
        q = q_refs[mp][:, j * LANES:(j + 1) * LANES]
        return lambda: _scores(k_refs[mp][0, :, j * LANES:(j + 1) * LANES],
                               jnp.where(masks[z % 2], q, jnp.zeros_like(q)))

    def finish(z, mp):
        def fin(s):
            a = _softmax_pv_t(s, vt_ref[0, z * A_V_DIM:(z + 1) * A_V_DIM, :])
            if mp == 0:
                first_map[z] = a
                return
            d = (first_map[z] - lam * a).T
            y = d * lax.rsqrt(jnp.mean(d * d, axis=-1, keepdims=True) + EPS) * subln_ref[...]
            o_ref[:, z * A_V_DIM:(z + 1) * A_V_DIM] = (y * (1.0 - lam_init)).astype(o_ref.dtype)
        return fin

    units = [(z, mp) for z in range(A_HEADS) for mp in range(2)]
    _run_units([score(z, mp) for z, mp in units], [finish(z, mp) for z, mp in units])


def _attend_sink(q, k, v, sink):
    s = _scores(q, k)
    m = jnp.maximum(jnp.max(s, axis=-1, keepdims=True), sink)
    e = jnp.exp(s - m)
    l = jnp.sum(e, axis=-1, keepdims=True) + jnp.exp(sink - m)
    return jnp.dot(e.astype(BF16), v, preferred_element_type=F32) / l


def _window_kernel(sink_ref, q_ref, k_ref, v_ref, o_ref, *, s_len, n_ctx, n_q_lat, with_ctx):
    tq = q_ref.shape[0]
    band = tq + 2 * WINDOW
    i = pl.program_id(1)
    pairs_per_group = D_HEADS // D_KV_HEADS // 2

    def body(is_lat):
        masks = _half_masks((tq, LANES))
        if is_lat:
            start = pl.multiple_of(jnp.clip(i * tq - WINDOW, 0, s_len - band), LANES)
            row0 = pl.multiple_of(n_ctx + start, LANES)
            qpos = i * tq + lax.broadcasted_iota(jnp.int32, (tq, band), 0)
            kpos = start + lax.broadcasted_iota(jnp.int32, (tq, band), 1)
            in_band = jnp.abs(qpos - kpos) <= WINDOW
        for j in range(q_ref.shape[1] // LANES):
            g = j // pairs_per_group
            lanes = slice(g * LANES, (g + 1) * LANES)
            k_c, v_c = k_ref[0, :n_ctx, lanes], v_ref[0, :n_ctx, lanes]
            if is_lat:
                k_b = k_ref[0, pl.ds(row0, band), lanes]
                v_b = v_ref[0, pl.ds(row0, band), lanes]
            q = q_ref[:, j * LANES:(j + 1) * LANES]
            outs = []
            for z in range(2):
                qz = jnp.where(masks[z], q, jnp.zeros_like(q))
                sink = sink_ref[2 * j + z]
                if not is_lat:
                    outs.append(_attend_sink(qz, k_c, v_c, sink))
                    continue
                s_c = _scores(qz, k_c)
                s_b = jnp.where(in_band, _scores(qz, k_b), -jnp.inf)
                m = jnp.maximum(jnp.maximum(jnp.max(s_c, axis=-1, keepdims=True),
                                            jnp.max(s_b, axis=-1, keepdims=True)), sink)
                e_c, e_b = jnp.exp(s_c - m), jnp.exp(s_b - m)
                l = (jnp.sum(e_c, axis=-1, keepdims=True) + jnp.sum(e_b, axis=-1, keepdims=True)
                     + jnp.exp(sink - m))
                o = (jnp.dot(e_c.astype(BF16), v_c, preferred_element_type=F32)
                     + jnp.dot(e_b.astype(BF16), v_b, preferred_element_type=F32))
                outs.append(o / l)
            o_ref[:, j * LANES:(j + 1) * LANES] = jnp.where(masks[0], outs[0], outs[1]).astype(o_ref.dtype)

    if not with_ctx:
        body(True)
        return

    @pl.when(i < n_q_lat)
    def _():
        body(True)

    @pl.when(i >= n_q_lat)
    def _():
        body(False)


def _mixers(ops, dims, params, lam_init, with_ctx):
    bsz, s_len, n_ctx = dims
    qa, qb, qc, qd, ka, kb, kc, kd, vd, vta, vtb, vtc = ops
    lam_vecs, subln, sink = params
    n_lat = bsz * s_len
    sk_all = n_ctx + s_len

    def const_spec(a):
        return pl.BlockSpec(a.shape, lambda b, i: (0,) * a.ndim)

    def dense_set(tq, nk, row0_tiles, n_q, out_rows):
        def q_row(b, i):
            return row0_tiles + b * n_q + i

        def qspec(w, col=0):
            return pl.BlockSpec((tq, w), lambda b, i: (q_row(b, i), col))

        def kspec(w, col=0):
            return pl.BlockSpec((1, nk, w), lambda b, i: (b, 0, col))

        def vtspec(w):
            return pl.BlockSpec((1, w, nk), lambda b, i: (b, 0, 0))

        def call(kernel, in_specs, args, name):
            return pl.pallas_call(
                kernel,
                out_shape=jax.ShapeDtypeStruct((out_rows, 512), BF16),
                grid=(bsz, n_q),
                in_specs=in_specs,
                out_specs=pl.BlockSpec((tq, 512), lambda b, i: (b * n_q + i, 0)),
                compiler_params=_cparams(("arbitrary", "arbitrary"), VMEM_LIMIT_BIG),
                name=name,
            )(*args)

        oa = call(functools.partial(_diff_t_kernel, lam_init=lam_init),
                  [const_spec(lam_vecs), const_spec(subln), qspec(256, 0), qspec(256, 1),
                   kspec(256, 0), kspec(256, 1), vtspec(512)],
                  [lam_vecs, subln, qa, qa, ka, ka, vta], "attn_diff")
        ob = call(_gqa_t_kernel, [qspec(512), kspec(256), vtspec(128)], [qb, kb, vtb], "attn_qknorm")
        oc = call(_mla_t_kernel, [qspec(1024), kspec(1024), vtspec(512)], [qc, kc, vtc], "attn_mla")
        return [oa, ob, oc]

    tq = _pick(s_len, (TQ, 256, 128))
    dense = [[o] for o in dense_set(tq, sk_all, 0, s_len // tq, n_lat)]
    if with_ctx:
        tq_c = _pick(n_ctx, (TQ, 256, 128))
        ctx = dense_set(tq_c, n_ctx, n_lat // tq_c, n_ctx // tq_c, bsz * n_ctx)
        dense = [a + [b] for a, b in zip(dense, ctx)]

    tw = _pick(math.gcd(s_len, n_ctx), (TQ_WIN, 128))
    n_q_lat, n_q_ctx = s_len // tw, n_ctx // tw
    n_q = n_q_lat + n_q_ctx if with_ctx else n_q_lat
    out_rows = n_lat + bsz * n_ctx if with_ctx else n_lat

    def w_row(b, i):
        return jnp.where(i < n_q_lat, b * n_q_lat + i, n_lat // tw + b * n_q_ctx + (i - n_q_lat))

    od = pl.pallas_call(
        functools.partial(_window_kernel, s_len=s_len, n_ctx=n_ctx, n_q_lat=n_q_lat, with_ctx=with_ctx),
        out_shape=jax.ShapeDtypeStruct((out_rows, 512), BF16),
        grid=(bsz, n_q),
        in_specs=[pl.BlockSpec(memory_space=pltpu.SMEM),
                  pl.BlockSpec((tw, 512), lambda b, i: (w_row(b, i), 0)),
                  pl.BlockSpec((1, sk_all, 256), lambda b, i: (b, 0, 0)),
                  pl.BlockSpec((1, sk_all, 256), lambda b, i: (b, 0, 0))],
        out_specs=pl.BlockSpec((tw, 512), lambda b, i: (w_row(b, i), 0)),
        compiler_params=_cparams(("arbitrary", "arbitrary")),
        name="attn_window",
    )(sink, qd, kd, vd)
    return dense + [[od]]


def _route_kernel(h_ref, g_ref, sh_ref, sc_ref, w_ref, b_ref, u_ref, sel_ref, idx_ref, gw_ref):
    u = _rms_modulate(h_ref[...], g_ref[...], sc_ref[0], sh_ref[0])
    u_ref[...] = _pack_bf16_pairs(u)
    w = w_ref[...]
    u_hi = u.astype(BF16)
    u_lo = (u - u_hi.astype(F32)).astype(BF16)
    w_hi = w.astype(BF16)
    w_lo = (w - w_hi.astype(F32)).astype(BF16)
    logits = (_scores(w_hi, u_hi) + _scores(w_hi, u_lo) + _scores(w_lo, u_hi) + _scores(w_lo, u_lo)
              + b_ref[...])
    ids = lax.broadcasted_iota(jnp.int32, logits.shape, 0).astype(F32)
    m1 = jnp.max(logits, axis=0, keepdims=True)
    i1 = jnp.min(jnp.where(logits == m1, ids, float(N_EXPERTS)), axis=0, keepdims=True)
    first = ids == i1
    rest = jnp.where(first, -jnp.inf, logits)
    m2 = jnp.max(rest, axis=0, keepdims=True)
    i2 = jnp.min(jnp.where(rest == m2, ids, float(N_EXPERTS)), axis=0, keepdims=True)
    second = ids == i2
    e = jnp.exp(m2 - m1)
    w1 = 1.0 / (1.0 + e)
    w2 = e / (1.0 + e)
    sel_ref[...] = jnp.where(first | second, 1, 0).astype(jnp.int32)
    idx_ref[...] = jnp.where(ids == 0.0, i1, jnp.where(ids == 1.0, i2, 0.0)).astype(jnp.int32)
    gw_ref[...] = jnp.where(ids == 0.0, w1, jnp.where(ids == 1.0, w2, 0.0))


def _route(h, g, mod, sh_blk, sc_blk, w_router_t, b_router, n_tok, s_len, mod_row):
    d = h.shape[1]
    tm = _row_tile(n_tok, s_len, (TM, 256, 128))
    outs = ([jax.ShapeDtypeStruct((n_tok, d // 2), jnp.uint32)]
            + [jax.ShapeDtypeStruct((N_EXPERTS, n_tok), dt) for dt in (jnp.int32, jnp.int32, F32)])
    return pl.pallas_call(
        _route_kernel,
        out_shape=outs,
        grid=(n_tok // tm,),
        in_specs=[pl.BlockSpec((tm, d), lambda m: (m, 0)),
                  pl.BlockSpec((1, d), lambda m: (0, 0)),
                  pl.BlockSpec((1, 1, d), lambda m: (mod_row(m * tm), 0, sh_blk)),
                  pl.BlockSpec((1, 1, d), lambda m: (mod_row(m * tm), 0, sc_blk)),
                  pl.BlockSpec((N_EXPERTS, d), lambda m: (0, 0)),
                  pl.BlockSpec((N_EXPERTS, 1), lambda m: (0, 0))],
        out_specs=[pl.BlockSpec((tm, d // 2), lambda m: (m, 0))]
        + [pl.BlockSpec((N_EXPERTS, tm), lambda m: (0, m))] * 3,
        compiler_params=_cparams(("arbitrary",)),
        name="moe_route",
    )(h, g.reshape(1, d), mod, mod, w_router_t, b_router.reshape(N_EXPERTS, 1))


def _row_copy(src, dst, sem, src_row, dst_row):
    return pltpu.make_async_copy(src.at[pl.ds(src_row, 1)], dst.at[pl.ds(dst_row, 1)], sem)


def _gather_kernel(tok_ref, nxt_ref, src_ref, o_ref, buf, sem):
    i = pl.program_id(0)
    n = pl.num_programs(0)
    tg = buf.shape[1]
    slot = i % 2

    def issue(t_ref, s):
        def body(r, carry):
            _row_copy(src_ref, buf.at[s], sem.at[s], t_ref[0, 0, r], r).start()
            return carry
        lax.fori_loop(0, tg, body, 0, unroll=DMA_UNROLL)

    @pl.when(i == 0)
    def _():
        issue(tok_ref, 0)

    @pl.when(i + 1 < n)
    def _():
        issue(nxt_ref, 1 - slot)

    def wait(r, carry):
        _row_copy(src_ref, buf.at[slot], sem.at[slot], 0, r).wait()
        return carry

    lax.fori_loop(0, tg, wait, 0, unroll=DMA_UNROLL)
    lo, hi = _unpack_bf16_pairs(buf[slot])
    half = lo.shape[1]
    o_ref[:, :half] = lo.astype(o_ref.dtype)
    o_ref[:, half:] = hi.astype(o_ref.dtype)


def _gather_rows(src, tok, tg):
    n_rows = tok.shape[0]
    dw = src.shape[1]
    d = 2 * dw
    n_t = n_rows // tg
    tok = tok.reshape(n_t, 1, tg)
    return pl.pallas_call(
        _gather_kernel,
        out_shape=jax.ShapeDtypeStruct((n_rows, d), BF16),
        grid=(n_t,),
        in_specs=[pl.BlockSpec((1, 1, tg), lambda i: (i, 0, 0), memory_space=pltpu.SMEM),
                  pl.BlockSpec((1, 1, tg), lambda i: (jnp.minimum(i + 1, n_t - 1), 0, 0),
                               memory_space=pltpu.SMEM),
                  pl.BlockSpec(memory_space=pl.ANY)],
        out_specs=pl.BlockSpec((tg, d), lambda i: (i, 0)),
        scratch_shapes=[pltpu.VMEM((2, tg, dw), jnp.uint32), pltpu.SemaphoreType.DMA((2,))],
        compiler_params=_cparams(("arbitrary",)),
        name="moe_gather",
    )(tok, tok, src)


def _combine_kernel(pos_ref, nxt_ref, y_ref, h_ref, gw_ref, gt_ref, gf_ref, o_ref, buf, sem, *, pack_tile):
    i = pl.program_id(0)
    n = pl.num_programs(0)
    tc = buf.shape[2]
    slot = i % 2

    def issue(p_ref, s):
        def body(r, carry):
            _row_copy(y_ref, buf.at[s, 0], sem.at[s], p_ref[0, 0, r], r).start()
            _row_copy(y_ref, buf.at[s, 1], sem.at[s], p_ref[0, 1, r], r).start()
            return carry
        lax.fori_loop(0, tc, body, 0, unroll=DMA_UNROLL)

    @pl.when(i == 0)
    def _():
        issue(pos_ref, 0)

    @pl.when(i + 1 < n)
    def _():
        issue(nxt_ref, 1 - slot)

    def wait(r, carry):
        _row_copy(y_ref, buf.at[slot, 0], sem.at[slot], 0, r).wait()
        _row_copy(y_ref, buf.at[slot, 1], sem.at[slot], 0, r).wait()
        return carry

    lax.fori_loop(0, tc, wait, 0, unroll=DMA_UNROLL)
    def expert_rows(k):
        lo, hi = _unpack_bf16_pairs(buf[slot, k])
        hw = pack_tile // 2
        cols = []
        for j in range(lo.shape[1] // hw):
            cols += [lo[:, j * hw:(j + 1) * hw], hi[:, j * hw:(j + 1) * hw]]
        return jnp.concatenate(cols, axis=-1)

    gw = gw_ref[...]
    moe = gw[:, 0:1] * expert_rows(0) + gw[:, 1:2] * expert_rows(1)
    x = h_ref[...] + gt_ref[0] * moe
    ms = jnp.mean(x * x, axis=-1, keepdims=True)
    o_ref[...] = x * lax.rsqrt(ms + EPS) * gf_ref[...]


def _combine(y, h, pos, gw_t, mod, gt_blk, mod_row, g_final, n_tok, pack_tile):
    d = h.shape[1]
    n_t, _, tc = pos.shape
    return pl.pallas_call(
        functools.partial(_combine_kernel, pack_tile=pack_tile),
        out_shape=jax.ShapeDtypeStruct((n_tok, d), F32),
        grid=(n_t,),
        in_specs=[pl.BlockSpec((1, 2, tc), lambda i: (i, 0, 0), memory_space=pltpu.SMEM),
                  pl.BlockSpec((1, 2, tc), lambda i: (jnp.minimum(i + 1, n_t - 1), 0, 0),
                               memory_space=pltpu.SMEM),
                  pl.BlockSpec(memory_space=pl.ANY),
                  pl.BlockSpec((tc, d), lambda i: (i, 0)),
                  pl.BlockSpec((tc, N_EXPERTS), lambda i: (i, 0)),
                  pl.BlockSpec((1, 1, d), lambda i: (mod_row(i * tc), 0, gt_blk)),
                  pl.BlockSpec((1, d), lambda i: (0, 0))],
        out_specs=pl.BlockSpec((tc, d), lambda i: (i, 0)),
        scratch_shapes=[pltpu.VMEM((2, 2, tc, d // 2), jnp.uint32), pltpu.SemaphoreType.DMA((2,))],
        compiler_params=_cparams(("arbitrary",)),
        name="moe_combine",
    )(pos, pos, y, h, gw_t, mod, g_final.reshape(1, d))


def _dispatch_plan(sel, idx, tm, n_slots, tc):
    n_tok = sel.shape[1]
    n_tiles = n_slots // tm
    counts = jnp.sum(sel, axis=1)
    padded = ((counts + tm - 1) // tm) * tm
    ends = jnp.cumsum(padded)
    offs = ends - padded
    pos = offs[:, None] + jnp.cumsum(sel, axis=1) - sel
    pos0 = jnp.take_along_axis(pos, idx[0:1], axis=0)[0]
    pos1 = jnp.take_along_axis(pos, idx[1:2], axis=0)[0]
    pos_tiles = jnp.stack([pos0.reshape(n_tok // tc, tc), pos1.reshape(n_tok // tc, tc)], axis=1)
    tile_start = jnp.arange(n_tiles, dtype=jnp.int32) * tm
    tile_expert = jnp.minimum(jnp.sum(ends[None, :] <= tile_start[:, None], axis=1), N_EXPERTS - 1)
    n_used = ends[-1] // tm
    tok = jnp.arange(n_tok, dtype=jnp.int32)
    tok_of_row = jnp.zeros((n_slots,), jnp.int32).at[jnp.concatenate([pos0, pos1])].set(
        jnp.concatenate([tok, tok]), unique_indices=True)
    return (pos_tiles.astype(jnp.int32), tok_of_row, tile_expert.astype(jnp.int32),
            n_used.reshape(1).astype(jnp.int32))


def _rope_tables(s_len, pad_rows):
    f32 = np.float32
    t = np.arange(s_len)
    rows, cols = (t // GRID_W).astype(f32), (t % GRID_W).astype(f32)

    def axis_tables(rot_dim):
        axis_dim = rot_dim // 2
        inv = (f32(ROPE_THETA) ** (-np.arange(0, axis_dim, 2, dtype=f32) / f32(axis_dim))).astype(f32)
        ar, ac = rows[:, None] * inv[None, :], cols[:, None] * inv[None, :]
        cos = np.concatenate([np.cos(ar), np.cos(ar), np.cos(ac), np.cos(ac)], axis=1)
        sin = np.concatenate([-np.sin(ar), np.sin(ar), -np.sin(ac), np.sin(ac)], axis=1)
        return cos.astype(f32), sin.astype(f32)

    def with_identity(cos, sin):
        w = cos.shape[1]
        return (np.concatenate([cos, np.ones((pad_rows, w), f32)], axis=0),
                np.concatenate([sin, np.zeros((pad_rows, w), f32)], axis=0))

    c64, s64 = axis_tables(HEAD_DIM)
    c64, s64 = np.tile(c64, (1, 2)), np.tile(s64, (1, 2))
    c32, s32 = axis_tables(C_ROPE_DIM)
    ones, zeros = np.ones((s_len, 1), f32), np.zeros((s_len, 1), f32)
    cq = np.concatenate([np.tile(ones, (1, 64)), c32, np.tile(ones, (1, 32))], axis=1)
    sq = np.concatenate([np.tile(zeros, (1, 64)), s32, np.tile(zeros, (1, 32))], axis=1)
    ck = np.concatenate([c32, np.tile(ones, (1, 96))], axis=1)
    sk = np.concatenate([s32, np.tile(zeros, (1, 96))], axis=1)
    out = []
    for c, s in ((c64, s64), (cq, sq), (ck, sk)):
        out += [jnp.asarray(a) for a in with_identity(c, s)]
    return out


def _mla_weights(w_q_up, w_kv_up):
    qd = C_NOPE_DIM + C_ROPE_DIM
    wq = jnp.pad(w_q_up.reshape(C_Q_RANK, C_HEADS, qd), ((0, 0), (0, 0), (0, LANES - qd)))
    wkv = w_kv_up.reshape(C_KV_RANK, C_HEADS, C_NOPE_DIM + C_V_DIM)
    wkk = jnp.pad(wkv[:, :, :C_NOPE_DIM], ((0, 0), (0, 0), (0, LANES - C_NOPE_DIM)))
    wkv_v = wkv[:, :, C_NOPE_DIM:]
    return (wq.reshape(C_Q_RANK, C_HEADS * LANES).astype(BF16),
            wkk.reshape(C_KV_RANK, C_HEADS * LANES).astype(BF16),
            wkv_v.reshape(C_KV_RANK, C_HEADS * C_V_DIM).astype(BF16))


def _static_mats():
    g = (np.arange(512)[:, None] // HEAD_DIM == np.arange(512)[None, :] // HEAD_DIM) / HEAD_DIM
    place = np.zeros((LANES, C_HEADS * LANES), np.float32)
    for h in range(C_HEADS):
        place[np.arange(C_ROPE_DIM), h * LANES + C_NOPE_DIM + np.arange(C_ROPE_DIM)] = 1.0
    return (jnp.asarray(g, BF16), jnp.asarray(g[:128, :128], BF16), jnp.asarray(place, BF16))


def kernel(x, c, ctx, c_ctx, w_mod, b_mod, g_mix, g_ffn, g_final, w_in, w_out, a_lam_q1, a_lam_k1, a_lam_q2, a_lam_k2, a_subln, b_q_norm, b_k_norm, c_q_norm, c_kv_norm, c_w_q_up, c_w_kv_up, d_sink, ffn_w_gate, ffn_w_up, ffn_w_down, moe_w_router, moe_b_router, moe_w_gate, moe_w_up, moe_w_down):
    bsz, s_len, d = x.shape
    n_ctx = ctx.shape[1]
    depth = w_mod.shape[0]
    n_lat = bsz * s_len
    n_all = n_lat + bsz * n_ctx
    dims = (bsz, s_len, n_ctx)

    def mod_row(row0):
        return jnp.where(row0 < n_lat, row0 // s_len, bsz)

    mod_rows = -(-(bsz + 1) // 8) * 8
    cc = jnp.zeros((mod_rows, d), F32).at[:bsz].set(c).at[bsz].set(c_ctx)
    mods = _modulation(cc, w_mod, b_mod)

    tr = _pick(math.gcd(s_len, n_ctx), (TR, 128))
    tables = _rope_tables(s_len, tr)
    g512, g128, place = _static_mats()

    h = [x.reshape(n_lat, d), ctx.reshape(bsz * n_ctx, d)]
    for l in range(depth):
        last = l == depth - 1
        lam_init = 0.8 - 0.6 * math.exp(-0.3 * l)
        n_rows = n_lat if last else n_all
        mod = mods[l].reshape(mod_rows, 1, 6 * d)
        nm = dict(s_len=s_len, mod_row=mod_row)

        wq, wkk, wkv_v = _mla_weights(c_w_q_up[l], c_w_kv_up[l])
        consts = [jnp.tile(b_q_norm[l], 8)[None], jnp.tile(b_k_norm[l], 2)[None],
                  c_q_norm[l][None], c_kv_norm[l][None], g512, g128, wq, wkk, wkv_v, place]
        ops = _proj_prep(h, g_mix[l], mod, w_in[l].astype(BF16), tables, consts, dims, mod_row)
        lam_vecs = jnp.stack([a_lam_q1[l], a_lam_k1[l], a_lam_q2[l], a_lam_k2[l]])
        mix = _mixers(ops, dims, (lam_vecs, a_subln[l][None], d_sink[l]), lam_init, not last)
        h = [_matmul(mix, [w_out[l][None]], n_rows=n_rows, out_dtype=F32, epi="resgate",
                     res=h, mod=mod, gt_blk=2, mod_row=mod_row, tn=1024)]

        i = l // 2
        if l % 2 == 0:
            mid = _nm_matmul(h, g_ffn[l], mod, 3, 4, [ffn_w_gate[i], ffn_w_up[i]], n_rows=n_rows,
                             out_dtype=BF16, **nm)
            h = [_matmul([[mid]], [ffn_w_down[i][None]], n_rows=n_rows, out_dtype=F32, epi="resgate",
                         res=h, mod=mod, gt_blk=5, mod_row=mod_row)]
        else:
            if not last:
                raise NotImplementedError("expert layers are only supported as the last layer")
            h = h[0]
            u2, sel, idx, gw = _route(h, g_ffn[l], mod, 3, 4, moe_w_router[i].T, moe_b_router[i],
                                      n_rows, s_len, mod_row)
            tm = _pick(n_rows, (TM, 256, 128))
            tc = _pick(s_len, (TC, 128))
            n_slots = 2 * n_rows + N_EXPERTS * tm
            pos, tok_of_row, tile_expert, n_used = _dispatch_plan(sel, idx, tm, n_slots, tc)
            xs = _gather_rows(u2, tok_of_row, tm)
            mid = _matmul([[xs]], [moe_w_gate[i], moe_w_up[i]], n_rows=n_slots, out_dtype=BF16,
                          epi="swiglu", tile_expert=tile_expert, n_used=n_used)
            y = _matmul([[mid]], [moe_w_down[i]], n_rows=n_slots, out_dtype=jnp.uint32, epi="packed",
                        tile_expert=tile_expert, n_used=n_used)
            out = _combine(y, h, pos, gw.T, mod, 5, mod_row, g_final, n_rows, _pick(d, (TN, 256, 128)))
            return out.reshape(bsz, s_len, d)

    tm = _pick(n_lat, (TM, 256, 128))
    out = pl.pallas_call(
        _final_norm_kernel,
        out_shape=jax.ShapeDtypeStruct((n_lat, d), F32),
        grid=(n_lat // tm,),
        in_specs=[pl.BlockSpec((tm, d), lambda m: (m, 0)), pl.BlockSpec((1, d), lambda m: (0, 0))],
        out_specs=pl.BlockSpec((tm, d), lambda m: (m, 0)),
        compiler_params=_cparams(("arbitrary",)),
        name="final_norm",
    )(h[0], g_final.reshape(1, d))
    return out.reshape(bsz, s_len, d)
```

```python
import functools
import math

import numpy as np
import jax
import jax.numpy as jnp
from jax import lax
from jax.experimental import pallas as pl
from jax.experimental.pallas import tpu as pltpu

F32 = jnp.float32
BF16 = jnp.bfloat16

GRID_W = 64
HEAD_DIM = 64
ROPE_THETA = 10000.0
EPS = 1e-6
A_HEADS, A_QK_DIM, A_V_DIM = 4, 64, 128
B_HEADS, B_KV_HEADS = 8, 2
C_HEADS, C_Q_RANK, C_KV_RANK, C_NOPE_DIM, C_ROPE_DIM, C_V_DIM = 8, 768, 256, 64, 32, 64
D_HEADS, D_KV_HEADS = 8, 2
WINDOW = 128
N_EXPERTS = 8
LOG2E = 1.4426950408889634

LANES = 128
BF16_SUBLANES = 16
VMEM_LIMIT = 48 * 2**20
VMEM_LIMIT_BIG = 56 * 2**20

TM = 512
TM_X = 1024
TN = 512
TR = 256
TQ = 512
TQ_WIN = 256
TC = 256
DMA_UNROLL = 8

OFF_QA, OFF_QB, OFF_QC, OFF_QD = 0, 512, 1024, 1792
OFF_KA, OFF_VA, OFF_KB, OFF_VB = 2304, 2816, 3328, 3456
OFF_CKV, OFF_KR, OFF_KD, OFF_VD = 3584, 3840, 3872, 4000
IN_COLS = 4128


def _cparams(sem, vmem=VMEM_LIMIT):
    return pltpu.CompilerParams(dimension_semantics=sem, vmem_limit_bytes=vmem)


def _pick(n, prefs):
    for p in prefs:
        if n % p == 0:
            return p
    return n


def _row_tile(n_rows, s_len, prefs):
    for p in prefs:
        if n_rows % p == 0 and s_len % p == 0:
            return p
    raise ValueError("no row tile fits")


def _pack_bf16_pairs(x):
    w = x.shape[1] // 2
    xb = x.astype(BF16).astype(F32)
    return (pltpu.bitcast(xb[:, :w], jnp.uint32) >> 16) | pltpu.bitcast(xb[:, w:], jnp.uint32)


def _unpack_bf16_pairs(words):
    return (pltpu.bitcast(words << 16, F32), pltpu.bitcast(words & jnp.uint32(0xFFFF0000), F32))


def _rms_modulate(x, g, sc, sh):
    ms = jnp.mean(x * x, axis=-1, keepdims=True)
    return (x * lax.rsqrt(ms + EPS) * g) * (1.0 + sc) + sh


def _mod_kernel(c_ref, w_ref, b_ref, o_ref):
    a = c_ref[...]
    a = (a * jax.nn.sigmoid(a)).astype(BF16)
    o_ref[0] = jnp.dot(a, w_ref[0].astype(BF16), preferred_element_type=F32) + b_ref[0]


def _modulation(cc, w_mod, b_mod):
    depth, d, d6 = w_mod.shape
    rows = cc.shape[0]
    tn = _pick(d6, (1024, 512, 256, 128))
    return pl.pallas_call(
        _mod_kernel,
        out_shape=jax.ShapeDtypeStruct((depth, rows, d6), F32),
        grid=(depth, d6 // tn),
        in_specs=[
            pl.BlockSpec((rows, d), lambda l, n: (0, 0)),
            pl.BlockSpec((1, d, tn), lambda l, n: (l, 0, n)),
            pl.BlockSpec((1, 1, tn), lambda l, n: (l, 0, n)),
        ],
        out_specs=pl.BlockSpec((1, rows, tn), lambda l, n: (l, 0, n)),
        compiler_params=_cparams(("arbitrary", "arbitrary")),
        name="modulation",
    )(cc, w_mod, b_mod.reshape(depth, 1, d6))


def _stacked_specs(parts, block, tile_rows, col_of, n_lead):
    def grid_m(idx):
        return idx[n_lead]

    if len(parts) == 1:
        return [pl.BlockSpec(block, lambda *idx: (grid_m(idx), col_of(idx)))], None
    mt_a = parts[0].shape[0] // tile_rows
    return ([pl.BlockSpec(block, lambda *idx: (jnp.minimum(grid_m(idx), mt_a - 1), col_of(idx))),
             pl.BlockSpec(block, lambda *idx: (jnp.maximum(grid_m(idx) - mt_a, 0), col_of(idx)))], mt_a)


def _nm_mm_kernel(*refs, n_h, mt_a, n_g):
    h_refs = refs[:n_h]
    g_ref, sh_ref, sc_ref = refs[n_h:n_h + 3]
    w_refs, o_ref, u_ref = refs[n_h + 3:n_h + 3 + n_g], refs[n_h + 3 + n_g], refs[n_h + 4 + n_g]
    m = pl.program_id(0)

    def fill(h_ref):
        u_ref[...] = _rms_modulate(h_ref[...], g_ref[...], sc_ref[0], sh_ref[0]).astype(BF16)

    first = pl.program_id(1) == 0
    if n_h == 1:
        pl.when(first)(lambda: fill(h_refs[0]))
    else:
        pl.when(first & (m < mt_a))(lambda: fill(h_refs[0]))
        pl.when(first & (m >= mt_a))(lambda: fill(h_refs[1]))

    u = u_ref[...]
    accs = [jnp.dot(u, w[...].astype(BF16), preferred_element_type=F32) for w in w_refs]
    out = accs[0] if n_g == 1 else (accs[0] * jax.nn.sigmoid(accs[0])) * accs[1]
    o_ref[...] = out.astype(o_ref.dtype)


def _nm_matmul(hs, g, mod, sh_blk, sc_blk, ws, *, n_rows, s_len, mod_row, out_dtype):
    d = hs[0].shape[1]
    n_cols = ws[0].shape[1]
    tm = _row_tile(n_rows, s_len, (TM_X, 512, 256, 128))
    tn = TN
    h_specs, mt_a = _stacked_specs(hs, (tm, d), tm, lambda idx: 0, 0)
    in_specs = h_specs + [
        pl.BlockSpec((1, d), lambda m, n: (0, 0)),
        pl.BlockSpec((1, 1, d), lambda m, n: (mod_row(m * tm), 0, sh_blk)),
        pl.BlockSpec((1, 1, d), lambda m, n: (mod_row(m * tm), 0, sc_blk)),
    ] + [pl.BlockSpec((d, tn), lambda m, n: (0, n)) for _ in ws]
    return pl.pallas_call(
        functools.partial(_nm_mm_kernel, n_h=len(hs), mt_a=mt_a, n_g=len(ws)),
        out_shape=jax.ShapeDtypeStruct((n_rows, n_cols), out_dtype),
        grid=(n_rows // tm, pl.cdiv(n_cols, tn)),
        in_specs=in_specs,
        out_specs=pl.BlockSpec((tm, tn), lambda m, n: (m, n)),
        scratch_shapes=[pltpu.VMEM((tm, d), BF16)],
        compiler_params=_cparams(("arbitrary", "arbitrary"), VMEM_LIMIT_BIG),
        name="normmod_matmul",
    )(*hs, g.reshape(1, d), mod, mod, *ws)


def _final_norm_kernel(x_ref, g_ref, o_ref):
    x = x_ref[...]
    ms = jnp.mean(x * x, axis=-1, keepdims=True)
    o_ref[...] = x * lax.rsqrt(ms + EPS) * g_ref[...]


def _mm_kernel(te_ref, nu_ref, *refs, x_parts, res_parts, mt_a, n_g, epi):
    n_x = len(x_parts)
    refs = list(refs)

    def take(n):
        out = refs[:n]
        del refs[:n]
        return out

    x_refs = [take(p) for p in x_parts]
    w_refs = take(n_x * n_g)
    if epi == "resgate":
        res_refs = take(res_parts)
        gt_ref, = take(1)
    o_ref, = take(1)
    wc_refs = refs
    m = pl.program_id(1)

    def stacked(parts):
        if len(parts) == 1:
            return parts[0][...]
        return jnp.where(m < mt_a, parts[0][...], parts[1][...])

    panel_changed = (m == 0) | (te_ref[m] != te_ref[jnp.maximum(m - 1, 0)])

    @pl.when(panel_changed)
    def _():
        for w, wc in zip(w_refs, wc_refs):
            wc[...] = w[0].astype(BF16)

    @pl.when(m < nu_ref[0])
    def _():
        accs = []
        for g in range(n_g):
            acc = None
            for i in range(n_x):
                part = jnp.dot(stacked(x_refs[i]).astype(BF16), wc_refs[g * n_x + i][...],
                               preferred_element_type=F32)
                acc = part if acc is None else acc + part
            accs.append(acc)
        if epi == "plain":
            out = accs[0]
        elif epi == "packed":
            out = _pack_bf16_pairs(accs[0])
        elif epi == "swiglu":
            a = accs[0]
            out = (a * jax.nn.sigmoid(a)) * accs[1]
        else:
            out = stacked(res_refs) + gt_ref[0] * accs[0]
        o_ref[...] = out.astype(o_ref.dtype)

    @pl.when(m >= nu_ref[0])
    def _():
        o_ref[...] = jnp.zeros_like(o_ref)


def _matmul(xs, ws, *, n_rows, out_dtype, epi="plain", tile_expert=None, n_used=None,
            res=None, mod=None, gt_blk=None, mod_row=None, tm=TM, tn=TN):
    n_x, n_g = len(xs), len(ws)
    n_cols = ws[0].shape[2]
    tn = _pick(n_cols, (tn, 512, 256, 128))
    tm = _pick(n_rows, (tm, 256, 128))
    mt, nt = n_rows // tm, n_cols // tn
    if tile_expert is None:
        tile_expert = jnp.zeros((mt,), jnp.int32)
        n_used = jnp.full((1,), mt, jnp.int32)
    ks = [x[0].shape[1] for x in xs]
    in_specs, args, mt_a = [], [], None
    for x, k in zip(xs, ks):
        if len(x) == 1:
            in_specs.append(pl.BlockSpec((tm, k), lambda n, m, te, nu: (jnp.minimum(m, nu[0] - 1), 0)))
        else:
            specs, mt_a = _stacked_specs(x, (tm, k), tm, lambda idx: 0, 1)
            in_specs += specs
        args += list(x)
    for g in range(n_g):
        for i, k in enumerate(ks):
            in_specs.append(pl.BlockSpec((1, k, tn), lambda n, m, te, nu, i=i: (te[m], i, n)))
            args.append(ws[g])
    if epi == "resgate":
        blk0 = gt_blk * (n_cols // tn)
        specs, mt_res = _stacked_specs(res, (tm, tn), tm, lambda idx: idx[0], 1)
        mt_a = mt_res if mt_res is not None else mt_a
        in_specs += specs
        in_specs.append(pl.BlockSpec((1, 1, tn), lambda n, m, te, nu: (mod_row(m * tm), 0, blk0 + n)))
        args += list(res) + [mod]
    scratch = [pltpu.VMEM((k, tn), BF16) for _ in range(n_g) for k in ks]
    out_div = 2 if epi == "packed" else 1
    return pl.pallas_call(
        functools.partial(_mm_kernel, x_parts=tuple(len(x) for x in xs),
                          res_parts=len(res) if res is not None else 0, mt_a=mt_a, n_g=n_g, epi=epi),
        out_shape=jax.ShapeDtypeStruct((n_rows, n_cols // out_div), out_dtype),
        grid_spec=pltpu.PrefetchScalarGridSpec(
            num_scalar_prefetch=2,
            grid=(nt, mt),
            in_specs=in_specs,
            out_specs=pl.BlockSpec((tm, tn // out_div), lambda n, m, te, nu: (m, n)),
            scratch_shapes=scratch,
        ),
        compiler_params=_cparams(("arbitrary", "arbitrary")),
        name="matmul_" + epi,
    )(tile_expert, n_used, *args)


def _rope(x, cos, sin_signed, half):
    lane = lax.broadcasted_iota(jnp.int32, (x.shape[0], LANES), 1)
    first = (lane & (2 * half - 1)) < half
    blocks = []
    for j in range(x.shape[1] // LANES):
        xj = x[:, j * LANES:(j + 1) * LANES]
        partner = jnp.where(first, pltpu.roll(xj, LANES - half, 1), pltpu.roll(xj, half, 1))
        blocks.append(xj * cos + partner * sin_signed)
    return blocks[0] if len(blocks) == 1 else jnp.concatenate(blocks, axis=-1)


def _group_mean_sq(x, g_ref):
    x2 = x * x
    hi = x2.astype(BF16)
    lo = (x2 - hi.astype(F32)).astype(BF16)
    g = g_ref[...]
    return (jnp.dot(hi, g, preferred_element_type=F32) + jnp.dot(lo, g, preferred_element_type=F32))


def _dup_halves(x):
    lane = lax.broadcasted_iota(jnp.int32, x.shape, 1)
    lo = lane < HEAD_DIM
    swapped = pltpu.roll(x, HEAD_DIM, 1)
    return jnp.concatenate([jnp.where(lo, x, swapped), jnp.where(lo, swapped, x)], axis=-1)


def _store_transposed(dst_ref, x):
    for j in range(x.shape[1] // LANES):
        dst_ref[0, j * LANES:(j + 1) * LANES, :] = x[:, j * LANES:(j + 1) * LANES].T.astype(BF16)


def _proj_prep_kernel(*refs, n_h, mt_a):
    refs = list(refs)
    h_refs = [refs.pop(0) for _ in range(n_h)]
    (g_ref, sh_ref, sc_ref, w_ref, c64_ref, s64_ref, cq_ref, sq_ref, ck_ref, sk_ref,
     bqn_ref, bkn_ref, cqn_ref, ckvn_ref, g512_ref, g128_ref, wq_ref, wkk_ref, wkv_ref, place_ref,
     qa_ref, qb_ref, qc_ref, qd_ref,
     ka_ref, kb_ref, kc_ref, kd_ref, vd_ref, vta_ref, vtb_ref, vtc_ref, u_ref) = refs

    def fill(h_ref):
        u_ref[...] = _rms_modulate(h_ref[...], g_ref[...], sc_ref[0], sh_ref[0]).astype(BF16)

    if n_h == 1:
        fill(h_refs[0])
    else:
        t = pl.program_id(0)
        pl.when(t < mt_a)(lambda: fill(h_refs[0]))
        pl.when(t >= mt_a)(lambda: fill(h_refs[1]))

    p_all = jnp.dot(u_ref[...], w_ref[...], preferred_element_type=F32)

    def proj(off, width):
        return p_all[:, off:off + width]

    c64, s64 = c64_ref[...], s64_ref[...]
    c128, s128 = c64, s64
    scale64 = HEAD_DIM ** -0.5
    scale_mla = (C_NOPE_DIM + C_ROPE_DIM) ** -0.5

    qa_ref[...] = (_rope(proj(OFF_QA, 512), c64, s64, 16) * (scale64 * LOG2E)).astype(BF16)
    ka_ref[0] = _rope(proj(OFF_KA, 512), c64, s64, 16).astype(BF16)
    _store_transposed(vta_ref, proj(OFF_VA, 512))

    xb = proj(OFF_QB, 512)
    yb = xb * lax.rsqrt(_group_mean_sq(xb, g512_ref) + EPS) * bqn_ref[...]
    qb_ref[...] = (_rope(yb, c64, s64, 16) * (scale64 * LOG2E)).astype(BF16)
    kvb = proj(OFF_KB, 256)
    xk = kvb[:, :LANES]
    yk = xk * lax.rsqrt(_group_mean_sq(xk, g128_ref) + EPS) * bkn_ref[...]
    kb_ref[0] = _dup_halves(_rope(yk, c128, s128, 16)).astype(BF16)
    _store_transposed(vtb_ref, kvb[:, LANES:])

    xq = proj(OFF_QC, C_Q_RANK)
    yq = xq * lax.rsqrt(jnp.mean(xq * xq, axis=-1, keepdims=True) + EPS) * cqn_ref[...]
    qf = jnp.dot(yq.astype(BF16), wq_ref[...], preferred_element_type=F32)
    qc_ref[...] = (_rope(qf, cq_ref[...], sq_ref[...], 8) * (scale_mla * LOG2E)).astype(BF16)
    xc = proj(OFF_CKV, C_KV_RANK)
    yc = (xc * lax.rsqrt(jnp.mean(xc * xc, axis=-1, keepdims=True) + EPS) * ckvn_ref[...]).astype(BF16)
    tail = proj(OFF_KR, IN_COLS - OFF_KR)
    kr = _rope(tail[:, :LANES], ck_ref[...], sk_ref[...], 8).astype(BF16)
    kc = (jnp.dot(yc, wkk_ref[...], preferred_element_type=F32)
          + jnp.dot(kr, place_ref[...], preferred_element_type=F32))
    kc_ref[0] = kc.astype(BF16)
    _store_transposed(vtc_ref, jnp.dot(yc, wkv_ref[...], preferred_element_type=F32))

    qd_ref[...] = (_rope(proj(OFF_QD, 512), c64, s64, 16) * scale64).astype(BF16)
    kd0, vd0 = OFF_KD - OFF_KR, OFF_VD - OFF_KR
    kd_ref[0] = _dup_halves(_rope(tail[:, kd0:kd0 + LANES], c128, s128, 16)).astype(BF16)
    vd_ref[0] = _dup_halves(tail[:, vd0:vd0 + LANES]).astype(BF16)


def _proj_prep(hs, g, mod, w_in_bf16, tables, consts, dims, mod_row):
    bsz, s_len, n_ctx = dims
    d = hs[0].shape[1]
    n_rows = sum(a.shape[0] for a in hs)
    sk = n_ctx + s_len
    tr = _pick(math.gcd(s_len, n_ctx), (TR, 128))
    n_lat_t, lat_pb, ctx_pb = bsz * s_len // tr, s_len // tr, n_ctx // tr

    def is_lat(t):
        return t < n_lat_t

    def tbl_idx(t):
        return jnp.where(is_lat(t), t % lat_pb, lat_pb)

    def kv_b(t):
        return jnp.where(is_lat(t), t // lat_pb, (t - n_lat_t) // ctx_pb)

    def kv_j(t):
        return jnp.where(is_lat(t), ctx_pb + t % lat_pb, (t - n_lat_t) % ctx_pb)

    def row_spec(w):
        return pl.BlockSpec((tr, w), lambda t: (t, 0))

    def tbl_spec(w):
        return pl.BlockSpec((tr, w), lambda t: (tbl_idx(t), 0))

    def const_spec(a):
        return pl.BlockSpec(a.shape, lambda t: (0,) * a.ndim)

    def kv_spec(w):
        return pl.BlockSpec((1, tr, w), lambda t: (kv_b(t), kv_j(t), 0))

    def vt_spec(w):
        return pl.BlockSpec((1, w, tr), lambda t: (kv_b(t), 0, kv_j(t)))

    q_widths = (512, 512, 1024, 512)
    kv_widths = (512, 256, 1024, 256, 256)
    vt_widths = (512, 128, 512)
    out_shape = ([jax.ShapeDtypeStruct((n_rows, w), BF16) for w in q_widths]
                 + [jax.ShapeDtypeStruct((bsz, sk, w), BF16) for w in kv_widths]
                 + [jax.ShapeDtypeStruct((bsz, w, sk), BF16) for w in vt_widths])
    out_specs = ([row_spec(w) for w in q_widths] + [kv_spec(w) for w in kv_widths]
                 + [vt_spec(w) for w in vt_widths])
    h_specs, mt_a = _stacked_specs(hs, (tr, d), tr, lambda idx: 0, 0)
    in_specs = (h_specs
                + [pl.BlockSpec((1, d), lambda t: (0, 0)),
                   pl.BlockSpec((1, 1, d), lambda t: (mod_row(t * tr), 0, 0)),
                   pl.BlockSpec((1, 1, d), lambda t: (mod_row(t * tr), 0, 1)),
                   pl.BlockSpec(w_in_bf16.shape, lambda t: (0, 0), pipeline_mode=pl.Buffered(1))]
                + [tbl_spec(t.shape[1]) for t in tables] + [const_spec(a) for a in consts])
    return pl.pallas_call(
        functools.partial(_proj_prep_kernel, n_h=len(hs), mt_a=mt_a),
        out_shape=out_shape,
        grid=(n_rows // tr,),
        in_specs=in_specs,
        out_specs=out_specs,
        scratch_shapes=[pltpu.VMEM((tr, d), BF16)],
        compiler_params=_cparams(("arbitrary",), VMEM_LIMIT_BIG),
        name="proj_prep",
    )(*hs, g.reshape(1, d), mod, mod, w_in_bf16, *tables, *consts)


def _scores(a, b):
    return lax.dot_general(a, b, (((1,), (1,)), ((), ())), preferred_element_type=F32)


def _half_masks(shape):
    lane = lax.broadcasted_iota(jnp.int32, shape, 1)
    lo = lane < HEAD_DIM
    return lo, jnp.logical_not(lo)


def _softmax_pv_t(s, v_t):
    m = jnp.max(s, axis=0, keepdims=True)
    e = jnp.exp2(s - m).astype(BF16)
    dv = v_t.shape[0]
    v_ext = jnp.concatenate([v_t, jnp.ones((BF16_SUBLANES, v_t.shape[1]), BF16)], axis=0)
    oe = jnp.dot(v_ext, e, preferred_element_type=F32)
    return oe[:dv] / oe[dv:dv + 1]


def _run_units(score_fns, finish_fns):
    s = score_fns[0]()
    for i, finish in enumerate(finish_fns):
        s_next = score_fns[i + 1]() if i + 1 < len(score_fns) else None
        finish(s)
        s = s_next


def _gqa_t_kernel(q_ref, k_ref, vt_ref, o_ref):
    masks = _half_masks((q_ref.shape[0], LANES))
    n_pairs = q_ref.shape[1] // LANES
    pairs_per_group = B_HEADS // B_KV_HEADS // 2
    outs = {}

    def score(j, z):
        g = j // pairs_per_group
        q = q_ref[:, j * LANES:(j + 1) * LANES]
        return lambda: _scores(k_ref[0, :, g * LANES:(g + 1) * LANES],
                               jnp.where(masks[z], q, jnp.zeros_like(q)))

    def finish(j, z):
        g = j // pairs_per_group

        def fin(s):
            outs[z] = _softmax_pv_t(s, vt_ref[0, g * HEAD_DIM:(g + 1) * HEAD_DIM, :])
            if z == 1:
                pair = jnp.concatenate([outs[0], outs[1]], axis=0)
                o_ref[:, j * LANES:(j + 1) * LANES] = pair.T.astype(o_ref.dtype)
        return fin

    units = [(j, z) for j in range(n_pairs) for z in range(2)]
    _run_units([score(j, z) for j, z in units], [finish(j, z) for j, z in units])


def _mla_t_kernel(q_ref, k_ref, vt_ref, o_ref):
    n_heads = q_ref.shape[1] // LANES
    outs = {}

    def score(u):
        return lambda: _scores(k_ref[0, :, u * LANES:(u + 1) * LANES], q_ref[:, u * LANES:(u + 1) * LANES])

    def finish(u):
        def fin(s):
            outs[u % 2] = _softmax_pv_t(s, vt_ref[0, u * C_V_DIM:(u + 1) * C_V_DIM, :])
            if u % 2 == 1:
                pair = jnp.concatenate([outs[0], outs[1]], axis=0)
                j = u // 2
                o_ref[:, j * LANES:(j + 1) * LANES] = pair.T.astype(o_ref.dtype)
        return fin

    _run_units([score(u) for u in range(n_heads)], [finish(u) for u in range(n_heads)])


def _diff_t_kernel(lam_ref, subln_ref, q0_ref, q1_ref, k0_ref, k1_ref, vt_ref, o_ref, *, lam_init):
    t = lam_ref[...]
    lam = (jnp.exp(jnp.sum(t[0:1] * t[1:2], axis=-1, keepdims=True))
           - jnp.exp(jnp.sum(t[2:3] * t[3:4], axis=-1, keepdims=True)) + lam_init)
    q_refs, k_refs = (q0_ref, q1_ref), (k0_ref, k1_ref)
    masks = _half_masks((q0_ref.shape[0], LANES))
    first_map = {}

    def score(z, mp):
        j = z // 2
        q = q_refs[mp][:, j * LANES:(j + 1) * LANES]
        return lambda: _scores(k_refs[mp][0, :, j * LANES:(j + 1) * LANES],
                               jnp.where(masks[z % 2], q, jnp.zeros_like(q)))

    def finish(z, mp):
        def fin(s):
            a = _softmax_pv_t(s, vt_ref[0, z * A_V_DIM:(z + 1) * A_V_DIM, :])
            if mp == 0:
                first_map[z] = a
                return
            d = (first_map[z] - lam * a).T
            y = d * lax.rsqrt(jnp.mean(d * d, axis=-1, keepdims=True) + EPS) * subln_ref[...]
            o_ref[:, z * A_V_DIM:(z + 1) * A_V_DIM] = (y * (1.0 - lam_init)).astype(o_ref.dtype)
        return fin

    units = [(z, mp) for z in range(A_HEADS) for mp in range(2)]
    _run_units([score(z, mp) for z, mp in units], [finish(z, mp) for z, mp in units])


def _attend_sink(q, k, v, sink):
    s = _scores(q, k)
    m = jnp.maximum(jnp.max(s, axis=-1, keepdims=True), sink)
    e = jnp.exp(s - m)
    l = jnp.sum(e, axis=-1, keepdims=True) + jnp.exp(sink - m)
    return jnp.dot(e.astype(BF16), v, preferred_element_type=F32) / l


def _window_kernel(sink_ref, q_ref, k_ref, v_ref, o_ref, *, s_len, n_ctx, n_q_lat, with_ctx):
    tq = q_ref.shape[0]
    band = tq + 2 * WINDOW
    i = pl.program_id(1)
    pairs_per_group = D_HEADS // D_KV_HEADS // 2

    def body(is_lat):
        masks = _half_masks((tq, LANES))
        if is_lat:
            start = pl.multiple_of(jnp.clip(i * tq - WINDOW, 0, s_len - band), LANES)
            row0 = pl.multiple_of(n_ctx + start, LANES)
            qpos = i * tq + lax.broadcasted_iota(jnp.int32, (tq, n_ctx + band), 0)
            col = lax.broadcasted_iota(jnp.int32, (tq, n_ctx + band), 1)
            visible = (col < n_ctx) | (jnp.abs(qpos - (start + col - n_ctx)) <= WINDOW)
        kv = {}
        for j in range(q_ref.shape[1] // LANES):
            g = j // pairs_per_group
            if g not in kv:
                lanes = slice(g * LANES, (g + 1) * LANES)
                k, v = k_ref[0, :n_ctx, lanes], v_ref[0, :n_ctx, lanes]
                if is_lat:
                    k = jnp.concatenate([k, k_ref[0, pl.ds(row0, band), lanes]], axis=0)
                    v = jnp.concatenate([v, v_ref[0, pl.ds(row0, band), lanes]], axis=0)
                kv[g] = (k, v)
            k, v = kv[g]
            q = q_ref[:, j * LANES:(j + 1) * LANES]
            outs = []
            for z in range(2):
                qz = jnp.where(masks[z], q, jnp.zeros_like(q))
                sink = sink_ref[2 * j + z]
                if not is_lat:
                    outs.append(_attend_sink(qz, k, v, sink))
                    continue
                s = jnp.where(visible, _scores(qz, k), -jnp.inf)
                m = jnp.maximum(jnp.max(s, axis=-1, keepdims=True), sink)
                e = jnp.exp(s - m)
                l = jnp.sum(e, axis=-1, keepdims=True) + jnp.exp(sink - m)
                outs.append(jnp.dot(e.astype(BF16), v, preferred_element_type=F32) / l)
            o_ref[:, j * LANES:(j + 1) * LANES] = jnp.where(masks[0], outs[0], outs[1]).astype(o_ref.dtype)

    if not with_ctx:
        body(True)
        return

    @pl.when(i < n_q_lat)
    def _():
        body(True)

    @pl.when(i >= n_q_lat)
    def _():
        body(False)


def _mixers(ops, dims, params, lam_init, with_ctx):
    bsz, s_len, n_ctx = dims
    qa, qb, qc, qd, ka, kb, kc, kd, vd, vta, vtb, vtc = ops
    lam_vecs, subln, sink = params
    n_lat = bsz * s_len
    sk_all = n_ctx + s_len

    def const_spec(a):
        return pl.BlockSpec(a.shape, lambda b, i: (0,) * a.ndim)

    def dense_set(tq, nk, row0_tiles, n_q, out_rows):
        def q_row(b, i):
            return row0_tiles + b * n_q + i

        def qspec(w, col=0):
            return pl.BlockSpec((tq, w), lambda b, i: (q_row(b, i), col))

        def kspec(w, col=0):
            return pl.BlockSpec((1, nk, w), lambda b, i: (b, 0, col))

        def vtspec(w):
            return pl.BlockSpec((1, w, nk), lambda b, i: (b, 0, 0))

        def call(kernel, in_specs, args, name):
            return pl.pallas_call(
                kernel,
                out_shape=jax.ShapeDtypeStruct((out_rows, 512), BF16),
                grid=(bsz, n_q),
                in_specs=in_specs,
                out_specs=pl.BlockSpec((tq, 512), lambda b, i: (b * n_q + i, 0)),
                compiler_params=_cparams(("arbitrary", "arbitrary"), VMEM_LIMIT_BIG),
                name=name,
            )(*args)

        oa = call(functools.partial(_diff_t_kernel, lam_init=lam_init),
                  [const_spec(lam_vecs), const_spec(subln), qspec(256, 0), qspec(256, 1),
                   kspec(256, 0), kspec(256, 1), vtspec(512)],
                  [lam_vecs, subln, qa, qa, ka, ka, vta], "attn_diff")
        ob = call(_gqa_t_kernel, [qspec(512), kspec(256), vtspec(128)], [qb, kb, vtb], "attn_qknorm")
        oc = call(_mla_t_kernel, [qspec(1024), kspec(1024), vtspec(512)], [qc, kc, vtc], "attn_mla")
        return [oa, ob, oc]

    tq = _pick(s_len, (TQ, 256, 128))
    dense = [[o] for o in dense_set(tq, sk_all, 0, s_len // tq, n_lat)]
    if with_ctx:
        tq_c = _pick(n_ctx, (TQ, 256, 128))
        ctx = dense_set(tq_c, n_ctx, n_lat // tq_c, n_ctx // tq_c, bsz * n_ctx)
        dense = [a + [b] for a, b in zip(dense, ctx)]

    tw = _pick(math.gcd(s_len, n_ctx), (TQ_WIN, 128))
    n_q_lat, n_q_ctx = s_len // tw, n_ctx // tw
    n_q = n_q_lat + n_q_ctx if with_ctx else n_q_lat
    out_rows = n_lat + bsz * n_ctx if with_ctx else n_lat

    def w_row(b, i):
        return jnp.where(i < n_q_lat, b * n_q_lat + i, n_lat // tw + b * n_q_ctx + (i - n_q_lat))

    od = pl.pallas_call(
        functools.partial(_window_kernel, s_len=s_len, n_ctx=n_ctx, n_q_lat=n_q_lat, with_ctx=with_ctx),
        out_shape=jax.ShapeDtypeStruct((out_rows, 512), BF16),
        grid=(bsz, n_q),
        in_specs=[pl.BlockSpec(memory_space=pltpu.SMEM),
                  pl.BlockSpec((tw, 512), lambda b, i: (w_row(b, i), 0)),
                  pl.BlockSpec((1, sk_all, 256), lambda b, i: (b, 0, 0)),
                  pl.BlockSpec((1, sk_all, 256), lambda b, i: (b, 0, 0))],
        out_specs=pl.BlockSpec((tw, 512), lambda b, i: (w_row(b, i), 0)),
        compiler_params=_cparams(("arbitrary", "arbitrary")),
        name="attn_window",
    )(sink, qd, kd, vd)
    return dense + [[od]]


def _route_kernel(h_ref, g_ref, sh_ref, sc_ref, w_ref, b_ref, u_ref, sel_ref, idx_ref, gw_ref):
    u = _rms_modulate(h_ref[...], g_ref[...], sc_ref[0], sh_ref[0])
    u_ref[...] = _pack_bf16_pairs(u)
    w = w_ref[...]
    u_hi = u.astype(BF16)
    u_lo = (u - u_hi.astype(F32)).astype(BF16)
    w_hi = w.astype(BF16)
    w_lo = (w - w_hi.astype(F32)).astype(BF16)
    logits = (_scores(w_hi, u_hi) + _scores(w_hi, u_lo) + _scores(w_lo, u_hi) + _scores(w_lo, u_lo)
              + b_ref[...])
    ids = lax.broadcasted_iota(jnp.int32, logits.shape, 0).astype(F32)
    m1 = jnp.max(logits, axis=0, keepdims=True)
    i1 = jnp.min(jnp.where(logits == m1, ids, float(N_EXPERTS)), axis=0, keepdims=True)
    first = ids == i1
    rest = jnp.where(first, -jnp.inf, logits)
    m2 = jnp.max(rest, axis=0, keepdims=True)
    i2 = jnp.min(jnp.where(rest == m2, ids, float(N_EXPERTS)), axis=0, keepdims=True)
    second = ids == i2
    e = jnp.exp(m2 - m1)
    w1 = 1.0 / (1.0 + e)
    w2 = e / (1.0 + e)
    sel_ref[...] = jnp.where(first | second, 1, 0).astype(jnp.int32)
    idx_ref[...] = jnp.where(ids == 0.0, i1, jnp.where(ids == 1.0, i2, 0.0)).astype(jnp.int32)
    gw_ref[...] = jnp.where(ids == 0.0, w1, jnp.where(ids == 1.0, w2, 0.0))


def _route(h, g, mod, sh_blk, sc_blk, w_router_t, b_router, n_tok, s_len, mod_row):
    d = h.shape[1]
    tm = _row_tile(n_tok, s_len, (TM, 256, 128))
    outs = ([jax.ShapeDtypeStruct((n_tok, d // 2), jnp.uint32)]
            + [jax.ShapeDtypeStruct((N_EXPERTS, n_tok), dt) for dt in (jnp.int32, jnp.int32, F32)])
    return pl.pallas_call(
        _route_kernel,
        out_shape=outs,
        grid=(n_tok // tm,),
        in_specs=[pl.BlockSpec((tm, d), lambda m: (m, 0)),
                  pl.BlockSpec((1, d), lambda m: (0, 0)),
                  pl.BlockSpec((1, 1, d), lambda m: (mod_row(m * tm), 0, sh_blk)),
                  pl.BlockSpec((1, 1, d), lambda m: (mod_row(m * tm), 0, sc_blk)),
                  pl.BlockSpec((N_EXPERTS, d), lambda m: (0, 0)),
                  pl.BlockSpec((N_EXPERTS, 1), lambda m: (0, 0))],
        out_specs=[pl.BlockSpec((tm, d // 2), lambda m: (m, 0))]
        + [pl.BlockSpec((N_EXPERTS, tm), lambda m: (0, m))] * 3,
        compiler_params=_cparams(("arbitrary",)),
        name="moe_route",
    )(h, g.reshape(1, d), mod, mod, w_router_t, b_router.reshape(N_EXPERTS, 1))


def _row_copy(src, dst, sem, src_row, dst_row):
    return pltpu.make_async_copy(src.at[pl.ds(src_row, 1)], dst.at[pl.ds(dst_row, 1)], sem)


def _gather_kernel(tok_ref, nxt_ref, src_ref, o_ref, buf, sem):
    i = pl.program_id(0)
    n = pl.num_programs(0)
    tg = buf.shape[1]
    slot = i % 2

    def issue(t_ref, s):
        def body(r, carry):
            _row_copy(src_ref, buf.at[s], sem.at[s], t_ref[0, 0, r], r).start()
            return carry
        lax.fori_loop(0, tg, body, 0, unroll=DMA_UNROLL)

    @pl.when(i == 0)
    def _():
        issue(tok_ref, 0)

    @pl.when(i + 1 < n)
    def _():
        issue(nxt_ref, 1 - slot)

    def wait(r, carry):
        _row_copy(src_ref, buf.at[slot], sem.at[slot], 0, r).wait()
        return carry

    lax.fori_loop(0, tg, wait, 0, unroll=DMA_UNROLL)
    lo, hi = _unpack_bf16_pairs(buf[slot])
    half = lo.shape[1]
    o_ref[:, :half] = lo.astype(o_ref.dtype)
    o_ref[:, half:] = hi.astype(o_ref.dtype)


def _gather_rows(src, tok, tg):
    n_rows = tok.shape[0]
    dw = src.shape[1]
    d = 2 * dw
    n_t = n_rows // tg
    tok = tok.reshape(n_t, 1, tg)
    return pl.pallas_call(
        _gather_kernel,
        out_shape=jax.ShapeDtypeStruct((n_rows, d), BF16),
        grid=(n_t,),
        in_specs=[pl.BlockSpec((1, 1, tg), lambda i: (i, 0, 0), memory_space=pltpu.SMEM),
                  pl.BlockSpec((1, 1, tg), lambda i: (jnp.minimum(i + 1, n_t - 1), 0, 0),
                               memory_space=pltpu.SMEM),
                  pl.BlockSpec(memory_space=pl.ANY)],
        out_specs=pl.BlockSpec((tg, d), lambda i: (i, 0)),
        scratch_shapes=[pltpu.VMEM((2, tg, dw), jnp.uint32), pltpu.SemaphoreType.DMA((2,))],
        compiler_params=_cparams(("arbitrary",)),
        name="moe_gather",
    )(tok, tok, src)


def _combine_kernel(pos_ref, nxt_ref, y_ref, h_ref, gw_ref, gt_ref, gf_ref, o_ref, buf, sem, *, pack_tile):
    i = pl.program_id(0)
    n = pl.num_programs(0)
    tc = buf.shape[2]
    slot = i % 2

    def issue(p_ref, s):
        def body(r, carry):
            _row_copy(y_ref, buf.at[s, 0], sem.at[s], p_ref[0, 0, r], r).start()
            _row_copy(y_ref, buf.at[s, 1], sem.at[s], p_ref[0, 1, r], r).start()
            return carry
        lax.fori_loop(0, tc, body, 0, unroll=DMA_UNROLL)

    @pl.when(i == 0)
    def _():
        issue(pos_ref, 0)

    @pl.when(i + 1 < n)
    def _():
        issue(nxt_ref, 1 - slot)

    def wait(r, carry):
        _row_copy(y_ref, buf.at[slot, 0], sem.at[slot], 0, r).wait()
        _row_copy(y_ref, buf.at[slot, 1], sem.at[slot], 0, r).wait()
        return carry

    lax.fori_loop(0, tc, wait, 0, unroll=DMA_UNROLL)
    def expert_rows(k):
        lo, hi = _unpack_bf16_pairs(buf[slot, k])
        hw = pack_tile // 2
        cols = []
        for j in range(lo.shape[1] // hw):
            cols += [lo[:, j * hw:(j + 1) * hw], hi[:, j * hw:(j + 1) * hw]]
        return jnp.concatenate(cols, axis=-1)

    gw = gw_ref[...]
    moe = gw[:, 0:1] * expert_rows(0) + gw[:, 1:2] * expert_rows(1)
    x = h_ref[...] + gt_ref[0] * moe
    ms = jnp.mean(x * x, axis=-1, keepdims=True)
    o_ref[...] = x * lax.rsqrt(ms + EPS) * gf_ref[...]


def _combine(y, h, pos, gw_t, mod, gt_blk, mod_row, g_final, n_tok, pack_tile):
    d = h.shape[1]
    n_t, _, tc = pos.shape
    return pl.pallas_call(
        functools.partial(_combine_kernel, pack_tile=pack_tile),
        out_shape=jax.ShapeDtypeStruct((n_tok, d), F32),
        grid=(n_t,),
        in_specs=[pl.BlockSpec((1, 2, tc), lambda i: (i, 0, 0), memory_space=pltpu.SMEM),
                  pl.BlockSpec((1, 2, tc), lambda i: (jnp.minimum(i + 1, n_t - 1), 0, 0),
                               memory_space=pltpu.SMEM),
                  pl.BlockSpec(memory_space=pl.ANY),
                  pl.BlockSpec((tc, d), lambda i: (i, 0)),
                  pl.BlockSpec((tc, N_EXPERTS), lambda i: (i, 0)),
                  pl.BlockSpec((1, 1, d), lambda i: (mod_row(i * tc), 0, gt_blk)),
                  pl.BlockSpec((1, d), lambda i: (0, 0))],
        out_specs=pl.BlockSpec((tc, d), lambda i: (i, 0)),
        scratch_shapes=[pltpu.VMEM((2, 2, tc, d // 2), jnp.uint32), pltpu.SemaphoreType.DMA((2,))],
        compiler_params=_cparams(("arbitrary",)),
        name="moe_combine",
    )(pos, pos, y, h, gw_t, mod, g_final.reshape(1, d))


def _dispatch_plan(sel, idx, tm, n_slots, tc):
    n_tok = sel.shape[1]
    n_tiles = n_slots // tm
    counts = jnp.sum(sel, axis=1)
    padded = ((counts + tm - 1) // tm) * tm
    ends = jnp.cumsum(padded)
    offs = ends - padded
    pos = offs[:, None] + jnp.cumsum(sel, axis=1) - sel
    pos0 = jnp.take_along_axis(pos, idx[0:1], axis=0)[0]
    pos1 = jnp.take_along_axis(pos, idx[1:2], axis=0)[0]
    pos_tiles = jnp.stack([pos0.reshape(n_tok // tc, tc), pos1.reshape(n_tok // tc, tc)], axis=1)
    tile_start = jnp.arange(n_tiles, dtype=jnp.int32) * tm
    tile_expert = jnp.minimum(jnp.sum(ends[None, :] <= tile_start[:, None], axis=1), N_EXPERTS - 1)
    n_used = ends[-1] // tm
    tok = jnp.arange(n_tok, dtype=jnp.int32)
    tok_of_row = jnp.zeros((n_slots,), jnp.int32).at[jnp.concatenate([pos0, pos1])].set(
        jnp.concatenate([tok, tok]), unique_indices=True)
    return (pos_tiles.astype(jnp.int32), tok_of_row, tile_expert.astype(jnp.int32),
            n_used.reshape(1).astype(jnp.int32))


def _rope_tables(s_len, pad_rows):
    f32 = np.float32
    t = np.arange(s_len)
    rows, cols = (t // GRID_W).astype(f32), (t % GRID_W).astype(f32)

    def axis_tables(rot_dim):
        axis_dim = rot_dim // 2
        inv = (f32(ROPE_THETA) ** (-np.arange(0, axis_dim, 2, dtype=f32) / f32(axis_dim))).astype(f32)
        ar, ac = rows[:, None] * inv[None, :], cols[:, None] * inv[None, :]
        cos = np.concatenate([np.cos(ar), np.cos(ar), np.cos(ac), np.cos(ac)], axis=1)
        sin = np.concatenate([-np.sin(ar), np.sin(ar), -np.sin(ac), np.sin(ac)], axis=1)
        return cos.astype(f32), sin.astype(f32)

    def with_identity(cos, sin):
        w = cos.shape[1]
        return (np.concatenate([cos, np.ones((pad_rows, w), f32)], axis=0),
                np.concatenate([sin, np.zeros((pad_rows, w), f32)], axis=0))

    c64, s64 = axis_tables(HEAD_DIM)
    c64, s64 = np.tile(c64, (1, 2)), np.tile(s64, (1, 2))
    c32, s32 = axis_tables(C_ROPE_DIM)
    ones, zeros = np.ones((s_len, 1), f32), np.zeros((s_len, 1), f32)
    cq = np.concatenate([np.tile(ones, (1, 64)), c32, np.tile(ones, (1, 32))], axis=1)
    sq = np.concatenate([np.tile(zeros, (1, 64)), s32, np.tile(zeros, (1, 32))], axis=1)
    ck = np.concatenate([c32, np.tile(ones, (1, 96))], axis=1)
    sk = np.concatenate([s32, np.tile(zeros, (1, 96))], axis=1)
    out = []
    for c, s in ((c64, s64), (cq, sq), (ck, sk)):
        out += [jnp.asarray(a) for a in with_identity(c, s)]
    return out


def _mla_weights(w_q_up, w_kv_up):
    qd = C_NOPE_DIM + C_ROPE_DIM
    wq = jnp.pad(w_q_up.reshape(C_Q_RANK, C_HEADS, qd), ((0, 0), (0, 0), (0, LANES - qd)))
    wkv = w_kv_up.reshape(C_KV_RANK, C_HEADS, C_NOPE_DIM + C_V_DIM)
    wkk = jnp.pad(wkv[:, :, :C_NOPE_DIM], ((0, 0), (0, 0), (0, LANES - C_NOPE_DIM)))
    wkv_v = wkv[:, :, C_NOPE_DIM:]
    return (wq.reshape(C_Q_RANK, C_HEADS * LANES).astype(BF16),
            wkk.reshape(C_KV_RANK, C_HEADS * LANES).astype(BF16),
            wkv_v.reshape(C_KV_RANK, C_HEADS * C_V_DIM).astype(BF16))


def _static_mats():
    g = (np.arange(512)[:, None] // HEAD_DIM == np.arange(512)[None, :] // HEAD_DIM) / HEAD_DIM
    place = np.zeros((LANES, C_HEADS * LANES), np.float32)
    for h in range(C_HEADS):
        place[np.arange(C_ROPE_DIM), h * LANES + C_NOPE_DIM + np.arange(C_ROPE_DIM)] = 1.0
    return (jnp.asarray(g, BF16), jnp.asarray(g[:128, :128], BF16), jnp.asarray(place, BF16))


def kernel(x, c, ctx, c_ctx, w_mod, b_mod, g_mix, g_ffn, g_final, w_in, w_out, a_lam_q1, a_lam_k1, a_lam_q2, a_lam_k2, a_subln, b_q_norm, b_k_norm, c_q_norm, c_kv_norm, c_w_q_up, c_w_kv_up, d_sink, ffn_w_gate, ffn_w_up, ffn_w_down, moe_w_router, moe_b_router, moe_w_gate, moe_w_up, moe_w_down):
    bsz, s_len, d = x.shape
    n_ctx = ctx.shape[1]
    depth = w_mod.shape[0]
    n_lat = bsz * s_len
    n_all = n_lat + bsz * n_ctx
    dims = (bsz, s_len, n_ctx)

    def mod_row(row0):
        return jnp.where(row0 < n_lat, row0 // s_len, bsz)

    mod_rows = -(-(bsz + 1) // 8) * 8
    cc = jnp.zeros((mod_rows, d), F32).at[:bsz].set(c).at[bsz].set(c_ctx)
    mods = _modulation(cc, w_mod, b_mod)

    tr = _pick(math.gcd(s_len, n_ctx), (TR, 128))
    tables = _rope_tables(s_len, tr)
    g512, g128, place = _static_mats()

    h = [x.reshape(n_lat, d), ctx.reshape(bsz * n_ctx, d)]
    for l in range(depth):
        last = l == depth - 1
        lam_init = 0.8 - 0.6 * math.exp(-0.3 * l)
        n_rows = n_lat if last else n_all
        mod = mods[l].reshape(mod_rows, 1, 6 * d)
        nm = dict(s_len=s_len, mod_row=mod_row)

        wq, wkk, wkv_v = _mla_weights(c_w_q_up[l], c_w_kv_up[l])
        consts = [jnp.tile(b_q_norm[l], 8)[None], jnp.tile(b_k_norm[l], 2)[None],
                  c_q_norm[l][None], c_kv_norm[l][None], g512, g128, wq, wkk, wkv_v, place]
        ops = _proj_prep(h, g_mix[l], mod, w_in[l].astype(BF16), tables, consts, dims, mod_row)
        lam_vecs = jnp.stack([a_lam_q1[l], a_lam_k1[l], a_lam_q2[l], a_lam_k2[l]])
        mix = _mixers(ops, dims, (lam_vecs, a_subln[l][None], d_sink[l]), lam_init, not last)
        h = [_matmul(mix, [w_out[l][None]], n_rows=n_rows, out_dtype=F32, epi="resgate",
                     res=h, mod=mod, gt_blk=2, mod_row=mod_row, tn=1024)]

        i = l // 2
        if l % 2 == 0:
            mid = _nm_matmul(h, g_ffn[l], mod, 3, 4, [ffn_w_gate[i], ffn_w_up[i]], n_rows=n_rows,
                             out_dtype=BF16, **nm)
            h = [_matmul([[mid]], [ffn_w_down[i][None]], n_rows=n_rows, out_dtype=F32, epi="resgate",
                         res=h, mod=mod, gt_blk=5, mod_row=mod_row)]
        else:
            if not last:
                raise NotImplementedError("expert layers are only supported as the last layer")
            h = h[0]
            u2, sel, idx, gw = _route(h, g_ffn[l], mod, 3, 4, moe_w_router[i].T, moe_b_router[i],
                                      n_rows, s_len, mod_row)
            tm = _pick(n_rows, (TM, 256, 128))
            tc = _pick(s_len, (TC, 128))
            n_slots = 2 * n_rows + N_EXPERTS * tm
            pos, tok_of_row, tile_expert, n_used = _dispatch_plan(sel, idx, tm, n_slots, tc)
            xs = _gather_rows(u2, tok_of_row, tm)
            mid = _matmul([[xs]], [moe_w_gate[i], moe_w_up[i]], n_rows=n_slots, out_dtype=BF16,
                          epi="swiglu", tile_expert=tile_expert, n_used=n_used)
            y = _matmul([[mid]], [moe_w_down[i]], n_rows=n_slots, out_dtype=jnp.uint32, epi="packed",
                        tile_expert=tile_expert, n_used=n_used)
            out = _combine(y, h, pos, gw.T, mod, 5, mod_row, g_final, n_rows, _pick(d, (TN, 256, 128)))
            return out.reshape(bsz, s_len, d)

    tm = _pick(n_lat, (TM, 256, 128))
    out = pl.pallas_call(
        _final_norm_kernel,
        out_shape=jax.ShapeDtypeStruct((n_lat, d), F32),
        grid=(n_lat // tm,),
        in_specs=[pl.BlockSpec((tm, d), lambda m: (m, 0)), pl.BlockSpec((1, d), lambda m: (0, 0))],
        out_specs=pl.BlockSpec((tm, d), lambda m: (m, 0)),
        compiler_params=_cparams(("arbitrary",)),
        name="final_norm",
    )(h[0], g_final.reshape(1, d))
    return out.reshape(bsz, s_len, d)
```

```python
import functools
import math

import numpy as np
import jax
import jax.numpy as jnp
from jax import lax
from jax.experimental import pallas as pl
from jax.experimental.pallas import tpu as pltpu

F32 = jnp.float32
BF16 = jnp.bfloat16

GRID_W = 64
HEAD_DIM = 64
ROPE_THETA = 10000.0
EPS = 1e-6
A_HEADS, A_QK_DIM, A_V_DIM = 4, 64, 128
B_HEADS, B_KV_HEADS = 8, 2
C_HEADS, C_Q_RANK, C_KV_RANK, C_NOPE_DIM, C_ROPE_DIM, C_V_DIM = 8, 768, 256, 64, 32, 64
D_HEADS, D_KV_HEADS = 8, 2
WINDOW = 128
N_EXPERTS = 8
LOG2E = 1.4426950408889634

LANES = 128
BF16_SUBLANES = 16
VMEM_LIMIT = 48 * 2**20
VMEM_LIMIT_BIG = 56 * 2**20

TM = 512
TM_X = 1024
TN = 512
TR = 256
TQ = 512
TQ_WIN = 256
TC = 256
DMA_UNROLL = 8

OFF_QA, OFF_QB, OFF_QC, OFF_QD = 0, 512, 1024, 1792
OFF_KA, OFF_VA, OFF_KB, OFF_VB = 2304, 2816, 3328, 3456
OFF_CKV, OFF_KR, OFF_KD, OFF_VD = 3584, 3840, 3872, 4000
IN_COLS = 4128


def _cparams(sem, vmem=VMEM_LIMIT):
    return pltpu.CompilerParams(dimension_semantics=sem, vmem_limit_bytes=vmem)


def _pick(n, prefs):
    for p in prefs:
        if n % p == 0:
            return p
    return n


def _row_tile(n_rows, s_len, prefs):
    for p in prefs:
        if n_rows % p == 0 and s_len % p == 0:
            return p
    raise ValueError("no row tile fits")


def _pack_bf16_pairs(x):
    w = x.shape[1] // 2
    xb = x.astype(BF16).astype(F32)
    return (pltpu.bitcast(xb[:, :w], jnp.uint32) >> 16) | pltpu.bitcast(xb[:, w:], jnp.uint32)


def _unpack_bf16_pairs(words):
    return (pltpu.bitcast(words << 16, F32), pltpu.bitcast(words & jnp.uint32(0xFFFF0000), F32))


def _rms_modulate(x, g, sc, sh):
    ms = jnp.mean(x * x, axis=-1, keepdims=True)
    return (x * lax.rsqrt(ms + EPS) * g) * (1.0 + sc) + sh


def _mod_kernel(c_ref, w_ref, b_ref, o_ref):
    a = c_ref[...]
    a = (a * jax.nn.sigmoid(a)).astype(BF16)
    o_ref[0] = jnp.dot(a, w_ref[0].astype(BF16), preferred_element_type=F32) + b_ref[0]


def _modulation(cc, w_mod, b_mod):
    depth, d, d6 = w_mod.shape
    rows = cc.shape[0]
    tn = _pick(d6, (1024, 512, 256, 128))
    return pl.pallas_call(
        _mod_kernel,
        out_shape=jax.ShapeDtypeStruct((depth, rows, d6), F32),
        grid=(depth, d6 // tn),
        in_specs=[
            pl.BlockSpec((rows, d), lambda l, n: (0, 0)),
            pl.BlockSpec((1, d, tn), lambda l, n: (l, 0, n)),
            pl.BlockSpec((1, 1, tn), lambda l, n: (l, 0, n)),
        ],
        out_specs=pl.BlockSpec((1, rows, tn), lambda l, n: (l, 0, n)),
        compiler_params=_cparams(("arbitrary", "arbitrary")),
        name="modulation",
    )(cc, w_mod, b_mod.reshape(depth, 1, d6))


def _stacked_specs(parts, block, tile_rows, col_of, n_lead):
    def grid_m(idx):
        return idx[n_lead]

    if len(parts) == 1:
        return [pl.BlockSpec(block, lambda *idx: (grid_m(idx), col_of(idx)))], None
    mt_a = parts[0].shape[0] // tile_rows
    return ([pl.BlockSpec(block, lambda *idx: (jnp.minimum(grid_m(idx), mt_a - 1), col_of(idx))),
             pl.BlockSpec(block, lambda *idx: (jnp.maximum(grid_m(idx) - mt_a, 0), col_of(idx)))], mt_a)


def _nm_mm_kernel(*refs, n_h, mt_a, n_g):
    h_refs = refs[:n_h]
    g_ref, sh_ref, sc_ref = refs[n_h:n_h + 3]
    w_refs, o_ref, u_ref = refs[n_h + 3:n_h + 3 + n_g], refs[n_h + 3 + n_g], refs[n_h + 4 + n_g]
    m = pl.program_id(0)

    def fill(h_ref):
        u_ref[...] = _rms_modulate(h_ref[...], g_ref[...], sc_ref[0], sh_ref[0]).astype(BF16)

    first = pl.program_id(1) == 0
    if n_h == 1:
        pl.when(first)(lambda: fill(h_refs[0]))
    else:
        pl.when(first & (m < mt_a))(lambda: fill(h_refs[0]))
        pl.when(first & (m >= mt_a))(lambda: fill(h_refs[1]))

    u = u_ref[...]
    accs = [jnp.dot(u, w[...].astype(BF16), preferred_element_type=F32) for w in w_refs]
    out = accs[0] if n_g == 1 else (accs[0] * jax.nn.sigmoid(accs[0])) * accs[1]
    o_ref[...] = out.astype(o_ref.dtype)


def _nm_matmul(hs, g, mod, sh_blk, sc_blk, ws, *, n_rows, s_len, mod_row, out_dtype):
    d = hs[0].shape[1]
    n_cols = ws[0].shape[1]
    tm = _row_tile(n_rows, s_len, (TM_X, 512, 256, 128))
    tn = TN
    h_specs, mt_a = _stacked_specs(hs, (tm, d), tm, lambda idx: 0, 0)
    in_specs = h_specs + [
        pl.BlockSpec((1, d), lambda m, n: (0, 0)),
        pl.BlockSpec((1, 1, d), lambda m, n: (mod_row(m * tm), 0, sh_blk)),
        pl.BlockSpec((1, 1, d), lambda m, n: (mod_row(m * tm), 0, sc_blk)),
    ] + [pl.BlockSpec((d, tn), lambda m, n: (0, n)) for _ in ws]
    return pl.pallas_call(
        functools.partial(_nm_mm_kernel, n_h=len(hs), mt_a=mt_a, n_g=len(ws)),
        out_shape=jax.ShapeDtypeStruct((n_rows, n_cols), out_dtype),
        grid=(n_rows // tm, pl.cdiv(n_cols, tn)),
        in_specs=in_specs,
        out_specs=pl.BlockSpec((tm, tn), lambda m, n: (m, n)),
        scratch_shapes=[pltpu.VMEM((tm, d), BF16)],
        compiler_params=_cparams(("arbitrary", "arbitrary"), VMEM_LIMIT_BIG),
        name="normmod_matmul",
    )(*hs, g.reshape(1, d), mod, mod, *ws)


def _final_norm_kernel(x_ref, g_ref, o_ref):
    x = x_ref[...]
    ms = jnp.mean(x * x, axis=-1, keepdims=True)
    o_ref[...] = x * lax.rsqrt(ms + EPS) * g_ref[...]


def _mm_kernel(te_ref, nu_ref, *refs, x_parts, res_parts, mt_a, n_g, epi):
    n_x = len(x_parts)
    refs = list(refs)

    def take(n):
        out = refs[:n]
        del refs[:n]
        return out

    x_refs = [take(p) for p in x_parts]
    w_refs = take(n_x * n_g)
    if epi == "resgate":
        res_refs = take(res_parts)
        gt_ref, = take(1)
    o_ref, = take(1)
    wc_refs = refs
    m = pl.program_id(1)

    def stacked(parts):
        if len(parts) == 1:
            return parts[0][...]
        return jnp.where(m < mt_a, parts[0][...], parts[1][...])

    panel_changed = (m == 0) | (te_ref[m] != te_ref[jnp.maximum(m - 1, 0)])

    ks = [parts[0].shape[1] for parts in x_refs]
    offs = [sum(ks[:i]) for i in range(n_x)]

    @pl.when(panel_changed)
    def _():
        for g in range(n_g):
            for i in range(n_x):
                wc_refs[g][offs[i]:offs[i] + ks[i], :] = w_refs[g * n_x + i][0].astype(BF16)

    @pl.when(m < nu_ref[0])
    def _():
        xs = [stacked(parts).astype(BF16) for parts in x_refs]
        x = xs[0] if n_x == 1 else jnp.concatenate(xs, axis=1)
        accs = [jnp.dot(x, wc_refs[g][...], preferred_element_type=F32) for g in range(n_g)]
        if epi == "plain":
            out = accs[0]
        elif epi == "packed":
            out = _pack_bf16_pairs(accs[0])
        elif epi == "swiglu":
            a = accs[0]
            out = (a * jax.nn.sigmoid(a)) * accs[1]
        else:
            out = stacked(res_refs) + gt_ref[0] * accs[0]
        o_ref[...] = out.astype(o_ref.dtype)

    @pl.when(m >= nu_ref[0])
    def _():
        o_ref[...] = jnp.zeros_like(o_ref)


def _matmul(xs, ws, *, n_rows, out_dtype, epi="plain", tile_expert=None, n_used=None,
            res=None, mod=None, gt_blk=None, mod_row=None, tm=TM, tn=TN):
    n_x, n_g = len(xs), len(ws)
    n_cols = ws[0].shape[2]
    tn = _pick(n_cols, (tn, 512, 256, 128))
    tm = _pick(n_rows, (tm, 256, 128))
    mt, nt = n_rows // tm, n_cols // tn
    if tile_expert is None:
        tile_expert = jnp.zeros((mt,), jnp.int32)
        n_used = jnp.full((1,), mt, jnp.int32)
    ks = [x[0].shape[1] for x in xs]
    in_specs, args, mt_a = [], [], None
    for x, k in zip(xs, ks):
        if len(x) == 1:
            in_specs.append(pl.BlockSpec((tm, k), lambda n, m, te, nu: (jnp.minimum(m, nu[0] - 1), 0)))
        else:
            specs, mt_a = _stacked_specs(x, (tm, k), tm, lambda idx: 0, 1)
            in_specs += specs
        args += list(x)
    for g in range(n_g):
        for i, k in enumerate(ks):
            in_specs.append(pl.BlockSpec((1, k, tn), lambda n, m, te, nu, i=i: (te[m], i, n)))
            args.append(ws[g])
    if epi == "resgate":
        blk0 = gt_blk * (n_cols // tn)
        specs, mt_res = _stacked_specs(res, (tm, tn), tm, lambda idx: idx[0], 1)
        mt_a = mt_res if mt_res is not None else mt_a
        in_specs += specs
        in_specs.append(pl.BlockSpec((1, 1, tn), lambda n, m, te, nu: (mod_row(m * tm), 0, blk0 + n)))
        args += list(res) + [mod]
    scratch = [pltpu.VMEM((sum(ks), tn), BF16) for _ in range(n_g)]
    out_div = 2 if epi == "packed" else 1
    return pl.pallas_call(
        functools.partial(_mm_kernel, x_parts=tuple(len(x) for x in xs),
                          res_parts=len(res) if res is not None else 0, mt_a=mt_a, n_g=n_g, epi=epi),
        out_shape=jax.ShapeDtypeStruct((n_rows, n_cols // out_div), out_dtype),
        grid_spec=pltpu.PrefetchScalarGridSpec(
            num_scalar_prefetch=2,
            grid=(nt, mt),
            in_specs=in_specs,
            out_specs=pl.BlockSpec((tm, tn // out_div), lambda n, m, te, nu: (m, n)),
            scratch_shapes=scratch,
        ),
        compiler_params=_cparams(("arbitrary", "arbitrary")),
        name="matmul_" + epi,
    )(tile_expert, n_used, *args)


def _rope(x, cos, sin_signed, half):
    lane = lax.broadcasted_iota(jnp.int32, (x.shape[0], LANES), 1)
    first = (lane & (2 * half - 1)) < half
    blocks = []
    for j in range(x.shape[1] // LANES):
        xj = x[:, j * LANES:(j + 1) * LANES]
        partner = jnp.where(first, pltpu.roll(xj, LANES - half, 1), pltpu.roll(xj, half, 1))
        blocks.append(xj * cos + partner * sin_signed)
    return blocks[0] if len(blocks) == 1 else jnp.concatenate(blocks, axis=-1)


def _group_mean_sq(x, g_ref):
    x2 = x * x
    hi = x2.astype(BF16)
    lo = (x2 - hi.astype(F32)).astype(BF16)
    g = g_ref[...]
    return (jnp.dot(hi, g, preferred_element_type=F32) + jnp.dot(lo, g, preferred_element_type=F32))


def _dup_halves(x):
    lane = lax.broadcasted_iota(jnp.int32, x.shape, 1)
    lo = lane < HEAD_DIM
    swapped = pltpu.roll(x, HEAD_DIM, 1)
    return jnp.concatenate([jnp.where(lo, x, swapped), jnp.where(lo, swapped, x)], axis=-1)


def _store_transposed(dst_ref, x):
    for j in range(x.shape[1] // LANES):
        dst_ref[0, j * LANES:(j + 1) * LANES, :] = x[:, j * LANES:(j + 1) * LANES].T.astype(BF16)


def _proj_prep_kernel(*refs, n_h, mt_a):
    refs = list(refs)
    h_refs = [refs.pop(0) for _ in range(n_h)]
    (g_ref, sh_ref, sc_ref, w_ref, c64_ref, s64_ref, cq_ref, sq_ref, ck_ref, sk_ref,
     bqn_ref, bkn_ref, cqn_ref, ckvn_ref, g512_ref, g128_ref, wq_ref, wkk_ref, wkv_ref, place_ref,
     qa_ref, qb_ref, qc_ref, qd_ref,
     ka_ref, kb_ref, kc_ref, kd_ref, vd_ref, vta_ref, vtb_ref, vtc_ref, u_ref) = refs

    def fill(h_ref):
        u_ref[...] = _rms_modulate(h_ref[...], g_ref[...], sc_ref[0], sh_ref[0]).astype(BF16)

    if n_h == 1:
        fill(h_refs[0])
    else:
        t = pl.program_id(0)
        pl.when(t < mt_a)(lambda: fill(h_refs[0]))
        pl.when(t >= mt_a)(lambda: fill(h_refs[1]))

    p_all = jnp.dot(u_ref[...], w_ref[...], preferred_element_type=F32)

    def proj(off, width):
        return p_all[:, off:off + width]

    c64, s64 = c64_ref[...], s64_ref[...]
    c128, s128 = c64, s64
    scale64 = HEAD_DIM ** -0.5
    scale_mla = (C_NOPE_DIM + C_ROPE_DIM) ** -0.5

    qa_ref[...] = (_rope(proj(OFF_QA, 512), c64, s64, 16) * (scale64 * LOG2E)).astype(BF16)
    ka_ref[0] = _rope(proj(OFF_KA, 512), c64, s64, 16).astype(BF16)
    _store_transposed(vta_ref, proj(OFF_VA, 512))

    xb = proj(OFF_QB, 512)
    yb = xb * lax.rsqrt(_group_mean_sq(xb, g512_ref) + EPS) * bqn_ref[...]
    qb_ref[...] = (_rope(yb, c64, s64, 16) * (scale64 * LOG2E)).astype(BF16)
    kvb = proj(OFF_KB, 256)
    xk = kvb[:, :LANES]
    yk = xk * lax.rsqrt(_group_mean_sq(xk, g128_ref) + EPS) * bkn_ref[...]
    kb_ref[0] = _dup_halves(_rope(yk, c128, s128, 16)).astype(BF16)
    _store_transposed(vtb_ref, kvb[:, LANES:])

    xq = proj(OFF_QC, C_Q_RANK)
    yq = xq * lax.rsqrt(jnp.mean(xq * xq, axis=-1, keepdims=True) + EPS) * cqn_ref[...]
    qf = jnp.dot(yq.astype(BF16), wq_ref[...], preferred_element_type=F32)
    qc_ref[...] = (_rope(qf, cq_ref[...], sq_ref[...], 8) * (scale_mla * LOG2E)).astype(BF16)
    xc = proj(OFF_CKV, C_KV_RANK)
    yc = (xc * lax.rsqrt(jnp.mean(xc * xc, axis=-1, keepdims=True) + EPS) * ckvn_ref[...]).astype(BF16)
    tail = proj(OFF_KR, IN_COLS - OFF_KR)
    kr = _rope(tail[:, :LANES], ck_ref[...], sk_ref[...], 8).astype(BF16)
    kc = (jnp.dot(yc, wkk_ref[...], preferred_element_type=F32)
          + jnp.dot(kr, place_ref[...], preferred_element_type=F32))
    kc_ref[0] = kc.astype(BF16)
    _store_transposed(vtc_ref, jnp.dot(yc, wkv_ref[...], preferred_element_type=F32))

    qd_ref[...] = (_rope(proj(OFF_QD, 512), c64, s64, 16) * scale64).astype(BF16)
    kd0, vd0 = OFF_KD - OFF_KR, OFF_VD - OFF_KR
    kd_ref[0] = _dup_halves(_rope(tail[:, kd0:kd0 + LANES], c128, s128, 16)).astype(BF16)
    vd_ref[0] = _dup_halves(tail[:, vd0:vd0 + LANES]).astype(BF16)


def _proj_prep(hs, g, mod, w_in_bf16, tables, consts, dims, mod_row):
    bsz, s_len, n_ctx = dims
    d = hs[0].shape[1]
    n_rows = sum(a.shape[0] for a in hs)
    sk = n_ctx + s_len
    tr = _pick(math.gcd(s_len, n_ctx), (TR, 128))
    n_lat_t, lat_pb, ctx_pb = bsz * s_len // tr, s_len // tr, n_ctx // tr

    def is_lat(t):
        return t < n_lat_t

    def tbl_idx(t):
        return jnp.where(is_lat(t), t % lat_pb, lat_pb)

    def kv_b(t):
        return jnp.where(is_lat(t), t // lat_pb, (t - n_lat_t) // ctx_pb)

    def kv_j(t):
        return jnp.where(is_lat(t), ctx_pb + t % lat_pb, (t - n_lat_t) % ctx_pb)

    def row_spec(w):
        return pl.BlockSpec((tr, w), lambda t: (t, 0))

    def tbl_spec(w):
        return pl.BlockSpec((tr, w), lambda t: (tbl_idx(t), 0))

    def const_spec(a):
        return pl.BlockSpec(a.shape, lambda t: (0,) * a.ndim)

    def kv_spec(w):
        return pl.BlockSpec((1, tr, w), lambda t: (kv_b(t), kv_j(t), 0))

    def vt_spec(w):
        return pl.BlockSpec((1, w, tr), lambda t: (kv_b(t), 0, kv_j(t)))

    q_widths = (512, 512, 1024, 512)
    kv_widths = (512, 256, 1024, 256, 256)
    vt_widths = (512, 128, 512)
    out_shape = ([jax.ShapeDtypeStruct((n_rows, w), BF16) for w in q_widths]
                 + [jax.ShapeDtypeStruct((bsz, sk, w), BF16) for w in kv_widths]
                 + [jax.ShapeDtypeStruct((bsz, w, sk), BF16) for w in vt_widths])
    out_specs = ([row_spec(w) for w in q_widths] + [kv_spec(w) for w in kv_widths]
                 + [vt_spec(w) for w in vt_widths])
    h_specs, mt_a = _stacked_specs(hs, (tr, d), tr, lambda idx: 0, 0)
    in_specs = (h_specs
                + [pl.BlockSpec((1, d), lambda t: (0, 0)),
                   pl.BlockSpec((1, 1, d), lambda t: (mod_row(t * tr), 0, 0)),
                   pl.BlockSpec((1, 1, d), lambda t: (mod_row(t * tr), 0, 1)),
                   pl.BlockSpec(w_in_bf16.shape, lambda t: (0, 0), pipeline_mode=pl.Buffered(1))]
                + [tbl_spec(t.shape[1]) for t in tables] + [const_spec(a) for a in consts])
    return pl.pallas_call(
        functools.partial(_proj_prep_kernel, n_h=len(hs), mt_a=mt_a),
        out_shape=out_shape,
        grid=(n_rows // tr,),
        in_specs=in_specs,
        out_specs=out_specs,
        scratch_shapes=[pltpu.VMEM((tr, d), BF16)],
        compiler_params=_cparams(("arbitrary",), VMEM_LIMIT_BIG),
        name="proj_prep",
    )(*hs, g.reshape(1, d), mod, mod, w_in_bf16, *tables, *consts)


def _scores(a, b):
    return lax.dot_general(a, b, (((1,), (1,)), ((), ())), preferred_element_type=F32)


def _half_masks(shape):
    lane = lax.broadcasted_iota(jnp.int32, shape, 1)
    lo = lane < HEAD_DIM
    return lo, jnp.logical_not(lo)


def _softmax_pv_t(s, v_t):
    m = jnp.max(s, axis=0, keepdims=True)
    e = jnp.exp2(s - m).astype(BF16)
    dv = v_t.shape[0]
    v_ext = jnp.concatenate([v_t, jnp.ones((BF16_SUBLANES, v_t.shape[1]), BF16)], axis=0)
    oe = jnp.dot(v_ext, e, preferred_element_type=F32)
    return oe[:dv] / oe[dv:dv + 1]


def _run_units(score_fns, finish_fns):
    s = score_fns[0]()
    for i, finish in enumerate(finish_fns):
        s_next = score_fns[i + 1]() if i + 1 < len(score_fns) else None
        finish(s)
        s = s_next


def _gqa_t_kernel(q_ref, k_ref, vt_ref, o_ref):
    masks = _half_masks((q_ref.shape[0], LANES))
    n_pairs = q_ref.shape[1] // LANES
    pairs_per_group = B_HEADS // B_KV_HEADS // 2
    outs = {}

    def score(j, z):
        g = j // pairs_per_group
        q = q_ref[:, j * LANES:(j + 1) * LANES]
        return lambda: _scores(k_ref[0, :, g * LANES:(g + 1) * LANES],
                               jnp.where(masks[z], q, jnp.zeros_like(q)))

    def finish(j, z):
        g = j // pairs_per_group

        def fin(s):
            outs[z] = _softmax_pv_t(s, vt_ref[0, g * HEAD_DIM:(g + 1) * HEAD_DIM, :])
            if z == 1:
                pair = jnp.concatenate([outs[0], outs[1]], axis=0)
                o_ref[:, j * LANES:(j + 1) * LANES] = pair.T.astype(o_ref.dtype)
        return fin

    units = [(j, z) for j in range(n_pairs) for z in range(2)]
    _run_units([score(j, z) for j, z in units], [finish(j, z) for j, z in units])


def _mla_t_kernel(q_ref, k_ref, vt_ref, o_ref):
    n_heads = q_ref.shape[1] // LANES
    outs = {}

    def score(u):
        return lambda: _scores(k_ref[0, :, u * LANES:(u + 1) * LANES], q_ref[:, u * LANES:(u + 1) * LANES])

    def finish(u):
        def fin(s):
            outs[u % 2] = _softmax_pv_t(s, vt_ref[0, u * C_V_DIM:(u + 1) * C_V_DIM, :])
            if u % 2 == 1:
                pair = jnp.concatenate([outs[0], outs[1]], axis=0)
                j = u // 2
                o_ref[:, j * LANES:(j + 1) * LANES] = pair.T.astype(o_ref.dtype)
        return fin

    _run_units([score(u) for u in range(n_heads)], [finish(u) for u in range(n_heads)])


def _diff_t_kernel(lam_ref, subln_ref, q0_ref, q1_ref, k0_ref, k1_ref, vt_ref, o_ref, *, lam_init):
    t = lam_ref[...]
    lam = (jnp.exp(jnp.sum(t[0:1] * t[1:2], axis=-1, keepdims=True))
           - jnp.exp(jnp.sum(t[2:3] * t[3:4], axis=-1, keepdims=True)) + lam_init)
    q_refs, k_refs = (q0_ref, q1_ref), (k0_ref, k1_ref)
    masks = _half_masks((q0_ref.shape[0], LANES))
    first_map = {}

    def score(z, mp):
        j = z // 2
        q = q_refs[mp][:, j * LANES:(j + 1) * LANES]
        return lambda: _scores(k_refs[mp][0, :, j * LANES:(j + 1) * LANES],
                               jnp.where(masks[z % 2], q, jnp.zeros_like(q)))

    def finish(z, mp):
        def fin(s):
            a = _softmax_pv_t(s, vt_ref[0, z * A_V_DIM:(z + 1) * A_V_DIM, :])
            if mp == 0:
                first_map[z] = a
                return
            d = (first_map[z] - lam * a).T
            y = d * lax.rsqrt(jnp.mean(d * d, axis=-1, keepdims=True) + EPS) * subln_ref[...]
            o_ref[:, z * A_V_DIM:(z + 1) * A_V_DIM] = (y * (1.0 - lam_init)).astype(o_ref.dtype)
        return fin

    units = [(z, mp) for z in range(A_HEADS) for mp in range(2)]
    _run_units([score(z, mp) for z, mp in units], [finish(z, mp) for z, mp in units])


def _attend_sink(q, k, v, sink):
    s = _scores(q, k)
    m = jnp.maximum(jnp.max(s, axis=-1, keepdims=True), sink)
    e = jnp.exp(s - m)
    l = jnp.sum(e, axis=-1, keepdims=True) + jnp.exp(sink - m)
    return jnp.dot(e.astype(BF16), v, preferred_element_type=F32) / l


def _window_kernel(sink_ref, q_ref, k_ref, v_ref, o_ref, *, s_len, n_ctx, n_q_lat, with_ctx):
    tq = q_ref.shape[0]
    band = tq + 2 * WINDOW
    i = pl.program_id(1)
    pairs_per_group = D_HEADS // D_KV_HEADS // 2

    def body(is_lat):
        masks = _half_masks((tq, LANES))
        if is_lat:
            start = pl.multiple_of(jnp.clip(i * tq - WINDOW, 0, s_len - band), LANES)
            row0 = pl.multiple_of(n_ctx + start, LANES)
            qpos = i * tq + lax.broadcasted_iota(jnp.int32, (tq, n_ctx + band), 0)
            col = lax.broadcasted_iota(jnp.int32, (tq, n_ctx + band), 1)
            visible = (col < n_ctx) | (jnp.abs(qpos - (start + col - n_ctx)) <= WINDOW)
        kv = {}
        for j in range(q_ref.shape[1] // LANES):
            g = j // pairs_per_group
            if g not in kv:
                lanes = slice(g * LANES, (g + 1) * LANES)
                k, v = k_ref[0, :n_ctx, lanes], v_ref[0, :n_ctx, lanes]
                if is_lat:
                    k = jnp.concatenate([k, k_ref[0, pl.ds(row0, band), lanes]], axis=0)
                    v = jnp.concatenate([v, v_ref[0, pl.ds(row0, band), lanes]], axis=0)
                kv[g] = (k, v)
            k, v = kv[g]
            q = q_ref[:, j * LANES:(j + 1) * LANES]
            outs = []
            for z in range(2):
                qz = jnp.where(masks[z], q, jnp.zeros_like(q))
                sink = sink_ref[2 * j + z]
                if not is_lat:
                    outs.append(_attend_sink(qz, k, v, sink))
                    continue
                s = jnp.where(visible, _scores(qz, k), -jnp.inf)
                m = jnp.maximum(jnp.max(s, axis=-1, keepdims=True), sink)
                e = jnp.exp(s - m)
                l = jnp.sum(e, axis=-1, keepdims=True) + jnp.exp(sink - m)
                outs.append(jnp.dot(e.astype(BF16), v, preferred_element_type=F32) / l)
            o_ref[:, j * LANES:(j + 1) * LANES] = jnp.where(masks[0], outs[0], outs[1]).astype(o_ref.dtype)

    if not with_ctx:
        body(True)
        return

    @pl.when(i < n_q_lat)
    def _():
        body(True)

    @pl.when(i >= n_q_lat)
    def _():
        body(False)


def _mixers(ops, dims, params, lam_init, with_ctx):
    bsz, s_len, n_ctx = dims
    qa, qb, qc, qd, ka, kb, kc, kd, vd, vta, vtb, vtc = ops
    lam_vecs, subln, sink = params
    n_lat = bsz * s_len
    sk_all = n_ctx + s_len

    def const_spec(a):
        return pl.BlockSpec(a.shape, lambda b, i: (0,) * a.ndim)

    def dense_set(tq, nk, row0_tiles, n_q, out_rows):
        def q_row(b, i):
            return row0_tiles + b * n_q + i

        def qspec(w, col=0):
            return pl.BlockSpec((tq, w), lambda b, i: (q_row(b, i), col))

        def kspec(w, col=0):
            return pl.BlockSpec((1, nk, w), lambda b, i: (b, 0, col))

        def vtspec(w):
            return pl.BlockSpec((1, w, nk), lambda b, i: (b, 0, 0))

        def call(kernel, in_specs, args, name):
            return pl.pallas_call(
                kernel,
                out_shape=jax.ShapeDtypeStruct((out_rows, 512), BF16),
                grid=(bsz, n_q),
                in_specs=in_specs,
                out_specs=pl.BlockSpec((tq, 512), lambda b, i: (b * n_q + i, 0)),
                compiler_params=_cparams(("arbitrary", "arbitrary"), VMEM_LIMIT_BIG),
                name=name,
            )(*args)

        oa = call(functools.partial(_diff_t_kernel, lam_init=lam_init),
                  [const_spec(lam_vecs), const_spec(subln), qspec(256, 0), qspec(256, 1),
                   kspec(256, 0), kspec(256, 1), vtspec(512)],
                  [lam_vecs, subln, qa, qa, ka, ka, vta], "attn_diff")
        ob = call(_gqa_t_kernel, [qspec(512), kspec(256), vtspec(128)], [qb, kb, vtb], "attn_qknorm")
        oc = call(_mla_t_kernel, [qspec(1024), kspec(1024), vtspec(512)], [qc, kc, vtc], "attn_mla")
        return [oa, ob, oc]

    tq = _pick(s_len, (TQ, 256, 128))
    dense = [[o] for o in dense_set(tq, sk_all, 0, s_len // tq, n_lat)]
    if with_ctx:
        tq_c = _pick(n_ctx, (TQ, 256, 128))
        ctx = dense_set(tq_c, n_ctx, n_lat // tq_c, n_ctx // tq_c, bsz * n_ctx)
        dense = [a + [b] for a, b in zip(dense, ctx)]

    tw = _pick(math.gcd(s_len, n_ctx), (TQ_WIN, 128))
    n_q_lat, n_q_ctx = s_len // tw, n_ctx // tw
    n_q = n_q_lat + n_q_ctx if with_ctx else n_q_lat
    out_rows = n_lat + bsz * n_ctx if with_ctx else n_lat

    def w_row(b, i):
        return jnp.where(i < n_q_lat, b * n_q_lat + i, n_lat // tw + b * n_q_ctx + (i - n_q_lat))

    od = pl.pallas_call(
        functools.partial(_window_kernel, s_len=s_len, n_ctx=n_ctx, n_q_lat=n_q_lat, with_ctx=with_ctx),
        out_shape=jax.ShapeDtypeStruct((out_rows, 512), BF16),
        grid=(bsz, n_q),
        in_specs=[pl.BlockSpec(memory_space=pltpu.SMEM),
                  pl.BlockSpec((tw, 512), lambda b, i: (w_row(b, i), 0)),
                  pl.BlockSpec((1, sk_all, 256), lambda b, i: (b, 0, 0)),
                  pl.BlockSpec((1, sk_all, 256), lambda b, i: (b, 0, 0))],
        out_specs=pl.BlockSpec((tw, 512), lambda b, i: (w_row(b, i), 0)),
        compiler_params=_cparams(("arbitrary", "arbitrary")),
        name="attn_window",
    )(sink, qd, kd, vd)
    return dense + [[od]]


def _route_kernel(h_ref, g_ref, sh_ref, sc_ref, w_ref, b_ref, u_ref, sel_ref, idx_ref, gw_ref):
    u = _rms_modulate(h_ref[...], g_ref[...], sc_ref[0], sh_ref[0])
    u_ref[...] = _pack_bf16_pairs(u)
    w = w_ref[...]
    u_hi = u.astype(BF16)
    u_lo = (u - u_hi.astype(F32)).astype(BF16)
    w_hi = w.astype(BF16)
    w_lo = (w - w_hi.astype(F32)).astype(BF16)
    logits = (_scores(w_hi, u_hi) + _scores(w_hi, u_lo) + _scores(w_lo, u_hi) + _scores(w_lo, u_lo)
              + b_ref[...])
    ids = lax.broadcasted_iota(jnp.int32, logits.shape, 0).astype(F32)
    m1 = jnp.max(logits, axis=0, keepdims=True)
    i1 = jnp.min(jnp.where(logits == m1, ids, float(N_EXPERTS)), axis=0, keepdims=True)
    first = ids == i1
    rest = jnp.where(first, -jnp.inf, logits)
    m2 = jnp.max(rest, axis=0, keepdims=True)
    i2 = jnp.min(jnp.where(rest == m2, ids, float(N_EXPERTS)), axis=0, keepdims=True)
    second = ids == i2
    e = jnp.exp(m2 - m1)
    w1 = 1.0 / (1.0 + e)
    w2 = e / (1.0 + e)
    sel_ref[...] = jnp.where(first | second, 1, 0).astype(jnp.int32)
    idx_ref[...] = jnp.where(ids == 0.0, i1, jnp.where(ids == 1.0, i2, 0.0)).astype(jnp.int32)
    gw_ref[...] = jnp.where(ids == 0.0, w1, jnp.where(ids == 1.0, w2, 0.0))


def _route(h, g, mod, sh_blk, sc_blk, w_router_t, b_router, n_tok, s_len, mod_row):
    d = h.shape[1]
    tm = _row_tile(n_tok, s_len, (TM, 256, 128))
    outs = ([jax.ShapeDtypeStruct((n_tok, d // 2), jnp.uint32)]
            + [jax.ShapeDtypeStruct((N_EXPERTS, n_tok), dt) for dt in (jnp.int32, jnp.int32, F32)])
    return pl.pallas_call(
        _route_kernel,
        out_shape=outs,
        grid=(n_tok // tm,),
        in_specs=[pl.BlockSpec((tm, d), lambda m: (m, 0)),
                  pl.BlockSpec((1, d), lambda m: (0, 0)),
                  pl.BlockSpec((1, 1, d), lambda m: (mod_row(m * tm), 0, sh_blk)),
                  pl.BlockSpec((1, 1, d), lambda m: (mod_row(m * tm), 0, sc_blk)),
                  pl.BlockSpec((N_EXPERTS, d), lambda m: (0, 0)),
                  pl.BlockSpec((N_EXPERTS, 1), lambda m: (0, 0))],
        out_specs=[pl.BlockSpec((tm, d // 2), lambda m: (m, 0))]
        + [pl.BlockSpec((N_EXPERTS, tm), lambda m: (0, m))] * 3,
        compiler_params=_cparams(("arbitrary",)),
        name="moe_route",
    )(h, g.reshape(1, d), mod, mod, w_router_t, b_router.reshape(N_EXPERTS, 1))


def _row_copy(src, dst, sem, src_row, dst_row):
    return pltpu.make_async_copy(src.at[pl.ds(src_row, 1)], dst.at[pl.ds(dst_row, 1)], sem)


def _gather_kernel(tok_ref, nxt_ref, src_ref, o_ref, buf, sem):
    i = pl.program_id(0)
    n = pl.num_programs(0)
    tg = buf.shape[1]
    slot = i % 2

    def issue(t_ref, s):
        def body(r, carry):
            _row_copy(src_ref, buf.at[s], sem.at[s], t_ref[0, 0, r], r).start()
            return carry
        lax.fori_loop(0, tg, body, 0, unroll=DMA_UNROLL)

    @pl.when(i == 0)
    def _():
        issue(tok_ref, 0)

    @pl.when(i + 1 < n)
    def _():
        issue(nxt_ref, 1 - slot)

    def wait(r, carry):
        _row_copy(src_ref, buf.at[slot], sem.at[slot], 0, r).wait()
        return carry

    lax.fori_loop(0, tg, wait, 0, unroll=DMA_UNROLL)
    lo, hi = _unpack_bf16_pairs(buf[slot])
    half = lo.shape[1]
    o_ref[:, :half] = lo.astype(o_ref.dtype)
    o_ref[:, half:] = hi.astype(o_ref.dtype)


def _gather_rows(src, tok, tg):
    n_rows = tok.shape[0]
    dw = src.shape[1]
    d = 2 * dw
    n_t = n_rows // tg
    tok = tok.reshape(n_t, 1, tg)
    return pl.pallas_call(
        _gather_kernel,
        out_shape=jax.ShapeDtypeStruct((n_rows, d), BF16),
        grid=(n_t,),
        in_specs=[pl.BlockSpec((1, 1, tg), lambda i: (i, 0, 0), memory_space=pltpu.SMEM),
                  pl.BlockSpec((1, 1, tg), lambda i: (jnp.minimum(i + 1, n_t - 1), 0, 0),
                               memory_space=pltpu.SMEM),
                  pl.BlockSpec(memory_space=pl.ANY)],
        out_specs=pl.BlockSpec((tg, d), lambda i: (i, 0)),
        scratch_shapes=[pltpu.VMEM((2, tg, dw), jnp.uint32), pltpu.SemaphoreType.DMA((2,))],
        compiler_params=_cparams(("arbitrary",)),
        name="moe_gather",
    )(tok, tok, src)


def _combine_kernel(pos_ref, nxt_ref, y_ref, h_ref, gw_ref, gt_ref, gf_ref, o_ref, buf, sem, *, pack_tile):
    i = pl.program_id(0)
    n = pl.num_programs(0)
    tc = buf.shape[2]
    slot = i % 2

    def issue(p_ref, s):
        def body(r, carry):
            _row_copy(y_ref, buf.at[s, 0], sem.at[s], p_ref[0, 0, r], r).start()
            _row_copy(y_ref, buf.at[s, 1], sem.at[s], p_ref[0, 1, r], r).start()
            return carry
        lax.fori_loop(0, tc, body, 0, unroll=DMA_UNROLL)

    @pl.when(i == 0)
    def _():
        issue(pos_ref, 0)

    @pl.when(i + 1 < n)
    def _():
        issue(nxt_ref, 1 - slot)

    def wait(r, carry):
        _row_copy(y_ref, buf.at[slot, 0], sem.at[slot], 0, r).wait()
        _row_copy(y_ref, buf.at[slot, 1], sem.at[slot], 0, r).wait()
        return carry

    lax.fori_loop(0, tc, wait, 0, unroll=DMA_UNROLL)
    def expert_rows(k):
        lo, hi = _unpack_bf16_pairs(buf[slot, k])
        hw = pack_tile // 2
        cols = []
        for j in range(lo.shape[1] // hw):
            cols += [lo[:, j * hw:(j + 1) * hw], hi[:, j * hw:(j + 1) * hw]]
        return jnp.concatenate(cols, axis=-1)

    gw = gw_ref[...]
    moe = gw[:, 0:1] * expert_rows(0) + gw[:, 1:2] * expert_rows(1)
    x = h_ref[...] + gt_ref[0] * moe
    ms = jnp.mean(x * x, axis=-1, keepdims=True)
    o_ref[...] = x * lax.rsqrt(ms + EPS) * gf_ref[...]


def _combine(y, h, pos, gw_t, mod, gt_blk, mod_row, g_final, n_tok, pack_tile):
    d = h.shape[1]
    n_t, _, tc = pos.shape
    return pl.pallas_call(
        functools.partial(_combine_kernel, pack_tile=pack_tile),
        out_shape=jax.ShapeDtypeStruct((n_tok, d), F32),
        grid=(n_t,),
        in_specs=[pl.BlockSpec((1, 2, tc), lambda i: (i, 0, 0), memory_space=pltpu.SMEM),
                  pl.BlockSpec((1, 2, tc), lambda i: (jnp.minimum(i + 1, n_t - 1), 0, 0),
                               memory_space=pltpu.SMEM),
                  pl.BlockSpec(memory_space=pl.ANY),
                  pl.BlockSpec((tc, d), lambda i: (i, 0)),
                  pl.BlockSpec((tc, N_EXPERTS), lambda i: (i, 0)),
                  pl.BlockSpec((1, 1, d), lambda i: (mod_row(i * tc), 0, gt_blk)),
                  pl.BlockSpec((1, d), lambda i: (0, 0))],
        out_specs=pl.BlockSpec((tc, d), lambda i: (i, 0)),
        scratch_shapes=[pltpu.VMEM((2, 2, tc, d // 2), jnp.uint32), pltpu.SemaphoreType.DMA((2,))],
        compiler_params=_cparams(("arbitrary",)),
        name="moe_combine",
    )(pos, pos, y, h, gw_t, mod, g_final.reshape(1, d))


def _dispatch_plan(sel, idx, tm, n_slots, tc):
    n_tok = sel.shape[1]
    n_tiles = n_slots // tm
    counts = jnp.sum(sel, axis=1)
    padded = ((counts + tm - 1) // tm) * tm
    ends = jnp.cumsum(padded)
    offs = ends - padded
    pos = offs[:, None] + jnp.cumsum(sel, axis=1) - sel
    pos0 = jnp.take_along_axis(pos, idx[0:1], axis=0)[0]
    pos1 = jnp.take_along_axis(pos, idx[1:2], axis=0)[0]
    pos_tiles = jnp.stack([pos0.reshape(n_tok // tc, tc), pos1.reshape(n_tok // tc, tc)], axis=1)
    tile_start = jnp.arange(n_tiles, dtype=jnp.int32) * tm
    tile_expert = jnp.minimum(jnp.sum(ends[None, :] <= tile_start[:, None], axis=1), N_EXPERTS - 1)
    n_used = ends[-1] // tm
    tok = jnp.arange(n_tok, dtype=jnp.int32)
    tok_of_row = jnp.zeros((n_slots,), jnp.int32).at[jnp.concatenate([pos0, pos1])].set(
        jnp.concatenate([tok, tok]), unique_indices=True)
    return (pos_tiles.astype(jnp.int32), tok_of_row, tile_expert.astype(jnp.int32),
            n_used.reshape(1).astype(jnp.int32))


def _rope_tables(s_len, pad_rows):
    f32 = np.float32
    t = np.arange(s_len)
    rows, cols = (t // GRID_W).astype(f32), (t % GRID_W).astype(f32)

    def axis_tables(rot_dim):
        axis_dim = rot_dim // 2
        inv = (f32(ROPE_THETA) ** (-np.arange(0, axis_dim, 2, dtype=f32) / f32(axis_dim))).astype(f32)
        ar, ac = rows[:, None] * inv[None, :], cols[:, None] * inv[None, :]
        cos = np.concatenate([np.cos(ar), np.cos(ar), np.cos(ac), np.cos(ac)], axis=1)
        sin = np.concatenate([-np.sin(ar), np.sin(ar), -np.sin(ac), np.sin(ac)], axis=1)
        return cos.astype(f32), sin.astype(f32)

    def with_identity(cos, sin):
        w = cos.shape[1]
        return (np.concatenate([cos, np.ones((pad_rows, w), f32)], axis=0),
                np.concatenate([sin, np.zeros((pad_rows, w), f32)], axis=0))

    c64, s64 = axis_tables(HEAD_DIM)
    c64, s64 = np.tile(c64, (1, 2)), np.tile(s64, (1, 2))
    c32, s32 = axis_tables(C_ROPE_DIM)
    ones, zeros = np.ones((s_len, 1), f32), np.zeros((s_len, 1), f32)
    cq = np.concatenate([np.tile(ones, (1, 64)), c32, np.tile(ones, (1, 32))], axis=1)
    sq = np.concatenate([np.tile(zeros, (1, 64)), s32, np.tile(zeros, (1, 32))], axis=1)
    ck = np.concatenate([c32, np.tile(ones, (1, 96))], axis=1)
    sk = np.concatenate([s32, np.tile(zeros, (1, 96))], axis=1)
    out = []
    for c, s in ((c64, s64), (cq, sq), (ck, sk)):
        out += [jnp.asarray(a) for a in with_identity(c, s)]
    return out


def _mla_weights(w_q_up, w_kv_up):
    qd = C_NOPE_DIM + C_ROPE_DIM
    wq = jnp.pad(w_q_up.reshape(C_Q_RANK, C_HEADS, qd), ((0, 0), (0, 0), (0, LANES - qd)))
    wkv = w_kv_up.reshape(C_KV_RANK, C_HEADS, C_NOPE_DIM + C_V_DIM)
    wkk = jnp.pad(wkv[:, :, :C_NOPE_DIM], ((0, 0), (0, 0), (0, LANES - C_NOPE_DIM)))
    wkv_v = wkv[:, :, C_NOPE_DIM:]
    return (wq.reshape(C_Q_RANK, C_HEADS * LANES).astype(BF16),
            wkk.reshape(C_KV_RANK, C_HEADS * LANES).astype(BF16),
            wkv_v.reshape(C_KV_RANK, C_HEADS * C_V_DIM).astype(BF16))


def _static_mats():
    g = (np.arange(512)[:, None] // HEAD_DIM == np.arange(512)[None, :] // HEAD_DIM) / HEAD_DIM
    place = np.zeros((LANES, C_HEADS * LANES), np.float32)
    for h in range(C_HEADS):
        place[np.arange(C_ROPE_DIM), h * LANES + C_NOPE_DIM + np.arange(C_ROPE_DIM)] = 1.0
    return (jnp.asarray(g, BF16), jnp.asarray(g[:128, :128], BF16), jnp.asarray(place, BF16))


def kernel(x, c, ctx, c_ctx, w_mod, b_mod, g_mix, g_ffn, g_final, w_in, w_out, a_lam_q1, a_lam_k1, a_lam_q2, a_lam_k2, a_subln, b_q_norm, b_k_norm, c_q_norm, c_kv_norm, c_w_q_up, c_w_kv_up, d_sink, ffn_w_gate, ffn_w_up, ffn_w_down, moe_w_router, moe_b_router, moe_w_gate, moe_w_up, moe_w_down):
    bsz, s_len, d = x.shape
    n_ctx = ctx.shape[1]
    depth = w_mod.shape[0]
    n_lat = bsz * s_len
    n_all = n_lat + bsz * n_ctx
    dims = (bsz, s_len, n_ctx)

    def mod_row(row0):
        return jnp.where(row0 < n_lat, row0 // s_len, bsz)

    mod_rows = -(-(bsz + 1) // 8) * 8
    cc = jnp.zeros((mod_rows, d), F32).at[:bsz].set(c).at[bsz].set(c_ctx)
    mods = _modulation(cc, w_mod, b_mod)

    tr = _pick(math.gcd(s_len, n_ctx), (TR, 128))
    tables = _rope_tables(s_len, tr)
    g512, g128, place = _static_mats()

    h = [x.reshape(n_lat, d), ctx.reshape(bsz * n_ctx, d)]
    for l in range(depth):
        last = l == depth - 1
        lam_init = 0.8 - 0.6 * math.exp(-0.3 * l)
        n_rows = n_lat if last else n_all
        mod = mods[l].reshape(mod_rows, 1, 6 * d)
        nm = dict(s_len=s_len, mod_row=mod_row)

        wq, wkk, wkv_v = _mla_weights(c_w_q_up[l], c_w_kv_up[l])
        consts = [jnp.tile(b_q_norm[l], 8)[None], jnp.tile(b_k_norm[l], 2)[None],
                  c_q_norm[l][None], c_kv_norm[l][None], g512, g128, wq, wkk, wkv_v, place]
        ops = _proj_prep(h, g_mix[l], mod, w_in[l].astype(BF16), tables, consts, dims, mod_row)
        lam_vecs = jnp.stack([a_lam_q1[l], a_lam_k1[l], a_lam_q2[l], a_lam_k2[l]])
        mix = _mixers(ops, dims, (lam_vecs, a_subln[l][None], d_sink[l]), lam_init, not last)
        h = [_matmul(mix, [w_out[l][None]], n_rows=n_rows, out_dtype=F32, epi="resgate",
                     res=h, mod=mod, gt_blk=2, mod_row=mod_row, tn=1024)]

        i = l // 2
        if l % 2 == 0:
            mid = _nm_matmul(h, g_ffn[l], mod, 3, 4, [ffn_w_gate[i], ffn_w_up[i]], n_rows=n_rows,
                             out_dtype=BF16, **nm)
            h = [_matmul([[mid]], [ffn_w_down[i][None]], n_rows=n_rows, out_dtype=F32, epi="resgate",
                         res=h, mod=mod, gt_blk=5, mod_row=mod_row)]
        else:
            if not last:
                raise NotImplementedError("expert layers are only supported as the last layer")
            h = h[0]
            u2, sel, idx, gw = _route(h, g_ffn[l], mod, 3, 4, moe_w_router[i].T, moe_b_router[i],
                                      n_rows, s_len, mod_row)
            tm = _pick(n_rows, (TM, 256, 128))
            tc = _pick(s_len, (TC, 128))
            n_slots = 2 * n_rows + N_EXPERTS * tm
            pos, tok_of_row, tile_expert, n_used = _dispatch_plan(sel, idx, tm, n_slots, tc)
            xs = _gather_rows(u2, tok_of_row, tm)
            mid = _matmul([[xs]], [moe_w_gate[i], moe_w_up[i]], n_rows=n_slots, out_dtype=BF16,
                          epi="swiglu", tile_expert=tile_expert, n_used=n_used)
            y = _matmul([[mid]], [moe_w_down[i]], n_rows=n_slots, out_dtype=jnp.uint32, epi="packed",
                        tile_expert=tile_expert, n_used=n_used)
            out = _combine(y, h, pos, gw.T, mod, 5, mod_row, g_final, n_rows, _pick(d, (TN, 256, 128)))
            return out.reshape(bsz, s_len, d)

    tm = _pick(n_lat, (TM, 256, 128))
    out = pl.pallas_call(
        _final_norm_kernel,
        out_shape=jax.ShapeDtypeStruct((n_lat, d), F32),
        grid=(n_lat // tm,),
        in_specs=[pl.BlockSpec((tm, d), lambda m: (m, 0)), pl.BlockSpec((1, d), lambda m: (0, 0))],
        out_specs=pl.BlockSpec((tm, d), lambda m: (m, 0)),
        compiler_params=_cparams(("arbitrary",)),
        name="final_norm",
    )(h[0], g_final.reshape(1, d))
    return out.reshape(bsz, s_len, d)
```

```python
import functools
import math

import numpy as np
import jax
import jax.numpy as jnp
from jax import lax
from jax.experimental import pallas as pl
from jax.experimental.pallas import tpu as pltpu

F32 = jnp.float32
BF16 = jnp.bfloat16

GRID_W = 64
HEAD_DIM = 64
ROPE_THETA = 10000.0
EPS = 1e-6
A_HEADS, A_QK_DIM, A_V_DIM = 4, 64, 128
B_HEADS, B_KV_HEADS = 8, 2
C_HEADS, C_Q_RANK, C_KV_RANK, C_NOPE_DIM, C_ROPE_DIM, C_V_DIM = 8, 768, 256, 64, 32, 64
D_HEADS, D_KV_HEADS = 8, 2
WINDOW = 128
N_EXPERTS = 8
LOG2E = 1.4426950408889634

LANES = 128
BF16_SUBLANES = 16
VMEM_LIMIT = 48 * 2**20
VMEM_LIMIT_BIG = 56 * 2**20

TM = 512
TM_X = 1024
TN = 512
TR = 256
TQ = 512
TQ_WIN = 256
TC = 256
DMA_UNROLL = 8

OFF_QA, OFF_QB, OFF_QC, OFF_QD = 0, 512, 1024, 1792
OFF_KA, OFF_VA, OFF_KB, OFF_VB = 2304, 2816, 3328, 3456
OFF_CKV, OFF_KR, OFF_KD, OFF_VD = 3584, 3840, 3872, 4000
IN_COLS = 4128


def _cparams(sem, vmem=VMEM_LIMIT):
    return pltpu.CompilerParams(dimension_semantics=sem, vmem_limit_bytes=vmem)


def _pick(n, prefs):
    for p in prefs:
        if n % p == 0:
            return p
    return n


def _row_tile(n_rows, s_len, prefs):
    for p in prefs:
        if n_rows % p == 0 and s_len % p == 0:
            return p
    raise ValueError("no row tile fits")


def _pack_bf16_pairs(x):
    w = x.shape[1] // 2
    xb = x.astype(BF16).astype(F32)
    return (pltpu.bitcast(xb[:, :w], jnp.uint32) >> 16) | pltpu.bitcast(xb[:, w:], jnp.uint32)


def _unpack_bf16_pairs(words):
    return (pltpu.bitcast(words << 16, F32), pltpu.bitcast(words & jnp.uint32(0xFFFF0000), F32))


def _rms_modulate(x, g, sc, sh):
    ms = jnp.mean(x * x, axis=-1, keepdims=True)
    return (x * lax.rsqrt(ms + EPS) * g) * (1.0 + sc) + sh


def _mod_kernel(c_ref, w_ref, b_ref, o_ref):
    a = c_ref[...]
    a = (a * jax.nn.sigmoid(a)).astype(BF16)
    o_ref[0] = jnp.dot(a, w_ref[0].astype(BF16), preferred_element_type=F32) + b_ref[0]


def _modulation(cc, w_mod, b_mod):
    depth, d, d6 = w_mod.shape
    rows = cc.shape[0]
    tn = _pick(d6, (1024, 512, 256, 128))
    return pl.pallas_call(
        _mod_kernel,
        out_shape=jax.ShapeDtypeStruct((depth, rows, d6), F32),
        grid=(depth, d6 // tn),
        in_specs=[
            pl.BlockSpec((rows, d), lambda l, n: (0, 0)),
            pl.BlockSpec((1, d, tn), lambda l, n: (l, 0, n)),
            pl.BlockSpec((1, 1, tn), lambda l, n: (l, 0, n)),
        ],
        out_specs=pl.BlockSpec((1, rows, tn), lambda l, n: (l, 0, n)),
        compiler_params=_cparams(("arbitrary", "arbitrary")),
        name="modulation",
    )(cc, w_mod, b_mod.reshape(depth, 1, d6))


def _stacked_specs(parts, block, tile_rows, col_of, n_lead):
    def grid_m(idx):
        return idx[n_lead]

    if len(parts) == 1:
        return [pl.BlockSpec(block, lambda *idx: (grid_m(idx), col_of(idx)))], None
    mt_a = parts[0].shape[0] // tile_rows
    return ([pl.BlockSpec(block, lambda *idx: (jnp.minimum(grid_m(idx), mt_a - 1), col_of(idx))),
             pl.BlockSpec(block, lambda *idx: (jnp.maximum(grid_m(idx) - mt_a, 0), col_of(idx)))], mt_a)


def _nm_mm_kernel(*refs, n_h, mt_a, n_g):
    h_refs = refs[:n_h]
    g_ref, sh_ref, sc_ref = refs[n_h:n_h + 3]
    w_refs, o_ref, u_ref = refs[n_h + 3:n_h + 3 + n_g], refs[n_h + 3 + n_g], refs[n_h + 4 + n_g]
    m = pl.program_id(0)

    def fill(h_ref):
        u_ref[...] = _rms_modulate(h_ref[...], g_ref[...], sc_ref[0], sh_ref[0]).astype(BF16)

    first = pl.program_id(1) == 0
    if n_h == 1:
        pl.when(first)(lambda: fill(h_refs[0]))
    else:
        pl.when(first & (m < mt_a))(lambda: fill(h_refs[0]))
        pl.when(first & (m >= mt_a))(lambda: fill(h_refs[1]))

    u = u_ref[...]
    accs = [jnp.dot(u, w[...].astype(BF16), preferred_element_type=F32) for w in w_refs]
    out = accs[0] if n_g == 1 else (accs[0] * jax.nn.sigmoid(accs[0])) * accs[1]
    o_ref[...] = out.astype(o_ref.dtype)


def _nm_matmul(hs, g, mod, sh_blk, sc_blk, ws, *, n_rows, s_len, mod_row, out_dtype):
    d = hs[0].shape[1]
    n_cols = ws[0].shape[1]
    tm = _row_tile(n_rows, s_len, (TM_X, 512, 256, 128))
    tn = TN
    h_specs, mt_a = _stacked_specs(hs, (tm, d), tm, lambda idx: 0, 0)
    in_specs = h_specs + [
        pl.BlockSpec((1, d), lambda m, n: (0, 0)),
        pl.BlockSpec((1, 1, d), lambda m, n: (mod_row(m * tm), 0, sh_blk)),
        pl.BlockSpec((1, 1, d), lambda m, n: (mod_row(m * tm), 0, sc_blk)),
    ] + [pl.BlockSpec((d, tn), lambda m, n: (0, n)) for _ in ws]
    return pl.pallas_call(
        functools.partial(_nm_mm_kernel, n_h=len(hs), mt_a=mt_a, n_g=len(ws)),
        out_shape=jax.ShapeDtypeStruct((n_rows, n_cols), out_dtype),
        grid=(n_rows // tm, pl.cdiv(n_cols, tn)),
        in_specs=in_specs,
        out_specs=pl.BlockSpec((tm, tn), lambda m, n: (m, n)),
        scratch_shapes=[pltpu.VMEM((tm, d), BF16)],
        compiler_params=_cparams(("arbitrary", "arbitrary"), VMEM_LIMIT_BIG),
        name="normmod_matmul",
    )(*hs, g.reshape(1, d), mod, mod, *ws)


def _final_norm_kernel(x_ref, g_ref, o_ref):
    x = x_ref[...]
    ms = jnp.mean(x * x, axis=-1, keepdims=True)
    o_ref[...] = x * lax.rsqrt(ms + EPS) * g_ref[...]


def _mm_kernel(te_ref, nu_ref, *refs, x_parts, res_parts, mt_a, n_g, epi):
    n_x = len(x_parts)
    refs = list(refs)

    def take(n):
        out = refs[:n]
        del refs[:n]
        return out

    x_refs = [take(p) for p in x_parts]
    w_refs = take(n_x * n_g)
    if epi == "resgate":
        res_refs = take(res_parts)
        gt_ref, = take(1)
    o_ref, = take(1)
    wc_refs = refs
    m = pl.program_id(1)

    def stacked(parts):
        if len(parts) == 1:
            return parts[0][...]
        return jnp.where(m < mt_a, parts[0][...], parts[1][...])

    panel_changed = (m == 0) | (te_ref[m] != te_ref[jnp.maximum(m - 1, 0)])

    fused = n_x == 1
    tn = w_refs[0].shape[2]

    @pl.when(panel_changed)
    def _():
        if fused:
            for g in range(n_g):
                wc_refs[0][:, g * tn:(g + 1) * tn] = w_refs[g][0].astype(BF16)
        else:
            for w, wc in zip(w_refs, wc_refs):
                wc[...] = w[0].astype(BF16)

    @pl.when(m < nu_ref[0])
    def _():
        accs = []
        if fused:
            wide = jnp.dot(stacked(x_refs[0]).astype(BF16), wc_refs[0][...], preferred_element_type=F32)
            accs = [wide[:, g * tn:(g + 1) * tn] for g in range(n_g)]
        for g in range(0 if fused else n_g):
            acc = None
            for i in range(n_x):
                part = jnp.dot(stacked(x_refs[i]).astype(BF16), wc_refs[g * n_x + i][...],
                               preferred_element_type=F32)
                acc = part if acc is None else acc + part
            accs.append(acc)
        if epi == "plain":
            out = accs[0]
        elif epi == "packed":
            out = _pack_bf16_pairs(accs[0])
        elif epi == "swiglu":
            a = accs[0]
            out = (a * jax.nn.sigmoid(a)) * accs[1]
        else:
            out = stacked(res_refs) + gt_ref[0] * accs[0]
        o_ref[...] = out.astype(o_ref.dtype)

    @pl.when(m >= nu_ref[0])
    def _():
        o_ref[...] = jnp.zeros_like(o_ref)


def _matmul(xs, ws, *, n_rows, out_dtype, epi="plain", tile_expert=None, n_used=None,
            res=None, mod=None, gt_blk=None, mod_row=None, tm=TM, tn=TN):
    n_x, n_g = len(xs), len(ws)
    n_cols = ws[0].shape[2]
    tn = _pick(n_cols, (tn, 512, 256, 128))
    tm = _pick(n_rows, (tm, 256, 128))
    mt, nt = n_rows // tm, n_cols // tn
    if tile_expert is None:
        tile_expert = jnp.zeros((mt,), jnp.int32)
        n_used = jnp.full((1,), mt, jnp.int32)
    ks = [x[0].shape[1] for x in xs]
    in_specs, args, mt_a = [], [], None
    for x, k in zip(xs, ks):
        if len(x) == 1:
            in_specs.append(pl.BlockSpec((tm, k), lambda n, m, te, nu: (jnp.minimum(m, nu[0] - 1), 0)))
        else:
            specs, mt_a = _stacked_specs(x, (tm, k), tm, lambda idx: 0, 1)
            in_specs += specs
        args += list(x)
    for g in range(n_g):
        for i, k in enumerate(ks):
            in_specs.append(pl.BlockSpec((1, k, tn), lambda n, m, te, nu, i=i: (te[m], i, n)))
            args.append(ws[g])
    if epi == "resgate":
        blk0 = gt_blk * (n_cols // tn)
        specs, mt_res = _stacked_specs(res, (tm, tn), tm, lambda idx: idx[0], 1)
        mt_a = mt_res if mt_res is not None else mt_a
        in_specs += specs
        in_specs.append(pl.BlockSpec((1, 1, tn), lambda n, m, te, nu: (mod_row(m * tm), 0, blk0 + n)))
        args += list(res) + [mod]
    if n_x == 1:
        scratch = [pltpu.VMEM((ks[0], n_g * tn), BF16)]
    else:
        scratch = [pltpu.VMEM((k, tn), BF16) for _ in range(n_g) for k in ks]
    out_div = 2 if epi == "packed" else 1
    return pl.pallas_call(
        functools.partial(_mm_kernel, x_parts=tuple(len(x) for x in xs),
                          res_parts=len(res) if res is not None else 0, mt_a=mt_a, n_g=n_g, epi=epi),
        out_shape=jax.ShapeDtypeStruct((n_rows, n_cols // out_div), out_dtype),
        grid_spec=pltpu.PrefetchScalarGridSpec(
            num_scalar_prefetch=2,
            grid=(nt, mt),
            in_specs=in_specs,
            out_specs=pl.BlockSpec((tm, tn // out_div), lambda n, m, te, nu: (m, n)),
            scratch_shapes=scratch,
        ),
        compiler_params=_cparams(("arbitrary", "arbitrary")),
        name="matmul_" + epi,
    )(tile_expert, n_used, *args)


def _rope(x, cos, sin_signed, half):
    lane = lax.broadcasted_iota(jnp.int32, (x.shape[0], LANES), 1)
    first = (lane & (2 * half - 1)) < half
    blocks = []
    for j in range(x.shape[1] // LANES):
        xj = x[:, j * LANES:(j + 1) * LANES]
        partner = jnp.where(first, pltpu.roll(xj, LANES - half, 1), pltpu.roll(xj, half, 1))
        blocks.append(xj * cos + partner * sin_signed)
    return blocks[0] if len(blocks) == 1 else jnp.concatenate(blocks, axis=-1)


def _group_mean_sq(x, g_ref):
    x2 = x * x
    hi = x2.astype(BF16)
    lo = (x2 - hi.astype(F32)).astype(BF16)
    g = g_ref[...]
    return (jnp.dot(hi, g, preferred_element_type=F32) + jnp.dot(lo, g, preferred_element_type=F32))


def _dup_halves(x):
    lane = lax.broadcasted_iota(jnp.int32, x.shape, 1)
    lo = lane < HEAD_DIM
    swapped = pltpu.roll(x, HEAD_DIM, 1)
    return jnp.concatenate([jnp.where(lo, x, swapped), jnp.where(lo, swapped, x)], axis=-1)


def _store_transposed(dst_ref, x):
    for j in range(x.shape[1] // LANES):
        dst_ref[0, j * LANES:(j + 1) * LANES, :] = x[:, j * LANES:(j + 1) * LANES].T.astype(BF16)


def _proj_prep_kernel(*refs, n_h, mt_a):
    refs = list(refs)
    h_refs = [refs.pop(0) for _ in range(n_h)]
    (g_ref, sh_ref, sc_ref, w_ref, c64_ref, s64_ref, cq_ref, sq_ref, ck_ref, sk_ref,
     bqn_ref, bkn_ref, cqn_ref, ckvn_ref, g512_ref, g128_ref, wq_ref, wkk_ref, wkv_ref, place_ref,
     qa_ref, qb_ref, qc_ref, qd_ref,
     ka_ref, kb_ref, kc_ref, kd_ref, vd_ref, vta_ref, vtb_ref, vtc_ref, u_ref) = refs

    def fill(h_ref):
        u_ref[...] = _rms_modulate(h_ref[...], g_ref[...], sc_ref[0], sh_ref[0]).astype(BF16)

    if n_h == 1:
        fill(h_refs[0])
    else:
        t = pl.program_id(0)
        pl.when(t < mt_a)(lambda: fill(h_refs[0]))
        pl.when(t >= mt_a)(lambda: fill(h_refs[1]))

    p_all = jnp.dot(u_ref[...], w_ref[...], preferred_element_type=F32)

    def proj(off, width):
        return p_all[:, off:off + width]

    c64, s64 = c64_ref[...], s64_ref[...]
    c128, s128 = c64, s64
    scale64 = HEAD_DIM ** -0.5
    scale_mla = (C_NOPE_DIM + C_ROPE_DIM) ** -0.5

    qa_ref[...] = (_rope(proj(OFF_QA, 512), c64, s64, 16) * (scale64 * LOG2E)).astype(BF16)
    ka_ref[0] = _rope(proj(OFF_KA, 512), c64, s64, 16).astype(BF16)
    _store_transposed(vta_ref, proj(OFF_VA, 512))

    xb = proj(OFF_QB, 512)
    yb = xb * lax.rsqrt(_group_mean_sq(xb, g512_ref) + EPS) * bqn_ref[...]
    qb_ref[...] = (_rope(yb, c64, s64, 16) * (scale64 * LOG2E)).astype(BF16)
    kvb = proj(OFF_KB, 256)
    xk = kvb[:, :LANES]
    yk = xk * lax.rsqrt(_group_mean_sq(xk, g128_ref) + EPS) * bkn_ref[...]
    kb_ref[0] = _dup_halves(_rope(yk, c128, s128, 16)).astype(BF16)
    _store_transposed(vtb_ref, kvb[:, LANES:])

    xq = proj(OFF_QC, C_Q_RANK)
    yq = xq * lax.rsqrt(jnp.mean(xq * xq, axis=-1, keepdims=True) + EPS) * cqn_ref[...]
    qf = jnp.dot(yq.astype(BF16), wq_ref[...], preferred_element_type=F32)
    qc_ref[...] = (_rope(qf, cq_ref[...], sq_ref[...], 8) * (scale_mla * LOG2E)).astype(BF16)
    xc = proj(OFF_CKV, C_KV_RANK)
    yc = (xc * lax.rsqrt(jnp.mean(xc * xc, axis=-1, keepdims=True) + EPS) * ckvn_ref[...]).astype(BF16)
    tail = proj(OFF_KR, IN_COLS - OFF_KR)
    kr = _rope(tail[:, :LANES], ck_ref[...], sk_ref[...], 8).astype(BF16)
    kc = (jnp.dot(yc, wkk_ref[...], preferred_element_type=F32)
          + jnp.dot(kr, place_ref[...], preferred_element_type=F32))
    kc_ref[0] = kc.astype(BF16)
    _store_transposed(vtc_ref, jnp.dot(yc, wkv_ref[...], preferred_element_type=F32))

    qd_ref[...] = (_rope(proj(OFF_QD, 512), c64, s64, 16) * scale64).astype(BF16)
    kd0, vd0 = OFF_KD - OFF_KR, OFF_VD - OFF_KR
    kd_ref[0] = _dup_halves(_rope(tail[:, kd0:kd0 + LANES], c128, s128, 16)).astype(BF16)
    vd_ref[0] = _dup_halves(tail[:, vd0:vd0 + LANES]).astype(BF16)


def _proj_prep(hs, g, mod, w_in_bf16, tables, consts, dims, mod_row):
    bsz, s_len, n_ctx = dims
    d = hs[0].shape[1]
    n_rows = sum(a.shape[0] for a in hs)
    sk = n_ctx + s_len
    tr = _pick(math.gcd(s_len, n_ctx), (TR, 128))
    n_lat_t, lat_pb, ctx_pb = bsz * s_len // tr, s_len // tr, n_ctx // tr

    def is_lat(t):
        return t < n_lat_t

    def tbl_idx(t):
        return jnp.where(is_lat(t), t % lat_pb, lat_pb)

    def kv_b(t):
        return jnp.where(is_lat(t), t // lat_pb, (t - n_lat_t) // ctx_pb)

    def kv_j(t):
        return jnp.where(is_lat(t), ctx_pb + t % lat_pb, (t - n_lat_t) % ctx_pb)

    def row_spec(w):
        return pl.BlockSpec((tr, w), lambda t: (t, 0))

    def tbl_spec(w):
        return pl.BlockSpec((tr, w), lambda t: (tbl_idx(t), 0))

    def const_spec(a):
        return pl.BlockSpec(a.shape, lambda t: (0,) * a.ndim)

    def kv_spec(w):
        return pl.BlockSpec((1, tr, w), lambda t: (kv_b(t), kv_j(t), 0))

    def vt_spec(w):
        return pl.BlockSpec((1, w, tr), lambda t: (kv_b(t), 0, kv_j(t)))

    q_widths = (512, 512, 1024, 512)
    kv_widths = (512, 256, 1024, 256, 256)
    vt_widths = (512, 128, 512)
    out_shape = ([jax.ShapeDtypeStruct((n_rows, w), BF16) for w in q_widths]
                 + [jax.ShapeDtypeStruct((bsz, sk, w), BF16) for w in kv_widths]
                 + [jax.ShapeDtypeStruct((bsz, w, sk), BF16) for w in vt_widths])
    out_specs = ([row_spec(w) for w in q_widths] + [kv_spec(w) for w in kv_widths]
                 + [vt_spec(w) for w in vt_widths])
    h_specs, mt_a = _stacked_specs(hs, (tr, d), tr, lambda idx: 0, 0)
    in_specs = (h_specs
                + [pl.BlockSpec((1, d), lambda t: (0, 0)),
                   pl.BlockSpec((1, 1, d), lambda t: (mod_row(t * tr), 0, 0)),
                   pl.BlockSpec((1, 1, d), lambda t: (mod_row(t * tr), 0, 1)),
                   pl.BlockSpec(w_in_bf16.shape, lambda t: (0, 0), pipeline_mode=pl.Buffered(1))]
                + [tbl_spec(t.shape[1]) for t in tables] + [const_spec(a) for a in consts])
    return pl.pallas_call(
        functools.partial(_proj_prep_kernel, n_h=len(hs), mt_a=mt_a),
        out_shape=out_shape,
        grid=(n_rows // tr,),
        in_specs=in_specs,
        out_specs=out_specs,
        scratch_shapes=[pltpu.VMEM((tr, d), BF16)],
        compiler_params=_cparams(("arbitrary",), VMEM_LIMIT_BIG),
        name="proj_prep",
    )(*hs, g.reshape(1, d), mod, mod, w_in_bf16, *tables, *consts)


def _scores(a, b):
    return lax.dot_general(a, b, (((1,), (1,)), ((), ())), preferred_element_type=F32)


def _half_masks(shape):
    lane = lax.broadcasted_iota(jnp.int32, shape, 1)
    lo = lane < HEAD_DIM
    return lo, jnp.logical_not(lo)


def _softmax_pv_t(s, v_t):
    m = jnp.max(s, axis=0, keepdims=True)
    e = jnp.exp2(s - m).astype(BF16)
    dv = v_t.shape[0]
    v_ext = jnp.concatenate([v_t, jnp.ones((BF16_SUBLANES, v_t.shape[1]), BF16)], axis=0)
    oe = jnp.dot(v_ext, e, preferred_element_type=F32)
    return oe[:dv] / oe[dv:dv + 1]


def _run_units(score_fns, finish_fns):
    s = score_fns[0]()
    for i, finish in enumerate(finish_fns):
        s_next = score_fns[i + 1]() if i + 1 < len(score_fns) else None
        finish(s)
        s = s_next


def _gqa_t_kernel(q_ref, k_ref, vt_ref, o_ref):
    masks = _half_masks((q_ref.shape[0], LANES))
    n_pairs = q_ref.shape[1] // LANES
    pairs_per_group = B_HEADS // B_KV_HEADS // 2
    outs = {}

    def score(j, z):
        g = j // pairs_per_group
        q = q_ref[:, j * LANES:(j + 1) * LANES]
        return lambda: _scores(k_ref[0, :, g * LANES:(g + 1) * LANES],
                               jnp.where(masks[z], q, jnp.zeros_like(q)))

    def finish(j, z):
        g = j // pairs_per_group

        def fin(s):
            outs[z] = _softmax_pv_t(s, vt_ref[0, g * HEAD_DIM:(g + 1) * HEAD_DIM, :])
            if z == 1:
                pair = jnp.concatenate([outs[0], outs[1]], axis=0)
                o_ref[:, j * LANES:(j + 1) * LANES] = pair.T.astype(o_ref.dtype)
        return fin

    units = [(j, z) for j in range(n_pairs) for z in range(2)]
    _run_units([score(j, z) for j, z in units], [finish(j, z) for j, z in units])


def _mla_t_kernel(q_ref, k_ref, vt_ref, o_ref):
    n_heads = q_ref.shape[1] // LANES
    outs = {}

    def score(u):
        return lambda: _scores(k_ref[0, :, u * LANES:(u + 1) * LANES], q_ref[:, u * LANES:(u + 1) * LANES])

    def finish(u):
        def fin(s):
            outs[u % 2] = _softmax_pv_t(s, vt_ref[0, u * C_V_DIM:(u + 1) * C_V_DIM, :])
            if u % 2 == 1:
                pair = jnp.concatenate([outs[0], outs[1]], axis=0)
                j = u // 2
                o_ref[:, j * LANES:(j + 1) * LANES] = pair.T.astype(o_ref.dtype)
        return fin

    _run_units([score(u) for u in range(n_heads)], [finish(u) for u in range(n_heads)])


def _diff_t_kernel(lam_ref, subln_ref, q0_ref, q1_ref, k0_ref, k1_ref, vt_ref, o_ref, *, lam_init):
    t = lam_ref[...]
    lam = (jnp.exp(jnp.sum(t[0:1] * t[1:2], axis=-1, keepdims=True))
           - jnp.exp(jnp.sum(t[2:3] * t[3:4], axis=-1, keepdims=True)) + lam_init)
    q_refs, k_refs = (q0_ref, q1_ref), (k0_ref, k1_ref)
    masks = _half_masks((q0_ref.shape[0], LANES))
    first_map = {}

    def score(z, mp):
        j = z // 2
        q = q_refs[mp][:, j * LANES:(j + 1) * LANES]
        return lambda: _scores(k_refs[mp][0, :, j * LANES:(j + 1) * LANES],
                               jnp.where(masks[z % 2], q, jnp.zeros_like(q)))

    def finish(z, mp):
        def fin(s):
            a = _softmax_pv_t(s, vt_ref[0, z * A_V_DIM:(z + 1) * A_V_DIM, :])
            if mp == 0:
                first_map[z] = a
                return
            d = (first_map[z] - lam * a).T
            y = d * lax.rsqrt(jnp.mean(d * d, axis=-1, keepdims=True) + EPS) * subln_ref[...]
            o_ref[:, z * A_V_DIM:(z + 1) * A_V_DIM] = (y * (1.0 - lam_init)).astype(o_ref.dtype)
        return fin

    units = [(z, mp) for z in range(A_HEADS) for mp in range(2)]
    _run_units([score(z, mp) for z, mp in units], [finish(z, mp) for z, mp in units])


def _attend_sink(q, k, v, sink):
    s = _scores(q, k)
    m = jnp.maximum(jnp.max(s, axis=-1, keepdims=True), sink)
    e = jnp.exp(s - m)
    l = jnp.sum(e, axis=-1, keepdims=True) + jnp.exp(sink - m)
    return jnp.dot(e.astype(BF16), v, preferred_element_type=F32) / l


def _window_kernel(sink_ref, q_ref, k_ref, v_ref, o_ref, *, s_len, n_ctx, n_q_lat, with_ctx):
    tq = q_ref.shape[0]
    band = tq + 2 * WINDOW
    i = pl.program_id(1)
    pairs_per_group = D_HEADS // D_KV_HEADS // 2

    def body(is_lat):
        masks = _half_masks((tq, LANES))
        if is_lat:
            start = pl.multiple_of(jnp.clip(i * tq - WINDOW, 0, s_len - band), LANES)
            row0 = pl.multiple_of(n_ctx + start, LANES)
            qpos = i * tq + lax.broadcasted_iota(jnp.int32, (tq, n_ctx + band), 0)
            col = lax.broadcasted_iota(jnp.int32, (tq, n_ctx + band), 1)
            visible = (col < n_ctx) | (jnp.abs(qpos - (start + col - n_ctx)) <= WINDOW)
        kv = {}
        for j in range(q_ref.shape[1] // LANES):
            g = j // pairs_per_group
            if g not in kv:
                lanes = slice(g * LANES, (g + 1) * LANES)
                k, v = k_ref[0, :n_ctx, lanes], v_ref[0, :n_ctx, lanes]
                if is_lat:
                    k = jnp.concatenate([k, k_ref[0, pl.ds(row0, band), lanes]], axis=0)
                    v = jnp.concatenate([v, v_ref[0, pl.ds(row0, band), lanes]], axis=0)
                kv[g] = (k, v)
            k, v = kv[g]
            q = q_ref[:, j * LANES:(j + 1) * LANES]
            outs = []
            for z in range(2):
                qz = jnp.where(masks[z], q, jnp.zeros_like(q))
                sink = sink_ref[2 * j + z]
                if not is_lat:
                    outs.append(_attend_sink(qz, k, v, sink))
                    continue
                s = jnp.where(visible, _scores(qz, k), -jnp.inf)
                m = jnp.maximum(jnp.max(s, axis=-1, keepdims=True), sink)
                e = jnp.exp(s - m)
                l = jnp.sum(e, axis=-1, keepdims=True) + jnp.exp(sink - m)
                outs.append(jnp.dot(e.astype(BF16), v, preferred_element_type=F32) / l)
            o_ref[:, j * LANES:(j + 1) * LANES] = jnp.where(masks[0], outs[0], outs[1]).astype(o_ref.dtype)

    if not with_ctx:
        body(True)
        return

    @pl.when(i < n_q_lat)
    def _():
        body(True)

    @pl.when(i >= n_q_lat)
    def _():
        body(False)


def _mixers(ops, dims, params, lam_init, with_ctx):
    bsz, s_len, n_ctx = dims
    qa, qb, qc, qd, ka, kb, kc, kd, vd, vta, vtb, vtc = ops
    lam_vecs, subln, sink = params
    n_lat = bsz * s_len
    sk_all = n_ctx + s_len

    def const_spec(a):
        return pl.BlockSpec(a.shape, lambda b, i: (0,) * a.ndim)

    def dense_set(tq, nk, row0_tiles, n_q, out_rows):
        def q_row(b, i):
            return row0_tiles + b * n_q + i

        def qspec(w, col=0):
            return pl.BlockSpec((tq, w), lambda b, i: (q_row(b, i), col))

        def kspec(w, col=0):
            return pl.BlockSpec((1, nk, w), lambda b, i: (b, 0, col))

        def vtspec(w):
            return pl.BlockSpec((1, w, nk), lambda b, i: (b, 0, 0))

        def call(kernel, in_specs, args, name):
            return pl.pallas_call(
                kernel,
                out_shape=jax.ShapeDtypeStruct((out_rows, 512), BF16),
                grid=(bsz, n_q),
                in_specs=in_specs,
                out_specs=pl.BlockSpec((tq, 512), lambda b, i: (b * n_q + i, 0)),
                compiler_params=_cparams(("arbitrary", "arbitrary"), VMEM_LIMIT_BIG),
                name=name,
            )(*args)

        oa = call(functools.partial(_diff_t_kernel, lam_init=lam_init),
                  [const_spec(lam_vecs), const_spec(subln), qspec(256, 0), qspec(256, 1),
                   kspec(256, 0), kspec(256, 1), vtspec(512)],
                  [lam_vecs, subln, qa, qa, ka, ka, vta], "attn_diff")
        ob = call(_gqa_t_kernel, [qspec(512), kspec(256), vtspec(128)], [qb, kb, vtb], "attn_qknorm")
        oc = call(_mla_t_kernel, [qspec(1024), kspec(1024), vtspec(512)], [qc, kc, vtc], "attn_mla")
        return [oa, ob, oc]

    tq = _pick(s_len, (TQ, 256, 128))
    dense = [[o] for o in dense_set(tq, sk_all, 0, s_len // tq, n_lat)]
    if with_ctx:
        tq_c = _pick(n_ctx, (TQ, 256, 128))
        ctx = dense_set(tq_c, n_ctx, n_lat // tq_c, n_ctx // tq_c, bsz * n_ctx)
        dense = [a + [b] for a, b in zip(dense, ctx)]

    tw = _pick(math.gcd(s_len, n_ctx), (TQ_WIN, 128))
    n_q_lat, n_q_ctx = s_len // tw, n_ctx // tw
    n_q = n_q_lat + n_q_ctx if with_ctx else n_q_lat
    out_rows = n_lat + bsz * n_ctx if with_ctx else n_lat

    def w_row(b, i):
        return jnp.where(i < n_q_lat, b * n_q_lat + i, n_lat // tw + b * n_q_ctx + (i - n_q_lat))

    od = pl.pallas_call(
        functools.partial(_window_kernel, s_len=s_len, n_ctx=n_ctx, n_q_lat=n_q_lat, with_ctx=with_ctx),
        out_shape=jax.ShapeDtypeStruct((out_rows, 512), BF16),
        grid=(bsz, n_q),
        in_specs=[pl.BlockSpec(memory_space=pltpu.SMEM),
                  pl.BlockSpec((tw, 512), lambda b, i: (w_row(b, i), 0)),
                  pl.BlockSpec((1, sk_all, 256), lambda b, i: (b, 0, 0)),
                  pl.BlockSpec((1, sk_all, 256), lambda b, i: (b, 0, 0))],
        out_specs=pl.BlockSpec((tw, 512), lambda b, i: (w_row(b, i), 0)),
        compiler_params=_cparams(("arbitrary", "arbitrary")),
        name="attn_window",
    )(sink, qd, kd, vd)
    return dense + [[od]]


def _route_kernel(h_ref, g_ref, sh_ref, sc_ref, w_ref, b_ref, u_ref, sel_ref, idx_ref, gw_ref):
    u = _rms_modulate(h_ref[...], g_ref[...], sc_ref[0], sh_ref[0])
    u_ref[...] = _pack_bf16_pairs(u)
    w = w_ref[...]
    u_hi = u.astype(BF16)
    u_lo = (u - u_hi.astype(F32)).astype(BF16)
    w_hi = w.astype(BF16)
    w_lo = (w - w_hi.astype(F32)).astype(BF16)
    logits = (_scores(w_hi, u_hi) + _scores(w_hi, u_lo) + _scores(w_lo, u_hi) + _scores(w_lo, u_lo)
              + b_ref[...])
    ids = lax.broadcasted_iota(jnp.int32, logits.shape, 0).astype(F32)
    m1 = jnp.max(logits, axis=0, keepdims=True)
    i1 = jnp.min(jnp.where(logits == m1, ids, float(N_EXPERTS)), axis=0, keepdims=True)
    first = ids == i1
    rest = jnp.where(first, -jnp.inf, logits)
    m2 = jnp.max(rest, axis=0, keepdims=True)
    i2 = jnp.min(jnp.where(rest == m2, ids, float(N_EXPERTS)), axis=0, keepdims=True)
    second = ids == i2
    e = jnp.exp(m2 - m1)
    w1 = 1.0 / (1.0 + e)
    w2 = e / (1.0 + e)
    sel_ref[...] = jnp.where(first | second, 1, 0).astype(jnp.int32)
    idx_ref[...] = jnp.where(ids == 0.0, i1, jnp.where(ids == 1.0, i2, 0.0)).astype(jnp.int32)
    gw_ref[...] = jnp.where(ids == 0.0, w1, jnp.where(ids == 1.0, w2, 0.0))


def _route(h, g, mod, sh_blk, sc_blk, w_router_t, b_router, n_tok, s_len, mod_row):
    d = h.shape[1]
    tm = _row_tile(n_tok, s_len, (TM, 256, 128))
    outs = ([jax.ShapeDtypeStruct((n_tok, d // 2), jnp.uint32)]
            + [jax.ShapeDtypeStruct((N_EXPERTS, n_tok), dt) for dt in (jnp.int32, jnp.int32, F32)])
    return pl.pallas_call(
        _route_kernel,
        out_shape=outs,
        grid=(n_tok // tm,),
        in_specs=[pl.BlockSpec((tm, d), lambda m: (m, 0)),
                  pl.BlockSpec((1, d), lambda m: (0, 0)),
                  pl.BlockSpec((1, 1, d), lambda m: (mod_row(m * tm), 0, sh_blk)),
                  pl.BlockSpec((1, 1, d), lambda m: (mod_row(m * tm), 0, sc_blk)),
                  pl.BlockSpec((N_EXPERTS, d), lambda m: (0, 0)),
                  pl.BlockSpec((N_EXPERTS, 1), lambda m: (0, 0))],
        out_specs=[pl.BlockSpec((tm, d // 2), lambda m: (m, 0))]
        + [pl.BlockSpec((N_EXPERTS, tm), lambda m: (0, m))] * 3,
        compiler_params=_cparams(("arbitrary",)),
        name="moe_route",
    )(h, g.reshape(1, d), mod, mod, w_router_t, b_router.reshape(N_EXPERTS, 1))


def _row_copy(src, dst, sem, src_row, dst_row):
    return pltpu.make_async_copy(src.at[pl.ds(src_row, 1)], dst.at[pl.ds(dst_row, 1)], sem)


def _gather_kernel(tok_ref, nxt_ref, src_ref, o_ref, buf, sem):
    i = pl.program_id(0)
    n = pl.num_programs(0)
    tg = buf.shape[1]
    slot = i % 2

    def issue(t_ref, s):
        def body(r, carry):
            _row_copy(src_ref, buf.at[s], sem.at[s], t_ref[0, 0, r], r).start()
            return carry
        lax.fori_loop(0, tg, body, 0, unroll=DMA_UNROLL)

    @pl.when(i == 0)
    def _():
        issue(tok_ref, 0)

    @pl.when(i + 1 < n)
    def _():
        issue(nxt_ref, 1 - slot)

    def wait(r, carry):
        _row_copy(src_ref, buf.at[slot], sem.at[slot], 0, r).wait()
        return carry

    lax.fori_loop(0, tg, wait, 0, unroll=DMA_UNROLL)
    lo, hi = _unpack_bf16_pairs(buf[slot])
    half = lo.shape[1]
    o_ref[:, :half] = lo.astype(o_ref.dtype)
    o_ref[:, half:] = hi.astype(o_ref.dtype)


def _gather_rows(src, tok, tg):
    n_rows = tok.shape[0]
    dw = src.shape[1]
    d = 2 * dw
    n_t = n_rows // tg
    tok = tok.reshape(n_t, 1, tg)
    return pl.pallas_call(
        _gather_kernel,
        out_shape=jax.ShapeDtypeStruct((n_rows, d), BF16),
        grid=(n_t,),
        in_specs=[pl.BlockSpec((1, 1, tg), lambda i: (i, 0, 0), memory_space=pltpu.SMEM),
                  pl.BlockSpec((1, 1, tg), lambda i: (jnp.minimum(i + 1, n_t - 1), 0, 0),
                               memory_space=pltpu.SMEM),
                  pl.BlockSpec(memory_space=pl.ANY)],
        out_specs=pl.BlockSpec((tg, d), lambda i: (i, 0)),
        scratch_shapes=[pltpu.VMEM((2, tg, dw), jnp.uint32), pltpu.SemaphoreType.DMA((2,))],
        compiler_params=_cparams(("arbitrary",)),
        name="moe_gather",
    )(tok, tok, src)


def _combine_kernel(pos_ref, nxt_ref, y_ref, h_ref, gw_ref, gt_ref, gf_ref, o_ref, buf, sem, *, pack_tile):
    i = pl.program_id(0)
    n = pl.num_programs(0)
    tc = buf.shape[2]
    slot = i % 2

    def issue(p_ref, s):
        def body(r, carry):
            _row_copy(y_ref, buf.at[s, 0], sem.at[s], p_ref[0, 0, r], r).start()
            _row_copy(y_ref, buf.at[s, 1], sem.at[s], p_ref[0, 1, r], r).start()
            return carry
        lax.fori_loop(0, tc, body, 0, unroll=DMA_UNROLL)

    @pl.when(i == 0)
    def _():
        issue(pos_ref, 0)

    @pl.when(i + 1 < n)
    def _():
        issue(nxt_ref, 1 - slot)

    def wait(r, carry):
        _row_copy(y_ref, buf.at[slot, 0], sem.at[slot], 0, r).wait()
        _row_copy(y_ref, buf.at[slot, 1], sem.at[slot], 0, r).wait()
        return carry

    lax.fori_loop(0, tc, wait, 0, unroll=DMA_UNROLL)
    def expert_rows(k):
        lo, hi = _unpack_bf16_pairs(buf[slot, k])
        hw = pack_tile // 2
        cols = []
        for j in range(lo.shape[1] // hw):
            cols += [lo[:, j * hw:(j + 1) * hw], hi[:, j * hw:(j + 1) * hw]]
        return jnp.concatenate(cols, axis=-1)

    gw = gw_ref[...]
    moe = gw[:, 0:1] * expert_rows(0) + gw[:, 1:2] * expert_rows(1)
    x = h_ref[...] + gt_ref[0] * moe
    ms = jnp.mean(x * x, axis=-1, keepdims=True)
    o_ref[...] = x * lax.rsqrt(ms + EPS) * gf_ref[...]


def _combine(y, h, pos, gw_t, mod, gt_blk, mod_row, g_final, n_tok, pack_tile):
    d = h.shape[1]
    n_t, _, tc = pos.shape
    return pl.pallas_call(
        functools.partial(_combine_kernel, pack_tile=pack_tile),
        out_shape=jax.ShapeDtypeStruct((n_tok, d), F32),
        grid=(n_t,),
        in_specs=[pl.BlockSpec((1, 2, tc), lambda i: (i, 0, 0), memory_space=pltpu.SMEM),
                  pl.BlockSpec((1, 2, tc), lambda i: (jnp.minimum(i + 1, n_t - 1), 0, 0),
                               memory_space=pltpu.SMEM),
                  pl.BlockSpec(memory_space=pl.ANY),
                  pl.BlockSpec((tc, d), lambda i: (i, 0)),
                  pl.BlockSpec((tc, N_EXPERTS), lambda i: (i, 0)),
                  pl.BlockSpec((1, 1, d), lambda i: (mod_row(i * tc), 0, gt_blk)),
                  pl.BlockSpec((1, d), lambda i: (0, 0))],
        out_specs=pl.BlockSpec((tc, d), lambda i: (i, 0)),
        scratch_shapes=[pltpu.VMEM((2, 2, tc, d // 2), jnp.uint32), pltpu.SemaphoreType.DMA((2,))],
        compiler_params=_cparams(("arbitrary",)),
        name="moe_combine",
    )(pos, pos, y, h, gw_t, mod, g_final.reshape(1, d))


def _dispatch_plan(sel, idx, tm, n_slots, tc):
    n_tok = sel.shape[1]
    n_tiles = n_slots // tm
    counts = jnp.sum(sel, axis=1)
    padded = ((counts + tm - 1) // tm) * tm
    ends = jnp.cumsum(padded)
    offs = ends - padded
    pos = offs[:, None] + jnp.cumsum(sel, axis=1) - sel
    pos0 = jnp.take_along_axis(pos, idx[0:1], axis=0)[0]
    pos1 = jnp.take_along_axis(pos, idx[1:2], axis=0)[0]
    pos_tiles = jnp.stack([pos0.reshape(n_tok // tc, tc), pos1.reshape(n_tok // tc, tc)], axis=1)
    tile_start = jnp.arange(n_tiles, dtype=jnp.int32) * tm
    tile_expert = jnp.minimum(jnp.sum(ends[None, :] <= tile_start[:, None], axis=1), N_EXPERTS - 1)
    n_used = ends[-1] // tm
    tok = jnp.arange(n_tok, dtype=jnp.int32)
    tok_of_row = jnp.zeros((n_slots,), jnp.int32).at[jnp.concatenate([pos0, pos1])].set(
        jnp.concatenate([tok, tok]), unique_indices=True)
    return (pos_tiles.astype(jnp.int32), tok_of_row, tile_expert.astype(jnp.int32),
            n_used.reshape(1).astype(jnp.int32))


def _rope_tables(s_len, pad_rows):
    f32 = np.float32
    t = np.arange(s_len)
    rows, cols = (t // GRID_W).astype(f32), (t % GRID_W).astype(f32)

    def axis_tables(rot_dim):
        axis_dim = rot_dim // 2
        inv = (f32(ROPE_THETA) ** (-np.arange(0, axis_dim, 2, dtype=f32) / f32(axis_dim))).astype(f32)
        ar, ac = rows[:, None] * inv[None, :], cols[:, None] * inv[None, :]
        cos = np.concatenate([np.cos(ar), np.cos(ar), np.cos(ac), np.cos(ac)], axis=1)
        sin = np.concatenate([-np.sin(ar), np.sin(ar), -np.sin(ac), np.sin(ac)], axis=1)
        return cos.astype(f32), sin.astype(f32)

    def with_identity(cos, sin):
        w = cos.shape[1]
        return (np.concatenate([cos, np.ones((pad_rows, w), f32)], axis=0),
                np.concatenate([sin, np.zeros((pad_rows, w), f32)], axis=0))

    c64, s64 = axis_tables(HEAD_DIM)
    c64, s64 = np.tile(c64, (1, 2)), np.tile(s64, (1, 2))
    c32, s32 = axis_tables(C_ROPE_DIM)
    ones, zeros = np.ones((s_len, 1), f32), np.zeros((s_len, 1), f32)
    cq = np.concatenate([np.tile(ones, (1, 64)), c32, np.tile(ones, (1, 32))], axis=1)
    sq = np.concatenate([np.tile(zeros, (1, 64)), s32, np.tile(zeros, (1, 32))], axis=1)
    ck = np.concatenate([c32, np.tile(ones, (1, 96))], axis=1)
    sk = np.concatenate([s32, np.tile(zeros, (1, 96))], axis=1)
    out = []
    for c, s in ((c64, s64), (cq, sq), (ck, sk)):
        out += [jnp.asarray(a) for a in with_identity(c, s)]
    return out


def _mla_weights(w_q_up, w_kv_up):
    qd = C_NOPE_DIM + C_ROPE_DIM
    wq = jnp.pad(w_q_up.reshape(C_Q_RANK, C_HEADS, qd), ((0, 0), (0, 0), (0, LANES - qd)))
    wkv = w_kv_up.reshape(C_KV_RANK, C_HEADS, C_NOPE_DIM + C_V_DIM)
    wkk = jnp.pad(wkv[:, :, :C_NOPE_DIM], ((0, 0), (0, 0), (0, LANES - C_NOPE_DIM)))
    wkv_v = wkv[:, :, C_NOPE_DIM:]
    return (wq.reshape(C_Q_RANK, C_HEADS * LANES).astype(BF16),
            wkk.reshape(C_KV_RANK, C_HEADS * LANES).astype(BF16),
            wkv_v.reshape(C_KV_RANK, C_HEADS * C_V_DIM).astype(BF16))


def _static_mats():
    g = (np.arange(512)[:, None] // HEAD_DIM == np.arange(512)[None, :] // HEAD_DIM) / HEAD_DIM
    place = np.zeros((LANES, C_HEADS * LANES), np.float32)
    for h in range(C_HEADS):
        place[np.arange(C_ROPE_DIM), h * LANES + C_NOPE_DIM + np.arange(C_ROPE_DIM)] = 1.0
    return (jnp.asarray(g, BF16), jnp.asarray(g[:128, :128], BF16), jnp.asarray(place, BF16))


def kernel(x, c, ctx, c_ctx, w_mod, b_mod, g_mix, g_ffn, g_final, w_in, w_out, a_lam_q1, a_lam_k1, a_lam_q2, a_lam_k2, a_subln, b_q_norm, b_k_norm, c_q_norm, c_kv_norm, c_w_q_up, c_w_kv_up, d_sink, ffn_w_gate, ffn_w_up, ffn_w_down, moe_w_router, moe_b_router, moe_w_gate, moe_w_up, moe_w_down):
    bsz, s_len, d = x.shape
    n_ctx = ctx.shape[1]
    depth = w_mod.shape[0]
    n_lat = bsz * s_len
    n_all = n_lat + bsz * n_ctx
    dims = (bsz, s_len, n_ctx)

    def mod_row(row0):
        return jnp.where(row0 < n_lat, row0 // s_len, bsz)

    mod_rows = -(-(bsz + 1) // 8) * 8
    cc = jnp.zeros((mod_rows, d), F32).at[:bsz].set(c).at[bsz].set(c_ctx)
    mods = _modulation(cc, w_mod, b_mod)

    tr = _pick(math.gcd(s_len, n_ctx), (TR, 128))
    tables = _rope_tables(s_len, tr)
    g512, g128, place = _static_mats()

    h = [x.reshape(n_lat, d), ctx.reshape(bsz * n_ctx, d)]
    for l in range(depth):
        last = l == depth - 1
        lam_init = 0.8 - 0.6 * math.exp(-0.3 * l)
        n_rows = n_lat if last else n_all
        mod = mods[l].reshape(mod_rows, 1, 6 * d)
        nm = dict(s_len=s_len, mod_row=mod_row)

        wq, wkk, wkv_v = _mla_weights(c_w_q_up[l], c_w_kv_up[l])
        consts = [jnp.tile(b_q_norm[l], 8)[None], jnp.tile(b_k_norm[l], 2)[None],
                  c_q_norm[l][None], c_kv_norm[l][None], g512, g128, wq, wkk, wkv_v, place]
        ops = _proj_prep(h, g_mix[l], mod, w_in[l].astype(BF16), tables, consts, dims, mod_row)
        lam_vecs = jnp.stack([a_lam_q1[l], a_lam_k1[l], a_lam_q2[l], a_lam_k2[l]])
        mix = _mixers(ops, dims, (lam_vecs, a_subln[l][None], d_sink[l]), lam_init, not last)
        h = [_matmul(mix, [w_out[l][None]], n_rows=n_rows, out_dtype=F32, epi="resgate",
                     res=h, mod=mod, gt_blk=2, mod_row=mod_row, tn=1024)]

        i = l // 2
        if l % 2 == 0:
            mid = _nm_matmul(h, g_ffn[l], mod, 3, 4, [ffn_w_gate[i], ffn_w_up[i]], n_rows=n_rows,
                             out_dtype=BF16, **nm)
            h = [_matmul([[mid]], [ffn_w_down[i][None]], n_rows=n_rows, out_dtype=F32, epi="resgate",
                         res=h, mod=mod, gt_blk=5, mod_row=mod_row)]
        else:
            if not last:
                raise NotImplementedError("expert layers are only supported as the last layer")
            h = h[0]
            u2, sel, idx, gw = _route(h, g_ffn[l], mod, 3, 4, moe_w_router[i].T, moe_b_router[i],
                                      n_rows, s_len, mod_row)
            tm = _pick(n_rows, (TM, 256, 128))
            tc = _pick(s_len, (TC, 128))
            n_slots = 2 * n_rows + N_EXPERTS * tm
            pos, tok_of_row, tile_expert, n_used = _dispatch_plan(sel, idx, tm, n_slots, tc)
            xs = _gather_rows(u2, tok_of_row, tm)
            mid = _matmul([[xs]], [moe_w_gate[i], moe_w_up[i]], n_rows=n_slots, out_dtype=BF16,
                          epi="swiglu", tile_expert=tile_expert, n_used=n_used)
            y = _matmul([[mid]], [moe_w_down[i]], n_rows=n_slots, out_dtype=jnp.uint32, epi="packed",
                        tile_expert=tile_expert, n_used=n_used)
            out = _combine(y, h, pos, gw.T, mod, 5, mod_row, g_final, n_rows, _pick(d, (TN, 256, 128)))
            return out.reshape(bsz, s_len, d)

    tm = _pick(n_lat, (TM, 256, 128))
    out = pl.pallas_call(
        _final_norm_kernel,
        out_shape=jax.ShapeDtypeStruct((n_lat, d), F32),
        grid=(n_lat // tm,),
        in_specs=[pl.BlockSpec((tm, d), lambda m: (m, 0)), pl.BlockSpec((1, d), lambda m: (0, 0))],
        out_specs=pl.BlockSpec((tm, d), lambda m: (m, 0)),
        compiler_params=_cparams(("arbitrary",)),
        name="final_norm",
    )(h[0], g_final.reshape(1, d))
    return out.reshape(bsz, s_len, d)
```
